```python
import jax, jax.numpy as jnp
from jax import lax
import numpy as np

D_MODEL = 1024
BATCH = 16
SEQ = 2048
DEPTH = 2
DEC_BATCH = 128
DEC_SEQ = 8
PAST_LEN = 16384
PAGE_SIZE = 128

MIX_W = D_MODEL
ATTN_W = MIX_W // 2
HEAD_DIM = 64
N_HEADS = ATTN_W // HEAD_DIM
N_KV_HEADS = 2
GROUP = N_HEADS // N_KV_HEADS
KV_W = N_KV_HEADS * HEAD_DIM
LRU_W = MIX_W - ATTN_W
N_LRU_BLOCKS = 8
LRU_BLOCK = LRU_W // N_LRU_BLOCKS
CONV_WIDTH = 4
RG_C = 8.0
WINDOW = 128
Q_BLOCK = 128
D_FF = ((8 * D_MODEL // 3 + 127) // 128) * 128
N_MOD = 9
FFN_RES = 0.5
IN_COLS = ATTN_W + 2 * KV_W + 2 * LRU_W
SPLITS = (ATTN_W, ATTN_W + KV_W, ATTN_W + 2 * KV_W, ATTN_W + 2 * KV_W + LRU_W)
RMS_EPS = 1e-6
NEG_INF = -1e30

kernel_name = 'hymba_rglru_swa_sink_alibi_macaron_adaln_step'


def rms_norm(x, eps=RMS_EPS):
    xf = x.astype(jnp.float32)
    return (xf * lax.rsqrt(jnp.mean(xf * xf, axis=-1, keepdims=True) + eps)).astype(x.dtype)


def modulate(x, shift, scale):
    return rms_norm(x) * (1 + scale) + shift


def swiglu(h, w_gate, w_up, w_down):
    return (jax.nn.silu(h @ w_gate) * (h @ w_up)) @ w_down


def alibi_slopes():
    return jnp.asarray([2.0 ** (-8.0 * (hh + 1) / N_HEADS) for hh in range(N_HEADS)], dtype=jnp.float32)


def band_attention(q, k, v, dist, valid, sinks):
    b, nb, tq = q.shape[:3]
    qg = q.reshape(b, nb, tq, N_KV_HEADS, GROUP, HEAD_DIM)
    s = jnp.einsum('bnqkgd,bnskd->bnkgqs', qg, k, preferred_element_type=jnp.float32) * (HEAD_DIM ** -0.5)
    slopes = alibi_slopes().reshape(N_KV_HEADS, GROUP, 1, 1)
    s = s - slopes * dist.astype(jnp.float32)
    s = jnp.where(valid[:, None, None], s, NEG_INF)
    sink = jnp.broadcast_to(sinks.astype(jnp.float32).reshape(N_KV_HEADS, GROUP, 1, 1), s.shape[:-1] + (1,))
    p = jax.nn.softmax(jnp.concatenate([s, sink], axis=-1), axis=-1)[..., :-1]
    o = jnp.einsum('bnkgqs,bnskd->bnqkgd', p.astype(v.dtype), v)
    return o.reshape(b, nb * tq, N_HEADS * HEAD_DIM)


def prompt_window_attention(q, k, v, sinks):
    b, t = q.shape[:2]
    nb = t // Q_BLOCK
    qb = q.reshape(b, nb, Q_BLOCK, N_HEADS, HEAD_DIM)

    def with_prev(z):
        zb = z.reshape(b, nb, Q_BLOCK, N_KV_HEADS, HEAD_DIM)
        prev = jnp.concatenate([jnp.zeros_like(zb[:, :1]), zb[:, :-1]], axis=1)
        return jnp.concatenate([prev, zb], axis=2)

    i = jnp.arange(Q_BLOCK)[:, None]
    j = jnp.arange(2 * Q_BLOCK)[None, :]
    dist = Q_BLOCK + i - j
    band = (dist >= 0) & (dist <= WINDOW)
    no_prev = (jnp.arange(nb)[:, None, None] == 0) & (j[None] < Q_BLOCK)
    valid = band[None] & ~no_prev
    o = band_attention(qb, with_prev(k), with_prev(v), dist, valid, sinks)
    return o, k[:, t - WINDOW:], v[:, t - WINDOW:]


def sample_window_attention(q, k, v, k_buf, v_buf, sinks):
    tq = q.shape[1]
    n_buf = k_buf.shape[1]
    kk = jnp.concatenate([k_buf, k], axis=1)
    vv = jnp.concatenate([v_buf, v], axis=1)
    i = jnp.arange(tq)[:, None]
    j = jnp.arange(n_buf + tq)[None, :]
    dist = n_buf + i - j
    valid = ((dist >= 0) & (dist <= WINDOW))[None]
    o = band_attention(q[:, None], kk[:, None], vv[:, None], dist, valid, sinks)
    return o, kk[:, tq:], vv[:, tq:]


def causal_conv(x, buf, w, b):
    t = x.shape[1]
    xp = jnp.concatenate([buf, x], axis=1)
    y = b + xp[:, 0:t] * w[0]
    for tap in range(1, CONV_WIDTH):
        y = y + xp[:, tap:tap + t] * w[tap]
    return y, xp[:, t:]


def rg_lru(xc, h0, w_r, b_r, w_i, b_i, lam):
    b, t, w = xc.shape
    xb = xc.reshape(b, t, N_LRU_BLOCKS, LRU_BLOCK)
    r = jax.nn.sigmoid(jnp.einsum('btnc,ncd->btnd', xb, w_r).reshape(b, t, w) + b_r)
    gi = jax.nn.sigmoid(jnp.einsum('btnc,ncd->btnd', xb, w_i).reshape(b, t, w) + b_i)
    log_a = -RG_C * r.astype(jnp.float32) * jax.nn.softplus(-lam.astype(jnp.float32))
    a = jnp.exp(log_a)
    u = jnp.sqrt(-jnp.expm1(2.0 * log_a)) * (gi * xc).astype(jnp.float32)

    def step(h, au):
        a_t, u_t = au
        h = a_t * h + u_t
        return h, h

    h_last, hs = lax.scan(step, h0.astype(jnp.float32), (jnp.swapaxes(a, 0, 1), jnp.swapaxes(u, 0, 1)))
    return jnp.swapaxes(hs, 0, 1).astype(xc.dtype), h_last.astype(xc.dtype)


def layer(x, c, conv_buf, h0, k_buf, v_buf,
          w_ada, b_ada, w1_gate, w1_up, w1_down, w_in, q_gain, k_gain, sinks,
          conv_w, conv_b, w_rg, b_rg, w_ig, b_ig, lru_lambda, beta_attn, beta_lru,
          w_out, w2_gate, w2_up, w2_down):
    b, t, _ = x.shape
    mod = (jax.nn.silu(c) @ w_ada + b_ada)[:, None, :]
    sh1, sc1, g1, sh2, sc2, g2, sh3, sc3, g3 = jnp.split(mod, N_MOD, axis=-1)

    x = x + FFN_RES * g1 * swiglu(modulate(x, sh1, sc1), w1_gate, w1_up, w1_down)

    h = modulate(x, sh2, sc2)
    z = h @ w_in
    q, k, v, xr, gr = jnp.split(z, SPLITS, axis=-1)
    q = rms_norm(q.reshape(b, t, N_HEADS, HEAD_DIM)) * q_gain
    k = rms_norm(k.reshape(b, t, N_KV_HEADS, HEAD_DIM)) * k_gain
    v = v.reshape(b, t, N_KV_HEADS, HEAD_DIM)
    if k_buf is None:
        attn, new_k, new_v = prompt_window_attention(q, k, v, sinks)
        conv_buf = jnp.zeros((b, CONV_WIDTH - 1, LRU_W), x.dtype)
        h0 = jnp.zeros((b, LRU_W), x.dtype)
    else:
        attn, new_k, new_v = sample_window_attention(q, k, v, k_buf, v_buf, sinks)
    xc, new_conv = causal_conv(xr, conv_buf, conv_w, conv_b)
    hs, h_last = rg_lru(xc, h0, w_rg, b_rg, w_ig, b_ig, lru_lambda)
    lru = hs * jax.nn.gelu(gr)
    merged = jnp.concatenate([rms_norm(attn) * beta_attn, rms_norm(lru) * beta_lru], axis=-1)
    x = x + g2 * (merged @ w_out)

    x = x + FFN_RES * g3 * swiglu(modulate(x, sh3, sc3), w2_gate, w2_up, w2_down)
    return x, new_k, new_v, new_conv, h_last


def setup_inputs(seed: int = 0) -> dict:
    key = jax.random.key(seed)
    ks = list(jax.random.split(key, 40))

    def nrm(shape, scale):
        return jax.random.normal(ks.pop(), shape, jnp.float32) * scale

    u = jax.random.uniform(ks.pop(), (DEPTH, LRU_W), jnp.float32, minval=0.9, maxval=0.999)
    a_base = u ** (1.0 / RG_C)
    lru_lambda = jnp.log(a_base) - jnp.log1p(-a_base)
    return {
        'x_prompt': nrm((BATCH, SEQ, D_MODEL), 1.0),
        'x_sample': nrm((DEC_BATCH, DEC_SEQ, D_MODEL), 1.0),
        'cache_k': nrm((DEPTH, DEC_BATCH, WINDOW, N_KV_HEADS, HEAD_DIM), 1.0),
        'cache_v': nrm((DEPTH, DEC_BATCH, WINDOW, N_KV_HEADS, HEAD_DIM), 1.0),
        'state_conv': nrm((DEPTH, DEC_BATCH, CONV_WIDTH - 1, LRU_W), 1.0),
        'state_lru': nrm((DEPTH, DEC_BATCH, LRU_W), 0.5),
        'c_prompt': nrm((BATCH, D_MODEL), 1.0),
        'c_sample': nrm((DEC_BATCH, D_MODEL), 1.0),
        'w_ada': nrm((DEPTH, D_MODEL, N_MOD * D_MODEL), 0.02),
        'b_ada': nrm((DEPTH, N_MOD * D_MODEL), 0.02),
        'w1_gate': nrm((DEPTH, D_MODEL, D_FF), D_MODEL ** -0.5),
        'w1_up': nrm((DEPTH, D_MODEL, D_FF), D_MODEL ** -0.5),
        'w1_down': nrm((DEPTH, D_FF, D_MODEL), D_FF ** -0.5),
        'w_in': nrm((DEPTH, D_MODEL, IN_COLS), D_MODEL ** -0.5),
        'q_gain': 1.0 + nrm((DEPTH, HEAD_DIM), 0.05),
        'k_gain': 1.0 + nrm((DEPTH, HEAD_DIM), 0.05),
        'sinks': nrm((DEPTH, N_HEADS), 0.5),
        'conv_w': nrm((DEPTH, CONV_WIDTH, LRU_W), CONV_WIDTH ** -0.5),
        'conv_b': nrm((DEPTH, LRU_W), 0.02),
        'w_rg': nrm((DEPTH, N_LRU_BLOCKS, LRU_BLOCK, LRU_BLOCK), LRU_BLOCK ** -0.5),
        'b_rg': nrm((DEPTH, LRU_W), 0.02),
        'w_ig': nrm((DEPTH, N_LRU_BLOCKS, LRU_BLOCK, LRU_BLOCK), LRU_BLOCK ** -0.5),
        'b_ig': nrm((DEPTH, LRU_W), 0.02),
        'lru_lambda': lru_lambda,
        'beta_attn': 1.0 + nrm((DEPTH, ATTN_W), 0.05),
        'beta_lru': 1.0 + nrm((DEPTH, LRU_W), 0.05),
        'w_out': nrm((DEPTH, MIX_W, D_MODEL), MIX_W ** -0.5),
        'w2_gate': nrm((DEPTH, D_MODEL, D_FF), D_MODEL ** -0.5),
        'w2_up': nrm((DEPTH, D_MODEL, D_FF), D_MODEL ** -0.5),
        'w2_down': nrm((DEPTH, D_FF, D_MODEL), D_FF ** -0.5),
    }


def reference(x_prompt, x_sample, cache_k, cache_v, state_conv, state_lru, c_prompt, c_sample,
              w_ada, b_ada, w1_gate, w1_up, w1_down, w_in, q_gain, k_gain, sinks,
              conv_w, conv_b, w_rg, b_rg, w_ig, b_ig, lru_lambda, beta_attn, beta_lru,
              w_out, w2_gate, w2_up, w2_down):
    yp = x_prompt
    ys = x_sample
    kp_l, vp_l, cp_l, hp_l = [], [], [], []
    ks_l, vs_l, cs_l, hs_l = [], [], [], []
    for l in range(DEPTH):
        lw = (w_ada[l], b_ada[l], w1_gate[l], w1_up[l], w1_down[l], w_in[l], q_gain[l], k_gain[l],
              sinks[l], conv_w[l], conv_b[l], w_rg[l], b_rg[l], w_ig[l], b_ig[l], lru_lambda[l],
              beta_attn[l], beta_lru[l], w_out[l], w2_gate[l], w2_up[l], w2_down[l])
        yp, kp, vp, cp, hp = layer(yp, c_prompt, None, None, None, None, *lw)
        ys, kS, vS, cS, hS = layer(ys, c_sample, state_conv[l], state_lru[l], cache_k[l], cache_v[l], *lw)
        kp_l.append(kp); vp_l.append(vp); cp_l.append(cp); hp_l.append(hp)
        ks_l.append(kS); vs_l.append(vS); cs_l.append(cS); hs_l.append(hS)
    return (yp, ys,
            jnp.stack(kp_l), jnp.stack(vp_l), jnp.stack(cp_l), jnp.stack(hp_l),
            jnp.stack(ks_l), jnp.stack(vs_l), jnp.stack(cs_l), jnp.stack(hs_l))
```

```python
import functools

import numpy as np
import jax
import jax.numpy as jnp
from jax import lax
from jax.experimental import pallas as pl
from jax.experimental.pallas import tpu as pltpu

D_MODEL = 1024
DEPTH = 2
HEAD_DIM = 64
N_HEADS = 8
N_KV_HEADS = 2
GROUP = N_HEADS // N_KV_HEADS
ATTN_W = N_HEADS * HEAD_DIM
KV_W = N_KV_HEADS * HEAD_DIM
LRU_W = 512
N_LRU_BLOCKS = 8
LRU_BLOCK = LRU_W // N_LRU_BLOCKS
CONV_WIDTH = 4
RG_C = 8.0
WINDOW = 128
Q_BLOCK = 128
D_FF = 2816
N_MOD = 9
FFN_RES = 0.5
IN_COLS = ATTN_W + 2 * KV_W + 2 * LRU_W
QK_W = ATTN_W + KV_W
RMS_EPS = 1e-6
NEG_INF = -1e30

LANES = 128
SUBLANES = 8
HALF = LANES // 2
VMEM_LIMIT = 56 * 1024 * 1024

FFN_ROWS = 512
SAMPLE_SEQS = 16

BF16 = jnp.bfloat16
F32 = jnp.float32


def _dot(a, b):
    return jnp.dot(a, b, preferred_element_type=F32)


def _dot_nt(a, b):
    return lax.dot_general(a, b, (((1,), (1,)), ((), ())), preferred_element_type=F32)


def _rms(x):
    return x * lax.rsqrt(jnp.mean(x * x, axis=-1, keepdims=True) + RMS_EPS)


def _resident(shape):
    nd = len(shape)
    return pl.BlockSpec(shape, lambda *_: (0,) * nd, pipeline_mode=pl.Buffered(1))


def _mod_kernel(c_ref, w_ref, b_ref, o_ref):
    c = c_ref[...]
    h = (c * jax.nn.sigmoid(c)).astype(BF16)
    o_ref[0] = _dot(h, w_ref[0].astype(BF16)) + b_ref[0]


def _mod_call(c_all, w_ada, b_ada):
    n = c_all.shape[0]
    return pl.pallas_call(
        _mod_kernel,
        out_shape=jax.ShapeDtypeStruct((DEPTH, n, N_MOD * D_MODEL), F32),
        grid=(DEPTH, N_MOD),
        in_specs=[
            pl.BlockSpec((n, D_MODEL), lambda l, j: (0, 0)),
            pl.BlockSpec((1, D_MODEL, D_MODEL), lambda l, j: (l, 0, j)),
            pl.BlockSpec((1, 1, D_MODEL), lambda l, j: (l, 0, j)),
        ],
        out_specs=pl.BlockSpec((1, n, D_MODEL), lambda l, j: (l, 0, j)),
        compiler_params=pltpu.CompilerParams(
            dimension_semantics=("arbitrary", "arbitrary"), vmem_limit_bytes=VMEM_LIMIT),
        name="adaln_mod",
    )(c_all, w_ada, b_ada.reshape(DEPTH, 1, N_MOD * D_MODEL))


def _ffn_kernel(x_ref, sh_ref, sc_ref, g_ref, wg_ref, wu_ref, wd_ref, o_ref):
    x = x_ref[...]
    s, t, d = x.shape
    h = _rms(x) * (1.0 + sc_ref[...]) + sh_ref[...]
    h2 = h.reshape(s * t, d).astype(BF16)
    g = _dot(h2, wg_ref[...])
    u = _dot(h2, wu_ref[...])
    a = (g * jax.nn.sigmoid(g) * u).astype(BF16)
    y = _dot(a, wd_ref[...]).reshape(s, t, d)
    o_ref[...] = x + (FFN_RES * g_ref[...]) * y


def _mod_specs(seqs, first_chunk, index):
    return [pl.BlockSpec((seqs, 1, D_MODEL), functools.partial(index, chunk=first_chunk + k))
            for k in range(3)]


def _ffn_call(x, mod, first_chunk, wg, wu, wd, seqs, rows, name):
    nseq, t, d = x.shape
    grid = (nseq // seqs, t // rows)
    xspec = pl.BlockSpec((seqs, rows, d), lambda i, j: (i, j, 0))
    mspecs = _mod_specs(seqs, first_chunk, lambda i, j, chunk: (i, 0, chunk))
    return pl.pallas_call(
        _ffn_kernel,
        out_shape=jax.ShapeDtypeStruct(x.shape, F32),
        grid=grid,
        in_specs=[xspec, *mspecs, _resident(wg.shape), _resident(wu.shape), _resident(wd.shape)],
        out_specs=xspec,
        compiler_params=pltpu.CompilerParams(
            dimension_semantics=("arbitrary", "arbitrary"), vmem_limit_bytes=VMEM_LIMIT),
        name=name,
    )(x, mod, mod, mod, wg, wu, wd)


def _project(x2, sh, sc, w_in_ref, ones_ref, gain_ref):
    h = (_rms(x2) * (1.0 + sc) + sh).astype(BF16)
    z = _dot(h, w_in_ref[...])
    qk = z[:, :QK_W]
    sq = qk * qk
    hi = sq.astype(BF16)
    lo = (sq - hi.astype(F32)).astype(BF16)
    ss = _dot(hi, ones_ref[...]) + _dot(lo, ones_ref[...])
    qkn = qk * lax.rsqrt(ss * (1.0 / HEAD_DIM) + RMS_EPS) * gain_ref[...]
    q = qkn[:, :ATTN_W]
    k = qkn[:, ATTN_W:QK_W]
    v = z[:, QK_W:QK_W + KV_W]
    xr = z[:, QK_W + KV_W:QK_W + KV_W + LRU_W]
    gr = z[:, QK_W + KV_W + LRU_W:]
    return q, k, v, xr, gr


def _shift_rows(x, k, t, fill):
    return jnp.where(t >= k, pltpu.roll(x, k, 0), fill)


def _conv(xr, prev, t, w_ref, b_ref):
    y = b_ref[...] + _shift_rows(xr, 3, t, prev[3]) * w_ref[0:1, :]
    y = y + _shift_rows(xr, 2, t, prev[2]) * w_ref[1:2, :]
    y = y + _shift_rows(xr, 1, t, prev[1]) * w_ref[2:3, :]
    return y + xr * w_ref[3:4, :]


def _rg_lru(xc, gr, h0, t, period, wgate_ref, bgate_ref, lam_ref):
    gates = _dot(xc.astype(BF16), wgate_ref[...]) + bgate_ref[...]
    r = jax.nn.sigmoid(gates[:, :LRU_W])
    gi = jax.nn.sigmoid(gates[:, LRU_W:])
    lam = lam_ref[...]
    softplus = jnp.maximum(-lam, 0.0) + jnp.log1p(jnp.exp(-jnp.abs(lam)))
    log_a = (-RG_C * r) * softplus
    a = jnp.exp(log_a)
    u = jnp.sqrt(-jnp.tanh(log_a) * (a * a + 1.0)) * (gi * xc)
    s = 1
    while s < period:
        a_prev = _shift_rows(a, s, t, 1.0)
        u_prev = _shift_rows(u, s, t, 0.0)
        u = u + a * u_prev
        a = a * a_prev
        s *= 2
    hs = a * h0 + u
    gelu = 0.5 * gr * (1.0 + jnp.tanh(np.sqrt(2.0 / np.pi) * (gr + 0.044715 * (gr * gr * gr))))
    return hs, hs * gelu


def _merge_out(x2, g2, attn, lru, beta_ref, wout_ref):
    merged = jnp.concatenate([_rms(attn), _rms(lru)], axis=-1) * beta_ref[...]
    return x2 + g2 * _dot(merged.astype(BF16), wout_ref[...])


def _softmax_with_sink(s, sink):
    m = jnp.maximum(jnp.max(s, axis=-1, keepdims=True), sink)
    p = jnp.exp(s - m)
    den = jnp.sum(p, axis=-1, keepdims=True) + jnp.exp(sink - m)
    return p, den


def _half_variants(x, low):
    xs = pltpu.roll(x, HALF, x.ndim - 1)
    zero = jnp.zeros_like(x)
    return (
        (jnp.where(low, x, zero).astype(BF16), jnp.where(low, zero, xs).astype(BF16)),
        (jnp.where(low, xs, zero).astype(BF16), jnp.where(low, zero, x).astype(BF16)),
    )


def _mix_prompt_kernel(sinks_ref, x_ref, sh_ref, sc_ref, g_ref, w_in_ref, ones_ref, gain_ref, bias_ref,
                       convw_ref, convb_ref, wgate_ref, bgate_ref, lam_ref, beta_ref, wout_ref,
                       y_ref, k_ref, v_ref, conv_ref, h_ref,
                       kprev, vprev, tail, hcar):
    n = pl.program_id(1)

    @pl.when(n == 0)
    def _():
        kprev[...] = jnp.zeros_like(kprev)
        vprev[...] = jnp.zeros_like(vprev)
        tail[...] = jnp.zeros_like(tail)
        hcar[...] = jnp.zeros_like(hcar)

    x2 = x_ref[0]
    q, k, v, xr, gr = _project(x2, sh_ref[0], sc_ref[0], w_in_ref, ones_ref, gain_ref)
    k_ref[0] = k
    v_ref[0] = v

    low = lax.broadcasted_iota(jnp.int32, (2 * Q_BLOCK, LANES), 1) < HALF
    kvar = _half_variants(jnp.concatenate([kprev[...], k], axis=0), low)
    vvar = _half_variants(jnp.concatenate([vprev[...], v], axis=0), low)
    kprev[...] = k
    vprev[...] = v
    tiles = []
    for tile in range(ATTN_W // LANES):
        qt = q[:, tile * LANES:(tile + 1) * LANES].astype(BF16)
        acc = None
        for parity in range(2):
            head = 2 * tile + parity
            kv = head // GROUP
            s = _dot_nt(qt, kvar[kv][parity]) + bias_ref[0, head]
            p, den = _softmax_with_sink(s, sinks_ref[head])
            o = _dot(p.astype(BF16), vvar[kv][parity]) / den
            acc = o if acc is None else acc + o
        tiles.append(acc)
    attn = jnp.concatenate(tiles, axis=-1)

    t = lax.broadcasted_iota(jnp.int32, (Q_BLOCK, 1), 0)
    tail_rows = jnp.concatenate([jnp.zeros((Q_BLOCK - SUBLANES, LRU_W), F32), tail[...]], axis=0)
    prev = {kk: pltpu.roll(tail_rows, kk, 0) for kk in (1, 2, 3)}
    xc = _conv(xr, prev, t, convw_ref, convb_ref)
    tail[...] = xr[Q_BLOCK - SUBLANES:, :]
    conv_ref[0] = xr[Q_BLOCK - SUBLANES:, :]
    hs, lru = _rg_lru(xc, gr, hcar[...], t, Q_BLOCK, wgate_ref, bgate_ref, lam_ref)
    hcar[...] = hs[Q_BLOCK - 1:, :]
    h_ref[0] = hs[Q_BLOCK - SUBLANES:, :]

    y_ref[0] = _merge_out(x2, g_ref[0], attn, lru, beta_ref, wout_ref)


def _prompt_bias():
    i = np.arange(Q_BLOCK)[:, None]
    j = np.arange(2 * Q_BLOCK)[None, :]
    dist = Q_BLOCK + i - j
    band = (dist >= 0) & (dist <= WINDOW)
    slopes = np.asarray([2.0 ** (-8.0 * (h + 1) / N_HEADS) for h in range(N_HEADS)], np.float32)
    alibi = -(slopes[:, None, None] * dist[None].astype(np.float32))
    general = np.where(band[None], alibi, np.float32(NEG_INF))
    first = np.where((band & (j >= Q_BLOCK))[None], alibi, np.float32(NEG_INF))
    return np.stack([first, general]).astype(np.float32)


def _mix_prompt_call(x, mod, lw):
    b, t, d = x.shape
    nb = t // Q_BLOCK
    xspec = pl.BlockSpec((1, Q_BLOCK, d), lambda i, j: (i, j, 0))
    mspecs = _mod_specs(1, 3, lambda i, j, chunk: (i, 0, chunk))
    bias = jnp.asarray(_prompt_bias())
    last = lambda shape: pl.BlockSpec(shape, lambda i, j: (i, 0, 0))
    small = [lw["conv_w"], lw["conv_b"], lw["w_gate"], lw["b_gate"], lw["lam"], lw["beta"], lw["w_out"]]
    outs = pl.pallas_call(
        _mix_prompt_kernel,
        out_shape=(
            jax.ShapeDtypeStruct(x.shape, F32),
            jax.ShapeDtypeStruct((b, WINDOW, KV_W), F32),
            jax.ShapeDtypeStruct((b, WINDOW, KV_W), F32),
            jax.ShapeDtypeStruct((b, SUBLANES, LRU_W), F32),
            jax.ShapeDtypeStruct((b, SUBLANES, LRU_W), F32),
        ),
        grid=(b, nb),
        in_specs=[
            pl.BlockSpec(memory_space=pltpu.SMEM),
            xspec, *mspecs,
            _resident(lw["w_in"].shape), _resident(lw["ones"].shape), _resident(lw["gain"].shape),
            pl.BlockSpec((1, N_HEADS, Q_BLOCK, 2 * Q_BLOCK), lambda i, j: (jnp.minimum(j, 1), 0, 0, 0)),
            *[_resident(a.shape) for a in small],
        ],
        out_specs=(xspec, last((1, WINDOW, KV_W)), last((1, WINDOW, KV_W)),
                   last((1, SUBLANES, LRU_W)), last((1, SUBLANES, LRU_W))),
        scratch_shapes=[
            pltpu.VMEM((Q_BLOCK, KV_W), F32), pltpu.VMEM((Q_BLOCK, KV_W), F32),
            pltpu.VMEM((SUBLANES, LRU_W), F32), pltpu.VMEM((1, LRU_W), F32),
        ],
        compiler_params=pltpu.CompilerParams(
            dimension_semantics=("arbitrary", "arbitrary"), vmem_limit_bytes=VMEM_LIMIT),
        name="mix_prompt",
    )(lw["sinks"], x, mod, mod, mod, lw["w_in"], lw["ones"], lw["gain"], bias, *small)
    return outs


def _mix_sample_kernel(x_ref, sh_ref, sc_ref, g_ref, ck_ref, cv_ref, cs_ref, h0_ref,
                       w_in_ref, ones_ref, gain_ref, bias_ref, sink_ref,
                       convw_ref, convb_ref, wgate_ref, bgate_ref, lam_ref, beta_ref, wout_ref,
                       y_ref, k_ref, v_ref, conv_ref, h_ref):
    sb, tq, d = x_ref.shape
    rows = sb * tq

    def flat(a):
        return jnp.broadcast_to(a, (sb, tq, a.shape[-1])).reshape(rows, a.shape[-1])

    x2 = x_ref[...].reshape(rows, d)
    q, k, v, xr, gr = _project(x2, flat(sh_ref[...]), flat(sc_ref[...]), w_in_ref, ones_ref, gain_ref)
    k3 = k.reshape(sb, tq, KV_W)
    v3 = v.reshape(sb, tq, KV_W)
    ck = ck_ref[...]
    cv = cv_ref[...]
    k_ref[...] = jnp.concatenate([ck[:, tq:, :], k3], axis=1)
    v_ref[...] = jnp.concatenate([cv[:, tq:, :], v3], axis=1)

    pad = jnp.zeros((sb, WINDOW - tq, KV_W), F32)
    kall = jnp.concatenate([ck, k3, pad], axis=1)
    vall = jnp.concatenate([cv, v3, pad], axis=1)
    low2 = lax.broadcasted_iota(jnp.int32, (rows, LANES), 1) < HALF
    zero2 = jnp.zeros((rows, LANES), F32)
    pieces = []
    for head in range(N_HEADS):
        tile, parity, kv = head // 2, head % 2, head // GROUP
        qt = q[:, tile * LANES:(tile + 1) * LANES]
        src = qt if parity == kv else pltpu.roll(qt, HALF, 1)
        piece = jnp.where(low2, src, zero2) if kv == 0 else jnp.where(low2, zero2, src)
        pieces.append(piece.reshape(sb, tq, LANES))
    qrows = jnp.concatenate(pieces, axis=1).astype(BF16)
    s = jnp.einsum("snc,sjc->snj", qrows, kall.astype(BF16), preferred_element_type=F32)
    s = s + bias_ref[...]
    p, den = _softmax_with_sink(s, sink_ref[...])
    o = jnp.einsum("snj,sjc->snc", p.astype(BF16), vall.astype(BF16), preferred_element_type=F32) / den
    tiles = []
    for tile in range(ATTN_W // LANES):
        kv = (2 * tile) // GROUP
        oe = o[:, (2 * tile) * tq:(2 * tile + 1) * tq, :].reshape(rows, LANES)
        oo = o[:, (2 * tile + 1) * tq:(2 * tile + 2) * tq, :].reshape(rows, LANES)
        if kv == 0:
            tiles.append(jnp.where(low2, oe, pltpu.roll(oo, HALF, 1)))
        else:
            tiles.append(jnp.where(low2, pltpu.roll(oe, HALF, 1), oo))
    attn = jnp.concatenate(tiles, axis=-1)

    t = lax.broadcasted_iota(jnp.int32, (sb, tq, 1), 1).reshape(rows, 1)
    state = cs_ref[...].reshape(rows, LRU_W)
    prev = {3: state, 2: pltpu.roll(state, rows - 1, 0), 1: pltpu.roll(state, rows - 2, 0)}
    xc = _conv(xr, prev, t, convw_ref, convb_ref)
    conv_ref[...] = xr.reshape(sb, tq, LRU_W)
    hs, lru = _rg_lru(xc, gr, flat(h0_ref[...]), t, tq, wgate_ref, bgate_ref, lam_ref)
    h_ref[...] = hs.reshape(sb, tq, LRU_W)

    y_ref[...] = _merge_out(x2, flat(g_ref[...]), attn, lru, beta_ref, wout_ref).reshape(sb, tq, d)


def _sample_bias(tq):
    i = np.arange(tq)[:, None]
    j = np.arange(2 * WINDOW)[None, :]
    dist = WINDOW + i - j
    ok = (dist >= 0) & (dist <= WINDOW) & (j < WINDOW + tq)
    slopes = np.asarray([2.0 ** (-8.0 * (h + 1) / N_HEADS) for h in range(N_HEADS)], np.float32)
    alibi = -(slopes[:, None, None] * dist[None].astype(np.float32))
    return np.where(ok[None], alibi, np.float32(NEG_INF)).reshape(N_HEADS * tq, 2 * WINDOW).astype(np.float32)


def _mix_sample_call(x, mod, cache_k, cache_v, conv_state, lru_state, lw):
    nseq, tq, d = x.shape
    sb = SAMPLE_SEQS
    seq_block = lambda shape: pl.BlockSpec(shape, lambda i: (i, 0, 0))
    xspec = seq_block((sb, tq, d))
    mspecs = _mod_specs(sb, 3, lambda i, chunk: (i, 0, chunk))
    bias = jnp.asarray(_sample_bias(tq))
    sink_col = jnp.repeat(lw["sinks"], tq).reshape(N_HEADS * tq, 1)
    conv_pad = jnp.pad(conv_state, ((0, 0), (0, tq - (CONV_WIDTH - 1)), (0, 0)))
    small = [lw["conv_w"], lw["conv_b"], lw["w_gate"], lw["b_gate"], lw["lam"], lw["beta"], lw["w_out"]]
    consts = [lw["w_in"], lw["ones"], lw["gain"], bias, sink_col, *small]
    return pl.pallas_call(
        _mix_sample_kernel,
        out_shape=(
            jax.ShapeDtypeStruct(x.shape, F32),
            jax.ShapeDtypeStruct((nseq, WINDOW, KV_W), F32),
            jax.ShapeDtypeStruct((nseq, WINDOW, KV_W), F32),
            jax.ShapeDtypeStruct((nseq, tq, LRU_W), F32),
            jax.ShapeDtypeStruct((nseq, tq, LRU_W), F32),
        ),
        grid=(nseq // sb,),
        in_specs=[
            xspec, *mspecs,
            seq_block((sb, WINDOW, KV_W)), seq_block((sb, WINDOW, KV_W)),
            seq_block((sb, tq, LRU_W)), seq_block((sb, 1, LRU_W)),
            *[_resident(a.shape) for a in consts],
        ],
        out_specs=(xspec, seq_block((sb, WINDOW, KV_W)), seq_block((sb, WINDOW, KV_W)),
                   seq_block((sb, tq, LRU_W)), seq_block((sb, tq, LRU_W))),
        compiler_params=pltpu.CompilerParams(
            dimension_semantics=("arbitrary",), vmem_limit_bytes=VMEM_LIMIT),
        name="mix_sample",
    )(x, mod, mod, mod, cache_k, cache_v, conv_pad, lru_state.reshape(nseq, 1, LRU_W), *consts)


def _block_diag(w):
    n, c, _ = w.shape
    eye = jnp.eye(n, dtype=w.dtype)
    return (w[:, :, None, :] * eye[:, None, :, None]).reshape(n * c, n * c)


def _layer_weights(l, w_in, q_gain, k_gain, sinks, conv_w, conv_b, w_rg, b_rg, w_ig, b_ig, lru_lambda,
                   beta_attn, beta_lru, w_out):
    head_of = np.arange(QK_W) // HEAD_DIM
    ones = jnp.asarray(head_of[:, None] == head_of[None, :], BF16)
    gain = jnp.concatenate([jnp.tile(q_gain[l] * (HEAD_DIM ** -0.5), N_HEADS), jnp.tile(k_gain[l], N_KV_HEADS)])
    return dict(
        w_in=w_in[l].astype(BF16),
        ones=ones,
        gain=gain.reshape(1, QK_W),
        sinks=sinks[l],
        conv_w=conv_w[l],
        conv_b=conv_b[l].reshape(1, LRU_W),
        w_gate=jnp.concatenate([_block_diag(w_rg[l]), _block_diag(w_ig[l])], axis=1).astype(BF16),
        b_gate=jnp.concatenate([b_rg[l], b_ig[l]]).reshape(1, 2 * LRU_W),
        lam=lru_lambda[l].reshape(1, LRU_W),
        beta=jnp.concatenate([beta_attn[l], beta_lru[l]]).reshape(1, ATTN_W + LRU_W),
        w_out=w_out[l].astype(BF16),
    )


def kernel(x_prompt, x_sample, cache_k, cache_v, state_conv, state_lru, c_prompt, c_sample, w_ada, b_ada, w1_gate, w1_up, w1_down, w_in, q_gain, k_gain, sinks, conv_w, conv_b, w_rg, b_rg, w_ig, b_ig, lru_lambda, beta_attn, beta_lru, w_out, w2_gate, w2_up, w2_down):
    nb = x_prompt.shape[0]
    ns, tq, _ = x_sample.shape
    mod = _mod_call(jnp.concatenate([c_prompt, c_sample], axis=0), w_ada, b_ada)
    mod_p = mod[:, :nb].reshape(DEPTH, nb, 1, N_MOD * D_MODEL)
    mod_s = mod[:, nb:].reshape(DEPTH, ns, 1, N_MOD * D_MODEL)

    yp, ys = x_prompt, x_sample
    outs_p, outs_s = [], []
    for l in range(DEPTH):
        lw = _layer_weights(l, w_in, q_gain, k_gain, sinks, conv_w, conv_b, w_rg, b_rg, w_ig, b_ig,
                            lru_lambda, beta_attn, beta_lru, w_out)
        w1 = (w1_gate[l].astype(BF16), w1_up[l].astype(BF16), w1_down[l].astype(BF16))
        w2 = (w2_gate[l].astype(BF16), w2_up[l].astype(BF16), w2_down[l].astype(BF16))

        yp = _ffn_call(yp, mod_p[l], 0, *w1, 1, FFN_ROWS, "ffn1_prompt")
        ys = _ffn_call(ys, mod_s[l], 0, *w1, FFN_ROWS // tq, tq, "ffn1_sample")

        yp, kp, vp, cp, hp = _mix_prompt_call(yp, mod_p[l], lw)
        ys, ks, vs, cs, hs = _mix_sample_call(
            ys, mod_s[l], cache_k[l].reshape(ns, WINDOW, KV_W), cache_v[l].reshape(ns, WINDOW, KV_W),
            state_conv[l], state_lru[l], lw)
        outs_p.append((kp.reshape(nb, WINDOW, N_KV_HEADS, HEAD_DIM), vp.reshape(nb, WINDOW, N_KV_HEADS, HEAD_DIM),
                       cp[:, SUBLANES - (CONV_WIDTH - 1):], hp[:, SUBLANES - 1]))
        outs_s.append((ks.reshape(ns, WINDOW, N_KV_HEADS, HEAD_DIM), vs.reshape(ns, WINDOW, N_KV_HEADS, HEAD_DIM),
                       cs[:, tq - (CONV_WIDTH - 1):], hs[:, tq - 1]))

        yp = _ffn_call(yp, mod_p[l], 6, *w2, 1, FFN_ROWS, "ffn2_prompt")
        ys = _ffn_call(ys, mod_s[l], 6, *w2, FFN_ROWS // tq, tq, "ffn2_sample")

    stack = lambda outs, k: jnp.stack([o[k] for o in outs])
    return (yp, ys,
            stack(outs_p, 0), stack(outs_p, 1), stack(outs_p, 2), stack(outs_p, 3),
            stack(outs_s, 0), stack(outs_s, 1), stack(outs_s, 2), stack(outs_s, 3))
```

```python
import functools

import numpy as np
import jax
import jax.numpy as jnp
from jax import lax
from jax.experimental import pallas as pl
from jax.experimental.pallas import tpu as pltpu

D_MODEL = 1024
DEPTH = 2
HEAD_DIM = 64
N_HEADS = 8
N_KV_HEADS = 2
GROUP = N_HEADS // N_KV_HEADS
ATTN_W = N_HEADS * HEAD_DIM
KV_W = N_KV_HEADS * HEAD_DIM
LRU_W = 512
N_LRU_BLOCKS = 8
LRU_BLOCK = LRU_W // N_LRU_BLOCKS
CONV_WIDTH = 4
RG_C = 8.0
WINDOW = 128
Q_BLOCK = 128
D_FF = 2816
N_MOD = 9
FFN_RES = 0.5
IN_COLS = ATTN_W + 2 * KV_W + 2 * LRU_W
QK_W = ATTN_W + KV_W
RMS_EPS = 1e-6
NEG_INF = -1e30

LANES = 128
SUBLANES = 8
HALF = LANES // 2
MXU_DIM = 256
SCAN_GROUP = 4
VMEM_LIMIT = 56 * 1024 * 1024

FFN_ROWS = 512
MIX_ROWS = 256
SAMPLE_SEQS = 16

BF16 = jnp.bfloat16
F32 = jnp.float32


def _dot(a, b):
    return jnp.dot(a, b, preferred_element_type=F32)


def _dot_nt(a, b):
    return lax.dot_general(a, b, (((1,), (1,)), ((), ())), preferred_element_type=F32)


def _rms(x):
    return x * lax.rsqrt(jnp.mean(x * x, axis=-1, keepdims=True) + RMS_EPS)


def _resident(shape):
    nd = len(shape)
    return pl.BlockSpec(shape, lambda *_: (0,) * nd, pipeline_mode=pl.Buffered(1))


def _mod_kernel(c_ref, w_ref, b_ref, o_ref):
    c = c_ref[...]
    h = (c * jax.nn.sigmoid(c)).astype(BF16)
    o_ref[0] = _dot(h, w_ref[0].astype(BF16)) + b_ref[0]


def _mod_call(c_all, w_ada, b_ada):
    n = c_all.shape[0]
    return pl.pallas_call(
        _mod_kernel,
        out_shape=jax.ShapeDtypeStruct((DEPTH, n, N_MOD * D_MODEL), F32),
        grid=(DEPTH, N_MOD),
        in_specs=[
            pl.BlockSpec((n, D_MODEL), lambda l, j: (0, 0)),
            pl.BlockSpec((1, D_MODEL, D_MODEL), lambda l, j: (l, 0, j)),
            pl.BlockSpec((1, 1, D_MODEL), lambda l, j: (l, 0, j)),
        ],
        out_specs=pl.BlockSpec((1, n, D_MODEL), lambda l, j: (l, 0, j)),
        compiler_params=pltpu.CompilerParams(
            dimension_semantics=("arbitrary", "arbitrary"), vmem_limit_bytes=VMEM_LIMIT),
        name="adaln_mod",
    )(c_all, w_ada, b_ada.reshape(DEPTH, 1, N_MOD * D_MODEL))


def _ffn_kernel(x_ref, sh_ref, sc_ref, g_ref, wg_ref, wu_ref, wd_ref, o_ref):
    x = x_ref[...]
    s, t, d = x.shape
    h = _rms(x) * (1.0 + sc_ref[...]) + sh_ref[...]
    h2 = h.reshape(s * t, d).astype(BF16)
    g = _dot(h2, wg_ref[...])
    u = _dot(h2, wu_ref[...])
    a = (g * jax.nn.sigmoid(g) * u).astype(BF16)
    y = _dot(a, wd_ref[...]).reshape(s, t, d)
    o_ref[...] = x + (FFN_RES * g_ref[...]) * y


def _mod_specs(seqs, first_chunk, index):
    return [pl.BlockSpec((seqs, 1, D_MODEL), functools.partial(index, chunk=first_chunk + k))
            for k in range(3)]


def _ffn_call(x, mod, first_chunk, wg, wu, wd, seqs, rows, name):
    nseq, t, d = x.shape
    grid = (nseq // seqs, t // rows)
    xspec = pl.BlockSpec((seqs, rows, d), lambda i, j: (i, j, 0))
    mspecs = _mod_specs(seqs, first_chunk, lambda i, j, chunk: (i, 0, chunk))
    return pl.pallas_call(
        _ffn_kernel,
        out_shape=jax.ShapeDtypeStruct(x.shape, F32),
        grid=grid,
        in_specs=[xspec, *mspecs, _resident(wg.shape), _resident(wu.shape), _resident(wd.shape)],
        out_specs=xspec,
        compiler_params=pltpu.CompilerParams(
            dimension_semantics=("arbitrary", "arbitrary"), vmem_limit_bytes=VMEM_LIMIT),
        name=name,
    )(x, mod, mod, mod, wg, wu, wd)


def _head_mean_square(qk, avg_ref):
    parts = []
    for c0 in range(0, qk.shape[1], MXU_DIM):
        w = min(MXU_DIM, qk.shape[1] - c0)
        sq = qk[:, c0:c0 + w] * qk[:, c0:c0 + w]
        hi = sq.astype(BF16)
        lo = (sq - hi.astype(F32)).astype(BF16)
        avg = avg_ref[:w, :w]
        parts.append(_dot(hi, avg) + _dot(lo, avg))
    return jnp.concatenate(parts, axis=-1)


def _project(x2, sh, sc, w_in_ref, ones_ref, gain_ref):
    h = (_rms(x2) * (1.0 + sc) + sh).astype(BF16)
    z = _dot(h, w_in_ref[...])
    qk = z[:, :QK_W]
    qkn = qk * lax.rsqrt(_head_mean_square(qk, ones_ref) + RMS_EPS) * gain_ref[...]
    q = qkn[:, :ATTN_W]
    k = qkn[:, ATTN_W:QK_W]
    v = z[:, QK_W:QK_W + KV_W]
    xr = z[:, QK_W + KV_W:QK_W + KV_W + LRU_W]
    gr = z[:, QK_W + KV_W + LRU_W:]
    return q, k, v, xr, gr


def _shift_rows(x, k, t, fill):
    return jnp.where(t >= k, pltpu.roll(x, k, 0), fill)


def _conv(delayed, w_ref, b_ref):
    y = b_ref[...] + delayed[3] * w_ref[0:1, :]
    y = y + delayed[2] * w_ref[1:2, :]
    y = y + delayed[1] * w_ref[2:3, :]
    return y + delayed[0] * w_ref[3:4, :]


def _lru_inputs(xc, wgate_ref, bgate_ref, lam_ref):
    gates = _dot(xc.astype(BF16), wgate_ref[...]) + bgate_ref[...]
    r = jax.nn.sigmoid(gates[:, :LRU_W])
    gi = jax.nn.sigmoid(gates[:, LRU_W:])
    lam = lam_ref[...]
    softplus = jnp.maximum(-lam, 0.0) + jnp.log1p(jnp.exp(-jnp.abs(lam)))
    log_a = (-RG_C * r) * softplus
    a = jnp.exp(log_a)
    u = jnp.sqrt(-jnp.tanh(log_a) * (a * a + 1.0)) * (gi * xc)
    return a, u


def _log_scan(a, u, t, period):
    s = 1
    while s < period:
        a_prev = _shift_rows(a, s, t, 1.0)
        u_prev = _shift_rows(u, s, t, 0.0)
        u = u + a * u_prev
        a = a * a_prev
        s *= 2
    return a, u


def _slab_load(ref, start, n, stride=1):
    rows = pl.ds(start, n) if stride == 1 else pl.ds(start, n, stride=stride)
    return jnp.concatenate([ref[s, rows, :] for s in range(ref.shape[0])], axis=-1)


def _slab_store(ref, start, n, val, stride=1):
    rows = pl.ds(start, n) if stride == 1 else pl.ds(start, n, stride=stride)
    for s in range(ref.shape[0]):
        ref[s, rows, :] = val[:, s * LANES:(s + 1) * LANES]


def _blocked_scan(a_scr, u_scr, base, n, h_init):
    if n <= 2 * SUBLANES:
        t = lax.broadcasted_iota(jnp.int32, (n, 1), 0)
        a, u = _log_scan(_slab_load(a_scr, base, n), _slab_load(u_scr, base, n), t, n)
        return a * h_init + u
    m = n // SCAN_GROUP
    a_loc, h_loc = [], []
    for r in range(SCAN_GROUP):
        a_r = _slab_load(a_scr, base + r, m, SCAN_GROUP)
        u_r = _slab_load(u_scr, base + r, m, SCAN_GROUP)
        h_loc.append(u_r if r == 0 else a_r * h_loc[-1] + u_r)
        a_loc.append(a_r if r == 0 else a_r * a_loc[-1])
    _slab_store(a_scr, base + n, m, a_loc[-1])
    _slab_store(u_scr, base + n, m, h_loc[-1])
    ends = _blocked_scan(a_scr, u_scr, base + n, m, h_init)
    g = lax.broadcasted_iota(jnp.int32, (m, 1), 0)
    carry = jnp.where(g >= 1, pltpu.roll(ends, 1, 0), h_init)
    for r in range(SCAN_GROUP):
        _slab_store(u_scr, base + r, m, a_loc[r] * carry + h_loc[r], SCAN_GROUP)
    return _slab_load(u_scr, base, n)


def _scan_rows(n):
    return n if n <= 2 * SUBLANES else n + _scan_rows(n // SCAN_GROUP)


def _gelu_tanh(x):
    return 0.5 * x * (1.0 + jnp.tanh(np.sqrt(2.0 / np.pi) * (x + 0.044715 * (x * x * x))))


def _merge_out(x2, g2, attn, lru, beta_ref, wout_ref):
    merged = jnp.concatenate([_rms(attn), _rms(lru)], axis=-1) * beta_ref[...]
    return x2 + g2 * _dot(merged.astype(BF16), wout_ref[...])


def _softmax_with_sink(s, sink):
    m = jnp.maximum(jnp.max(s, axis=-1, keepdims=True), sink)
    p = jnp.exp(s - m)
    den = jnp.sum(p, axis=-1, keepdims=True) + jnp.exp(sink - m)
    return p, den


def _half_variants(x, low):
    xs = pltpu.roll(x, HALF, x.ndim - 1)
    zero = jnp.zeros_like(x)
    return (
        (jnp.where(low, x, zero).astype(BF16), jnp.where(low, zero, xs).astype(BF16)),
        (jnp.where(low, xs, zero).astype(BF16), jnp.where(low, zero, x).astype(BF16)),
    )


def _mix_prompt_kernel(sinks_ref, x_ref, sh_ref, sc_ref, g_ref, w_in_ref, ones_ref, gain_ref, bias_ref,
                       convw_ref, convb_ref, wgate_ref, bgate_ref, lam_ref, beta_ref, wout_ref,
                       y_ref, k_ref, v_ref, conv_ref, h_ref,
                       kprev, vprev, hcar, xr_scr, a_scr, u_scr):
    n = pl.program_id(1)

    @pl.when(n == 0)
    def _():
        kprev[...] = jnp.zeros_like(kprev)
        vprev[...] = jnp.zeros_like(vprev)
        hcar[...] = jnp.zeros_like(hcar)
        xr_scr[:, 0:SUBLANES, :] = jnp.zeros((xr_scr.shape[0], SUBLANES, LANES), F32)

    low = lax.broadcasted_iota(jnp.int32, (2 * Q_BLOCK, LANES), 1) < HALF
    k_before, v_before, h_before = kprev[...], vprev[...], hcar[...]
    for sub in range(x_ref.shape[1] // Q_BLOCK):
        rows = pl.ds(sub * Q_BLOCK, Q_BLOCK)
        x2 = x_ref[0, rows, :]
        q, k, v, xr, gr = _project(x2, sh_ref[0], sc_ref[0], w_in_ref, ones_ref, gain_ref)

        bias_row = jnp.minimum(n, 1) if sub == 0 else 1
        kvar = _half_variants(jnp.concatenate([k_before, k], axis=0), low)
        vvar = _half_variants(jnp.concatenate([v_before, v], axis=0), low)
        k_before, v_before = k, v
        tiles = []
        for tile in range(ATTN_W // LANES):
            qt = q[:, tile * LANES:(tile + 1) * LANES].astype(BF16)
            acc = None
            for parity in range(2):
                head = 2 * tile + parity
                kv = head // GROUP
                s = _dot_nt(qt, kvar[kv][parity]) + bias_ref[bias_row, head]
                p, den = _softmax_with_sink(s, sinks_ref[head])
                o = _dot(p.astype(BF16), vvar[kv][parity]) / den
                acc = o if acc is None else acc + o
            tiles.append(acc)
        attn = jnp.concatenate(tiles, axis=-1)

        first = SUBLANES + sub * Q_BLOCK
        _slab_store(xr_scr, first, Q_BLOCK, xr)
        xc = _conv([_slab_load(xr_scr, first - kk, Q_BLOCK) for kk in range(CONV_WIDTH)], convw_ref, convb_ref)
        a, u = _lru_inputs(xc, wgate_ref, bgate_ref, lam_ref)
        _slab_store(a_scr.at[sub], 0, Q_BLOCK, a)
        _slab_store(u_scr.at[sub], 0, Q_BLOCK, u)
        hs = _blocked_scan(a_scr.at[sub], u_scr.at[sub], 0, Q_BLOCK, h_before)
        h_before = hs[Q_BLOCK - 1:, :]

        y_ref[0, rows, :] = _merge_out(x2, g_ref[0], attn, hs * _gelu_tanh(gr), beta_ref, wout_ref)

    k_ref[0] = k
    v_ref[0] = v
    kprev[...] = k
    vprev[...] = v
    conv_ref[0] = xr[Q_BLOCK - SUBLANES:, :]
    _slab_store(xr_scr, 0, SUBLANES, xr[Q_BLOCK - SUBLANES:, :])
    h_ref[0] = hs[Q_BLOCK - SUBLANES:, :]
    hcar[...] = h_before


def _prompt_bias():
    i = np.arange(Q_BLOCK)[:, None]
    j = np.arange(2 * Q_BLOCK)[None, :]
    dist = Q_BLOCK + i - j
    band = (dist >= 0) & (dist <= WINDOW)
    slopes = np.asarray([2.0 ** (-8.0 * (h + 1) / N_HEADS) for h in range(N_HEADS)], np.float32)
    alibi = -(slopes[:, None, None] * dist[None].astype(np.float32))
    general = np.where(band[None], alibi, np.float32(NEG_INF))
    first = np.where((band & (j >= Q_BLOCK))[None], alibi, np.float32(NEG_INF))
    return np.stack([first, general]).astype(np.float32)


def _mix_prompt_call(x, mod, lw):
    b, t, d = x.shape
    nb = t // MIX_ROWS
    subs = MIX_ROWS // Q_BLOCK
    xspec = pl.BlockSpec((1, MIX_ROWS, d), lambda i, j: (i, j, 0))
    mspecs = _mod_specs(1, 3, lambda i, j, chunk: (i, 0, chunk))
    bias = jnp.asarray(_prompt_bias())
    last = lambda shape: pl.BlockSpec(shape, lambda i, j: (i, 0, 0))
    small = [lw["conv_w"], lw["conv_b"], lw["w_gate"], lw["b_gate"], lw["lam"], lw["beta"], lw["w_out"]]
    outs = pl.pallas_call(
        _mix_prompt_kernel,
        out_shape=(
            jax.ShapeDtypeStruct(x.shape, F32),
            jax.ShapeDtypeStruct((b, WINDOW, KV_W), F32),
            jax.ShapeDtypeStruct((b, WINDOW, KV_W), F32),
            jax.ShapeDtypeStruct((b, SUBLANES, LRU_W), F32),
            jax.ShapeDtypeStruct((b, SUBLANES, LRU_W), F32),
        ),
        grid=(b, nb),
        in_specs=[
            pl.BlockSpec(memory_space=pltpu.SMEM),
            xspec, *mspecs,
            _resident(lw["w_in"].shape), _resident(lw["ones"].shape), _resident(lw["gain"].shape),
            _resident(bias.shape),
            *[_resident(a.shape) for a in small],
        ],
        out_specs=(xspec, last((1, WINDOW, KV_W)), last((1, WINDOW, KV_W)),
                   last((1, SUBLANES, LRU_W)), last((1, SUBLANES, LRU_W))),
        scratch_shapes=[
            pltpu.VMEM((Q_BLOCK, KV_W), F32), pltpu.VMEM((Q_BLOCK, KV_W), F32),
            pltpu.VMEM((1, LRU_W), F32),
            pltpu.VMEM((LRU_W // LANES, SUBLANES + MIX_ROWS, LANES), F32),
            pltpu.VMEM((subs, LRU_W // LANES, _scan_rows(Q_BLOCK), LANES), F32),
            pltpu.VMEM((subs, LRU_W // LANES, _scan_rows(Q_BLOCK), LANES), F32),
        ],
        compiler_params=pltpu.CompilerParams(
            dimension_semantics=("arbitrary", "arbitrary"), vmem_limit_bytes=VMEM_LIMIT),
        name="mix_prompt",
    )(lw["sinks"], x, mod, mod, mod, lw["w_in"], lw["ones"], lw["gain"], bias, *small)
    return outs


def _mix_sample_kernel(x_ref, sh_ref, sc_ref, g_ref, ck_ref, cv_ref, cs_ref, h0_ref,
                       w_in_ref, ones_ref, gain_ref, bias_ref, sink_ref,
                       convw_ref, convb_ref, wgate_ref, bgate_ref, lam_ref, beta_ref, wout_ref,
                       y_ref, k_ref, v_ref, conv_ref, h_ref):
    sb, tq, d = x_ref.shape
    rows = sb * tq

    def flat(a):
        return jnp.broadcast_to(a, (sb, tq, a.shape[-1])).reshape(rows, a.shape[-1])

    x2 = x_ref[...].reshape(rows, d)
    q, k, v, xr, gr = _project(x2, flat(sh_ref[...]), flat(sc_ref[...]), w_in_ref, ones_ref, gain_ref)
    k3 = k.reshape(sb, tq, KV_W)
    v3 = v.reshape(sb, tq, KV_W)
    ck = ck_ref[...]
    cv = cv_ref[...]
    k_ref[...] = jnp.concatenate([ck[:, tq:, :], k3], axis=1)
    v_ref[...] = jnp.concatenate([cv[:, tq:, :], v3], axis=1)

    pad = jnp.zeros((sb, WINDOW - tq, KV_W), F32)
    kall = jnp.concatenate([ck, k3, pad], axis=1)
    vall = jnp.concatenate([cv, v3, pad], axis=1)
    low2 = lax.broadcasted_iota(jnp.int32, (rows, LANES), 1) < HALF
    zero2 = jnp.zeros((rows, LANES), F32)
    pieces = []
    for head in range(N_HEADS):
        tile, parity, kv = head // 2, head % 2, head // GROUP
        qt = q[:, tile * LANES:(tile + 1) * LANES]
        src = qt if parity == kv else pltpu.roll(qt, HALF, 1)
        piece = jnp.where(low2, src, zero2) if kv == 0 else jnp.where(low2, zero2, src)
        pieces.append(piece.reshape(sb, tq, LANES))
    qrows = jnp.concatenate(pieces, axis=1).astype(BF16)
    s = jnp.einsum("snc,sjc->snj", qrows, kall.astype(BF16), preferred_element_type=F32)
    s = s + bias_ref[...]
    p, den = _softmax_with_sink(s, sink_ref[...])
    o = jnp.einsum("snj,sjc->snc", p.astype(BF16), vall.astype(BF16), preferred_element_type=F32) / den
    tiles = []
    for tile in range(ATTN_W // LANES):
        kv = (2 * tile) // GROUP
        oe = o[:, (2 * tile) * tq:(2 * tile + 1) * tq, :].reshape(rows, LANES)
        oo = o[:, (2 * tile + 1) * tq:(2 * tile + 2) * tq, :].reshape(rows, LANES)
        if kv == 0:
            tiles.append(jnp.where(low2, oe, pltpu.roll(oo, HALF, 1)))
        else:
            tiles.append(jnp.where(low2, pltpu.roll(oe, HALF, 1), oo))
    attn = jnp.concatenate(tiles, axis=-1)

    t = lax.broadcasted_iota(jnp.int32, (sb, tq, 1), 1).reshape(rows, 1)
    state = cs_ref[...].reshape(rows, LRU_W)
    prev = {3: state, 2: pltpu.roll(state, rows - 1, 0), 1: pltpu.roll(state, rows - 2, 0)}
    delayed = [xr] + [_shift_rows(xr, kk, t, prev[kk]) for kk in (1, 2, 3)]
    xc = _conv(delayed, convw_ref, convb_ref)
    conv_ref[...] = xr.reshape(sb, tq, LRU_W)
    a, u = _log_scan(*_lru_inputs(xc, wgate_ref, bgate_ref, lam_ref), t, tq)
    hs = a * flat(h0_ref[...]) + u
    h_ref[...] = hs.reshape(sb, tq, LRU_W)

    y_ref[...] = _merge_out(x2, flat(g_ref[...]), attn, hs * _gelu_tanh(gr), beta_ref, wout_ref).reshape(sb, tq, d)


def _sample_bias(tq):
    i = np.arange(tq)[:, None]
    j = np.arange(2 * WINDOW)[None, :]
    dist = WINDOW + i - j
    ok = (dist >= 0) & (dist <= WINDOW) & (j < WINDOW + tq)
    slopes = np.asarray([2.0 ** (-8.0 * (h + 1) / N_HEADS) for h in range(N_HEADS)], np.float32)
    alibi = -(slopes[:, None, None] * dist[None].astype(np.float32))
    return np.where(ok[None], alibi, np.float32(NEG_INF)).reshape(N_HEADS * tq, 2 * WINDOW).astype(np.float32)


def _mix_sample_call(x, mod, cache_k, cache_v, conv_state, lru_state, lw):
    nseq, tq, d = x.shape
    sb = SAMPLE_SEQS
    seq_block = lambda shape: pl.BlockSpec(shape, lambda i: (i, 0, 0))
    xspec = seq_block((sb, tq, d))
    mspecs = _mod_specs(sb, 3, lambda i, chunk: (i, 0, chunk))
    bias = jnp.asarray(_sample_bias(tq))
    sink_col = jnp.repeat(lw["sinks"], tq).reshape(N_HEADS * tq, 1)
    conv_pad = jnp.pad(conv_state, ((0, 0), (0, tq - (CONV_WIDTH - 1)), (0, 0)))
    small = [lw["conv_w"], lw["conv_b"], lw["w_gate"], lw["b_gate"], lw["lam"], lw["beta"], lw["w_out"]]
    consts = [lw["w_in"], lw["ones"], lw["gain"], bias, sink_col, *small]
    return pl.pallas_call(
        _mix_sample_kernel,
        out_shape=(
            jax.ShapeDtypeStruct(x.shape, F32),
            jax.ShapeDtypeStruct((nseq, WINDOW, KV_W), F32),
            jax.ShapeDtypeStruct((nseq, WINDOW, KV_W), F32),
            jax.ShapeDtypeStruct((nseq, tq, LRU_W), F32),
            jax.ShapeDtypeStruct((nseq, tq, LRU_W), F32),
        ),
        grid=(nseq // sb,),
        in_specs=[
            xspec, *mspecs,
            seq_block((sb, WINDOW, KV_W)), seq_block((sb, WINDOW, KV_W)),
            seq_block((sb, tq, LRU_W)), seq_block((sb, 1, LRU_W)),
            *[_resident(a.shape) for a in consts],
        ],
        out_specs=(xspec, seq_block((sb, WINDOW, KV_W)), seq_block((sb, WINDOW, KV_W)),
                   seq_block((sb, tq, LRU_W)), seq_block((sb, tq, LRU_W))),
        compiler_params=pltpu.CompilerParams(
            dimension_semantics=("arbitrary",), vmem_limit_bytes=VMEM_LIMIT),
        name="mix_sample",
    )(x, mod, mod, mod, cache_k, cache_v, conv_pad, lru_state.reshape(nseq, 1, LRU_W), *consts)


def _block_diag(w):
    n, c, _ = w.shape
    eye = jnp.eye(n, dtype=w.dtype)
    return (w[:, :, None, :] * eye[:, None, :, None]).reshape(n * c, n * c)


def _layer_weights(l, w_in, q_gain, k_gain, sinks, conv_w, conv_b, w_rg, b_rg, w_ig, b_ig, lru_lambda,
                   beta_attn, beta_lru, w_out):
    head_of = np.arange(MXU_DIM) // HEAD_DIM
    ones = jnp.asarray((head_of[:, None] == head_of[None, :]) * (1.0 / HEAD_DIM), BF16)
    gain = jnp.concatenate([jnp.tile(q_gain[l] * (HEAD_DIM ** -0.5), N_HEADS), jnp.tile(k_gain[l], N_KV_HEADS)])
    return dict(
        w_in=w_in[l].astype(BF16),
        ones=ones,
        gain=gain.reshape(1, QK_W),
        sinks=sinks[l],
        conv_w=conv_w[l],
        conv_b=conv_b[l].reshape(1, LRU_W),
        w_gate=jnp.concatenate([_block_diag(w_rg[l]), _block_diag(w_ig[l])], axis=1).astype(BF16),
        b_gate=jnp.concatenate([b_rg[l], b_ig[l]]).reshape(1, 2 * LRU_W),
        lam=lru_lambda[l].reshape(1, LRU_W),
        beta=jnp.concatenate([beta_attn[l], beta_lru[l]]).reshape(1, ATTN_W + LRU_W),
        w_out=w_out[l].astype(BF16),
    )


def kernel(x_prompt, x_sample, cache_k, cache_v, state_conv, state_lru, c_prompt, c_sample, w_ada, b_ada, w1_gate, w1_up, w1_down, w_in, q_gain, k_gain, sinks, conv_w, conv_b, w_rg, b_rg, w_ig, b_ig, lru_lambda, beta_attn, beta_lru, w_out, w2_gate, w2_up, w2_down):
    nb = x_prompt.shape[0]
    ns, tq, _ = x_sample.shape
    mod = _mod_call(jnp.concatenate([c_prompt, c_sample], axis=0), w_ada, b_ada)
    mod_p = mod[:, :nb].reshape(DEPTH, nb, 1, N_MOD * D_MODEL)
    mod_s = mod[:, nb:].reshape(DEPTH, ns, 1, N_MOD * D_MODEL)

    yp, ys = x_prompt, x_sample
    outs_p, outs_s = [], []
    for l in range(DEPTH):
        lw = _layer_weights(l, w_in, q_gain, k_gain, sinks, conv_w, conv_b, w_rg, b_rg, w_ig, b_ig,
                            lru_lambda, beta_attn, beta_lru, w_out)
        w1 = (w1_gate[l].astype(BF16), w1_up[l].astype(BF16), w1_down[l].astype(BF16))
        w2 = (w2_gate[l].astype(BF16), w2_up[l].astype(BF16), w2_down[l].astype(BF16))

        yp = _ffn_call(yp, mod_p[l], 0, *w1, 1, FFN_ROWS, "ffn1_prompt")
        ys = _ffn_call(ys, mod_s[l], 0, *w1, FFN_ROWS // tq, tq, "ffn1_sample")

        yp, kp, vp, cp, hp = _mix_prompt_call(yp, mod_p[l], lw)
        ys, ks, vs, cs, hs = _mix_sample_call(
            ys, mod_s[l], cache_k[l].reshape(ns, WINDOW, KV_W), cache_v[l].reshape(ns, WINDOW, KV_W),
            state_conv[l], state_lru[l], lw)
        outs_p.append((kp.reshape(nb, WINDOW, N_KV_HEADS, HEAD_DIM), vp.reshape(nb, WINDOW, N_KV_HEADS, HEAD_DIM),
                       cp[:, SUBLANES - (CONV_WIDTH - 1):], hp[:, SUBLANES - 1]))
        outs_s.append((ks.reshape(ns, WINDOW, N_KV_HEADS, HEAD_DIM), vs.reshape(ns, WINDOW, N_KV_HEADS, HEAD_DIM),
                       cs[:, tq - (CONV_WIDTH - 1):], hs[:, tq - 1]))

        yp = _ffn_call(yp, mod_p[l], 6, *w2, 1, FFN_ROWS, "ffn2_prompt")
        ys = _ffn_call(ys, mod_s[l], 6, *w2, FFN_ROWS // tq, tq, "ffn2_sample")

    stack = lambda outs, k: jnp.stack([o[k] for o in outs])
    return (yp, ys,
            stack(outs_p, 0), stack(outs_p, 1), stack(outs_p, 2), stack(outs_p, 3),
            stack(outs_s, 0), stack(outs_s, 1), stack(outs_s, 2), stack(outs_s, 3))
```

```python
import functools

import numpy as np
import jax
import jax.numpy as jnp
from jax import lax
from jax.experimental import pallas as pl
from jax.experimental.pallas import tpu as pltpu

D_MODEL = 1024
DEPTH = 2
HEAD_DIM = 64
N_HEADS = 8
N_KV_HEADS = 2
GROUP = N_HEADS // N_KV_HEADS
ATTN_W = N_HEADS * HEAD_DIM
KV_W = N_KV_HEADS * HEAD_DIM
LRU_W = 512
N_LRU_BLOCKS = 8
LRU_BLOCK = LRU_W // N_LRU_BLOCKS
CONV_WIDTH = 4
RG_C = 8.0
WINDOW = 128
Q_BLOCK = 128
D_FF = 2816
N_MOD = 9
FFN_RES = 0.5
IN_COLS = ATTN_W + 2 * KV_W + 2 * LRU_W
QK_W = ATTN_W + KV_W
RMS_EPS = 1e-6
NEG_INF = -1e30

LANES = 128
SUBLANES = 8
HALF = LANES // 2
MXU_DIM = 256
SCAN_GROUP = 4
VMEM_LIMIT = 56 * 1024 * 1024

FFN_ROWS = 512
MIX_ROWS = 512
MIX_STEPS_PER_PROJECT_STEP = 2
SAMPLE_SEQS = 16

BF16 = jnp.bfloat16
F32 = jnp.float32


def _dot(a, b):
    return jnp.dot(a, b, preferred_element_type=F32)


def _dot_nt(a, b):
    return lax.dot_general(a, b, (((1,), (1,)), ((), ())), preferred_element_type=F32)


def _rms(x):
    return x * lax.rsqrt(jnp.mean(x * x, axis=-1, keepdims=True) + RMS_EPS)


def _resident(shape):
    nd = len(shape)
    return pl.BlockSpec(shape, lambda *_: (0,) * nd, pipeline_mode=pl.Buffered(1))


def _mod_kernel(c_ref, w_ref, b_ref, o_ref):
    c = c_ref[...]
    h = (c * jax.nn.sigmoid(c)).astype(BF16)
    o_ref[0] = _dot(h, w_ref[0].astype(BF16)) + b_ref[0]


def _mod_call(c_all, w_ada, b_ada):
    n = c_all.shape[0]
    return pl.pallas_call(
        _mod_kernel,
        out_shape=jax.ShapeDtypeStruct((DEPTH, n, N_MOD * D_MODEL), F32),
        grid=(DEPTH, N_MOD),
        in_specs=[
            pl.BlockSpec((n, D_MODEL), lambda l, j: (0, 0)),
            pl.BlockSpec((1, D_MODEL, D_MODEL), lambda l, j: (l, 0, j)),
            pl.BlockSpec((1, 1, D_MODEL), lambda l, j: (l, 0, j)),
        ],
        out_specs=pl.BlockSpec((1, n, D_MODEL), lambda l, j: (l, 0, j)),
        compiler_params=pltpu.CompilerParams(
            dimension_semantics=("arbitrary", "arbitrary"), vmem_limit_bytes=VMEM_LIMIT),
        name="adaln_mod",
    )(c_all, w_ada, b_ada.reshape(DEPTH, 1, N_MOD * D_MODEL))


def _ffn_kernel(x_ref, sh_ref, sc_ref, g_ref, wg_ref, wu_ref, wd_ref, o_ref):
    x = x_ref[...]
    s, t, d = x.shape
    h = _rms(x) * (1.0 + sc_ref[...]) + sh_ref[...]
    h2 = h.reshape(s * t, d).astype(BF16)
    g = _dot(h2, wg_ref[...])
    u = _dot(h2, wu_ref[...])
    a = (g * jax.nn.sigmoid(g) * u).astype(BF16)
    y = _dot(a, wd_ref[...]).reshape(s, t, d)
    o_ref[...] = x + (FFN_RES * g_ref[...]) * y


def _mod_specs(seqs, first_chunk, index, rows=1):
    return [pl.BlockSpec((seqs, rows, D_MODEL), functools.partial(index, chunk=first_chunk + k))
            for k in range(3)]


def _ffn_call(x, mod, first_chunk, wg, wu, wd, seqs, rows, name):
    nseq, t, d = x.shape
    grid = (nseq // seqs, t // rows)
    xspec = pl.BlockSpec((seqs, rows, d), lambda i, j: (i, j, 0))
    mspecs = _mod_specs(seqs, first_chunk, lambda i, j, chunk: (i, 0, chunk))
    return pl.pallas_call(
        _ffn_kernel,
        out_shape=jax.ShapeDtypeStruct(x.shape, F32),
        grid=grid,
        in_specs=[xspec, *mspecs, _resident(wg.shape), _resident(wu.shape), _resident(wd.shape)],
        out_specs=xspec,
        compiler_params=pltpu.CompilerParams(
            dimension_semantics=("arbitrary", "arbitrary"), vmem_limit_bytes=VMEM_LIMIT),
        name=name,
    )(x, mod, mod, mod, wg, wu, wd)


def _head_mean_square(qk, avg_ref):
    parts = []
    for c0 in range(0, qk.shape[1], MXU_DIM):
        w = min(MXU_DIM, qk.shape[1] - c0)
        sq = qk[:, c0:c0 + w] * qk[:, c0:c0 + w]
        hi = sq.astype(BF16)
        lo = (sq - hi.astype(F32)).astype(BF16)
        avg = avg_ref[:w, :w]
        parts.append(_dot(hi, avg) + _dot(lo, avg))
    return jnp.concatenate(parts, axis=-1)


def _modulated(x2, sh, sc):
    return (_rms(x2) * (1.0 + sc) + sh).astype(BF16)


def _split_projection(z, ones_ref, gain):
    qk = z[:, :QK_W]
    qkn = qk * lax.rsqrt(_head_mean_square(qk, ones_ref) + RMS_EPS) * gain
    q = qkn[:, :ATTN_W]
    k = qkn[:, ATTN_W:QK_W]
    v = z[:, QK_W:QK_W + KV_W]
    xr = z[:, QK_W + KV_W:QK_W + KV_W + LRU_W]
    gr = z[:, QK_W + KV_W + LRU_W:]
    return q, k, v, xr, gr


def _shift_rows(x, k, t, fill):
    return jnp.where(t >= k, pltpu.roll(x, k, 0), fill)


def _conv(delayed, taps, bias):
    y = bias + delayed[3] * taps[0]
    y = y + delayed[2] * taps[1]
    y = y + delayed[1] * taps[2]
    return y + delayed[0] * taps[3]


def _lru_gates(xc, wgate_ref, bias):
    xb = xc.astype(BF16)
    cols = [_dot(xb[:, t * MXU_DIM:(t + 1) * MXU_DIM], wgate_ref[gate, t])
            for gate in range(2) for t in range(LRU_W // MXU_DIM)]
    return jnp.concatenate(cols, axis=-1) + bias


def _softplus_neg(lam):
    return jnp.maximum(-lam, 0.0) + jnp.log1p(jnp.exp(-jnp.abs(lam)))


def _lru_inputs(xc, gates, softplus):
    r = jax.nn.sigmoid(gates[:, :LRU_W])
    gi = jax.nn.sigmoid(gates[:, LRU_W:])
    log_a = (-RG_C * r) * softplus
    a = jnp.exp(log_a)
    w = -jnp.tanh(log_a) * (a * a + 1.0)
    u = jnp.where(w > 0.0, w * lax.rsqrt(w), 0.0) * (gi * xc)
    return a, u


def _log_scan(a, u, t, period):
    s = 1
    while s < period:
        a_prev = _shift_rows(a, s, t, 1.0)
        u_prev = _shift_rows(u, s, t, 0.0)
        u = u + a * u_prev
        a = a * a_prev
        s *= 2
    return a, u


def _slab_load(ref, start, n, stride=1):
    rows = pl.ds(start, n) if stride == 1 else pl.ds(start, n, stride=stride)
    return jnp.concatenate([ref[s, rows, :] for s in range(ref.shape[0])], axis=-1)


def _slab_store(ref, start, n, val, stride=1):
    rows = pl.ds(start, n) if stride == 1 else pl.ds(start, n, stride=stride)
    for s in range(ref.shape[0]):
        ref[s, rows, :] = val[:, s * LANES:(s + 1) * LANES]


def _blocked_scan(a_scr, u_scr, base, n, h_init):
    if n <= 2 * SUBLANES:
        t = lax.broadcasted_iota(jnp.int32, (n, 1), 0)
        a, u = _log_scan(_slab_load(a_scr, base, n), _slab_load(u_scr, base, n), t, n)
        return a * h_init + u
    m = n // SCAN_GROUP
    a_loc, h_loc = [], []
    for r in range(SCAN_GROUP):
        a_r = _slab_load(a_scr, base + r, m, SCAN_GROUP)
        u_r = _slab_load(u_scr, base + r, m, SCAN_GROUP)
        h_loc.append(u_r if r == 0 else a_r * h_loc[-1] + u_r)
        a_loc.append(a_r if r == 0 else a_r * a_loc[-1])
    _slab_store(a_scr, base + n, m, a_loc[-1])
    _slab_store(u_scr, base + n, m, h_loc[-1])
    ends = _blocked_scan(a_scr, u_scr, base + n, m, h_init)
    g = lax.broadcasted_iota(jnp.int32, (m, 1), 0)
    carry = jnp.where(g >= 1, pltpu.roll(ends, 1, 0), h_init)
    for r in range(SCAN_GROUP):
        _slab_store(u_scr, base + r, m, a_loc[r] * carry + h_loc[r], SCAN_GROUP)
    return _slab_load(u_scr, base, n)


def _scan_rows(n):
    return n if n <= 2 * SUBLANES else n + _scan_rows(n // SCAN_GROUP)


def _gelu_tanh(x):
    return 0.5 * x * (1.0 + jnp.tanh(np.sqrt(2.0 / np.pi) * (x + 0.044715 * (x * x * x))))


def _merged_heads(attn, lru, beta):
    return (jnp.concatenate([_rms(attn), _rms(lru)], axis=-1) * beta).astype(BF16)


def _softmax_with_sink(s, sink):
    m = jnp.maximum(jnp.max(s, axis=-1, keepdims=True), sink)
    p = jnp.exp(s - m)
    den = jnp.sum(p, axis=-1, keepdims=True) + jnp.exp(sink - m)
    return p, den


def _half_variants(x, low):
    xs = pltpu.roll(x, HALF, x.ndim - 1)
    zero = jnp.zeros_like(x)
    return (
        (jnp.where(low, x, zero).astype(BF16), jnp.where(low, zero, xs).astype(BF16)),
        (jnp.where(low, xs, zero).astype(BF16), jnp.where(low, zero, x).astype(BF16)),
    )


def _mix_prompt_kernel(sinks_ref, x_ref, sh_ref, sc_ref, g_ref, w_in_ref, ones_ref, gain_ref, bias_ref,
                       convw_ref, convb_ref, wgate_ref, bgate_ref, lam_ref, beta_ref, wout_ref,
                       y_ref, k_ref, v_ref, conv_ref, h_ref,
                       kprev, vprev, hcar, xr_scr, a_scr, u_scr):
    n = pl.program_id(1)

    @pl.when(n == 0)
    def _():
        kprev[...] = jnp.zeros_like(kprev)
        vprev[...] = jnp.zeros_like(vprev)
        hcar[...] = jnp.zeros_like(hcar)
        xr_scr[:, 0:SUBLANES, :] = jnp.zeros((xr_scr.shape[0], SUBLANES, LANES), F32)

    subs = x_ref.shape[1] // Q_BLOCK
    low = lax.broadcasted_iota(jnp.int32, (2 * Q_BLOCK, LANES), 1) < HALF
    carry = dict(k=kprev[...], v=vprev[...], h=hcar[...])
    st = [dict() for _ in range(subs)]

    def rep(r8):
        return jnp.broadcast_to(r8[None], (Q_BLOCK // SUBLANES,) + r8.shape).reshape(Q_BLOCK, r8.shape[-1])

    shift, scale1, gate = rep(sh_ref[0]), rep(1.0 + sc_ref[0]), rep(g_ref[0])
    gain, beta, gate_bias = rep(gain_ref[...]), rep(beta_ref[...]), rep(bgate_ref[...])
    taps, conv_bias = [rep(convw_ref[i]) for i in range(CONV_WIDTH)], rep(convb_ref[...])
    softplus = rep(_softplus_neg(lam_ref[...]))

    def rows(j):
        return pl.ds(j * Q_BLOCK, Q_BLOCK)

    def modulate(j):
        st[j]["h"] = (_rms(x_ref[0, rows(j), :]) * scale1 + shift).astype(BF16)
        st[j]["z"] = []

    def project(j, tiles):
        for c in tiles:
            st[j]["z"].append(_dot(st[j]["h"], w_in_ref[:, c * MXU_DIM:(c + 1) * MXU_DIM]))

    def split(j):
        z = jnp.concatenate(st[j].pop("z"), axis=-1)
        q, k, v, xr, gr = _split_projection(z, ones_ref, gain)
        st[j].update(q=q, k=k, v=v, xr=xr, gr=gr)

    def scores(j):
        s = st[j]
        bias_row = jnp.minimum(n, 1) if j == 0 else 1
        kvar = _half_variants(jnp.concatenate([carry["k"], s["k"]], axis=0), low)
        s["vvar"] = _half_variants(jnp.concatenate([carry["v"], s["v"]], axis=0), low)
        carry.update(k=s["k"], v=s["v"])
        s["s"] = []
        for head in range(N_HEADS):
            qt = s["q"][:, (head // 2) * LANES:(head // 2 + 1) * LANES].astype(BF16)
            s["s"].append(_dot_nt(qt, kvar[head // GROUP][head % 2]) + bias_ref[bias_row, head])

    def softmax(j):
        s = st[j]
        s["p"] = [_softmax_with_sink(sc, sinks_ref[head]) for head, sc in enumerate(s.pop("s"))]

    def attend(j):
        s = st[j]
        vvar = s.pop("vvar")
        outs = [_dot(p.astype(BF16), vvar[head // GROUP][head % 2]) / den
                for head, (p, den) in enumerate(s.pop("p"))]
        s["attn"] = jnp.concatenate([outs[2 * t] + outs[2 * t + 1] for t in range(N_HEADS // 2)], axis=-1)

    def conv(j):
        first = SUBLANES + j * Q_BLOCK
        _slab_store(xr_scr, first, Q_BLOCK, st[j]["xr"])
        st[j]["xc"] = _conv([_slab_load(xr_scr, first - kk, Q_BLOCK) for kk in range(CONV_WIDTH)],
                            taps, conv_bias)

    def gates(j):
        st[j]["gates"] = _lru_gates(st[j]["xc"], wgate_ref, gate_bias)

    def recur(j):
        s = st[j]
        a, u = _lru_inputs(s.pop("xc"), s.pop("gates"), softplus)
        _slab_store(a_scr.at[j], 0, Q_BLOCK, a)
        _slab_store(u_scr.at[j], 0, Q_BLOCK, u)
        hs = _blocked_scan(a_scr.at[j], u_scr.at[j], 0, Q_BLOCK, carry["h"])
        carry.update(h=hs[Q_BLOCK - 1:, :])
        s["lru"] = hs * _gelu_tanh(s.pop("gr"))
        if j == subs - 1:
            h_ref[0] = hs[Q_BLOCK - SUBLANES:, :]

    def merge(j):
        st[j]["merged"] = _merged_heads(st[j].pop("attn"), st[j].pop("lru"), beta)

    def output(j):
        merged = st[j].pop("merged")
        for c in range(D_MODEL // MXU_DIM):
            cols = slice(c * MXU_DIM, (c + 1) * MXU_DIM)
            y_ref[0, rows(j), cols] = x_ref[0, rows(j), cols] + gate[:, cols] * _dot(merged, wout_ref[c])

    first_tiles = range(0, 4)
    last_tiles = range(4, IN_COLS // MXU_DIM)
    modulate(0)
    project(0, first_tiles)
    project(0, last_tiles)
    for j in range(subs):
        nxt = j + 1 < subs
        split(j)
        if j > 0:
            merge(j - 1)
        scores(j)
        if j > 0:
            output(j - 1)
        conv(j)
        if nxt:
            modulate(j + 1)
        gates(j)
        if nxt:
            project(j + 1, first_tiles)
        softmax(j)
        attend(j)
        if nxt:
            project(j + 1, last_tiles)
        recur(j)
    merge(subs - 1)
    output(subs - 1)

    last = st[subs - 1]
    k_ref[0] = last["k"]
    v_ref[0] = last["v"]
    conv_ref[0] = last["xr"][Q_BLOCK - SUBLANES:, :]
    _slab_store(xr_scr, 0, SUBLANES, last["xr"][Q_BLOCK - SUBLANES:, :])
    kprev[...] = carry["k"]
    vprev[...] = carry["v"]
    hcar[...] = carry["h"]


def _prompt_bias():
    i = np.arange(Q_BLOCK)[:, None]
    j = np.arange(2 * Q_BLOCK)[None, :]
    dist = Q_BLOCK + i - j
    band = (dist >= 0) & (dist <= WINDOW)
    slopes = np.asarray([2.0 ** (-8.0 * (h + 1) / N_HEADS) for h in range(N_HEADS)], np.float32)
    alibi = -(slopes[:, None, None] * dist[None].astype(np.float32))
    general = np.where(band[None], alibi, np.float32(NEG_INF))
    first = np.where((band & (j >= Q_BLOCK))[None], alibi, np.float32(NEG_INF))
    return np.stack([first, general]).astype(np.float32)


def _mix_prompt_call(x, mod, lw):
    b, t, d = x.shape
    nb = t // MIX_ROWS
    subs = MIX_ROWS // Q_BLOCK
    xspec = pl.BlockSpec((1, MIX_ROWS, d), lambda i, j: (i, j, 0))
    mspecs = _mod_specs(1, 3, lambda i, j, chunk: (i, 0, chunk), rows=SUBLANES)
    bias = jnp.asarray(_prompt_bias())
    last = lambda shape: pl.BlockSpec(shape, lambda i, j: (i, 0, 0))
    small = [lw["conv_w"], lw["conv_b"], lw["w_gate"], lw["b_gate"], lw["lam"], lw["beta"], lw["w_out"]]
    outs = pl.pallas_call(
        _mix_prompt_kernel,
        out_shape=(
            jax.ShapeDtypeStruct(x.shape, F32),
            jax.ShapeDtypeStruct((b, WINDOW, KV_W), F32),
            jax.ShapeDtypeStruct((b, WINDOW, KV_W), F32),
            jax.ShapeDtypeStruct((b, SUBLANES, LRU_W), F32),
            jax.ShapeDtypeStruct((b, SUBLANES, LRU_W), F32),
        ),
        grid=(b, nb),
        in_specs=[
            pl.BlockSpec(memory_space=pltpu.SMEM),
            xspec, *mspecs,
            _resident(lw["w_in"].shape), _resident(lw["ones"].shape), _resident(lw["gain"].shape),
            _resident(bias.shape),
            *[_resident(a.shape) for a in small],
        ],
        out_specs=(xspec, last((1, WINDOW, KV_W)), last((1, WINDOW, KV_W)),
                   last((1, SUBLANES, LRU_W)), last((1, SUBLANES, LRU_W))),
        scratch_shapes=[
            pltpu.VMEM((Q_BLOCK, KV_W), F32), pltpu.VMEM((Q_BLOCK, KV_W), F32),
            pltpu.VMEM((1, LRU_W), F32),
            pltpu.VMEM((LRU_W // LANES, SUBLANES + MIX_ROWS, LANES), F32),
            pltpu.VMEM((subs, LRU_W // LANES, _scan_rows(Q_BLOCK), LANES), F32),
            pltpu.VMEM((subs, LRU_W // LANES, _scan_rows(Q_BLOCK), LANES), F32),
        ],
        compiler_params=pltpu.CompilerParams(
            dimension_semantics=("arbitrary", "arbitrary"), vmem_limit_bytes=VMEM_LIMIT),
        name="mix_prompt",
    )(lw["sinks"], x, mod, mod, mod, lw["w_in"], lw["ones"], lw["gain"], bias, *small)
    return outs


def _mix_sample_kernel(x_ref, sh_ref, sc_ref, g_ref, ck_ref, cv_ref, cs_ref, h0_ref,
                       w_in_ref, ones_ref, gain_ref, bias_ref, sink_ref,
                       convw_ref, convb_ref, wgate_ref, bgate_ref, lam_ref, beta_ref, wout_ref,
                       y_ref, k_ref, v_ref, conv_ref, h_ref):
    sb, tq, d = x_ref.shape
    rows = sb * tq

    def flat(a):
        return jnp.broadcast_to(a, (sb, tq, a.shape[-1])).reshape(rows, a.shape[-1])

    x2 = x_ref[...].reshape(rows, d)
    z = _dot(_modulated(x2, flat(sh_ref[...]), flat(sc_ref[...])), w_in_ref[...])
    q, k, v, xr, gr = _split_projection(z, ones_ref, gain_ref[0:1])
    k3 = k.reshape(sb, tq, KV_W)
    v3 = v.reshape(sb, tq, KV_W)
    ck = ck_ref[...]
    cv = cv_ref[...]
    k_ref[...] = jnp.concatenate([ck[:, tq:, :], k3], axis=1)
    v_ref[...] = jnp.concatenate([cv[:, tq:, :], v3], axis=1)

    pad = jnp.zeros((sb, WINDOW - tq, KV_W), F32)
    kall = jnp.concatenate([ck, k3, pad], axis=1)
    vall = jnp.concatenate([cv, v3, pad], axis=1)
    low2 = lax.broadcasted_iota(jnp.int32, (rows, LANES), 1) < HALF
    zero2 = jnp.zeros((rows, LANES), F32)
    pieces = []
    for head in range(N_HEADS):
        tile, parity, kv = head // 2, head % 2, head // GROUP
        qt = q[:, tile * LANES:(tile + 1) * LANES]
        src = qt if parity == kv else pltpu.roll(qt, HALF, 1)
        piece = jnp.where(low2, src, zero2) if kv == 0 else jnp.where(low2, zero2, src)
        pieces.append(piece.reshape(sb, tq, LANES))
    qrows = jnp.concatenate(pieces, axis=1).astype(BF16)
    s = jnp.einsum("snc,sjc->snj", qrows, kall.astype(BF16), preferred_element_type=F32)
    s = s + bias_ref[...]
    p, den = _softmax_with_sink(s, sink_ref[...])
    o = jnp.einsum("snj,sjc->snc", p.astype(BF16), vall.astype(BF16), preferred_element_type=F32) / den
    tiles = []
    for tile in range(ATTN_W // LANES):
        kv = (2 * tile) // GROUP
        oe = o[:, (2 * tile) * tq:(2 * tile + 1) * tq, :].reshape(rows, LANES)
        oo = o[:, (2 * tile + 1) * tq:(2 * tile + 2) * tq, :].reshape(rows, LANES)
        if kv == 0:
            tiles.append(jnp.where(low2, oe, pltpu.roll(oo, HALF, 1)))
        else:
            tiles.append(jnp.where(low2, pltpu.roll(oe, HALF, 1), oo))
    attn = jnp.concatenate(tiles, axis=-1)

    t = lax.broadcasted_iota(jnp.int32, (sb, tq, 1), 1).reshape(rows, 1)
    state = cs_ref[...].reshape(rows, LRU_W)
    prev = {3: state, 2: pltpu.roll(state, rows - 1, 0), 1: pltpu.roll(state, rows - 2, 0)}
    delayed = [xr] + [_shift_rows(xr, kk, t, prev[kk]) for kk in (1, 2, 3)]
    xc = _conv(delayed, [convw_ref[i, 0:1] for i in range(CONV_WIDTH)], convb_ref[0:1])
    conv_ref[...] = xr.reshape(sb, tq, LRU_W)
    gates = _lru_gates(xc, wgate_ref, bgate_ref[0:1])
    a, u = _log_scan(*_lru_inputs(xc, gates, _softplus_neg(lam_ref[0:1])), t, tq)
    hs = a * flat(h0_ref[...]) + u
    h_ref[...] = hs.reshape(sb, tq, LRU_W)

    merged = _merged_heads(attn, hs * _gelu_tanh(gr), beta_ref[0:1])
    y = jnp.concatenate([_dot(merged, wout_ref[c]) for c in range(d // MXU_DIM)], axis=-1)
    y_ref[...] = (x2 + flat(g_ref[...]) * y).reshape(sb, tq, d)


def _sample_bias(tq):
    i = np.arange(tq)[:, None]
    j = np.arange(2 * WINDOW)[None, :]
    dist = WINDOW + i - j
    ok = (dist >= 0) & (dist <= WINDOW) & (j < WINDOW + tq)
    slopes = np.asarray([2.0 ** (-8.0 * (h + 1) / N_HEADS) for h in range(N_HEADS)], np.float32)
    alibi = -(slopes[:, None, None] * dist[None].astype(np.float32))
    return np.where(ok[None], alibi, np.float32(NEG_INF)).reshape(N_HEADS * tq, 2 * WINDOW).astype(np.float32)


def _mix_sample_call(x, mod, cache_k, cache_v, conv_state, lru_state, lw):
    nseq, tq, d = x.shape
    sb = SAMPLE_SEQS
    seq_block = lambda shape: pl.BlockSpec(shape, lambda i: (i, 0, 0))
    xspec = seq_block((sb, tq, d))
    mspecs = _mod_specs(sb, 3, lambda i, chunk: (i, 0, chunk))
    bias = jnp.asarray(_sample_bias(tq))
    sink_col = jnp.repeat(lw["sinks"], tq).reshape(N_HEADS * tq, 1)
    conv_pad = jnp.pad(conv_state, ((0, 0), (0, tq - (CONV_WIDTH - 1)), (0, 0)))
    small = [lw["conv_w"], lw["conv_b"], lw["w_gate"], lw["b_gate"], lw["lam"], lw["beta"], lw["w_out"]]
    consts = [lw["w_in"], lw["ones"], lw["gain"], bias, sink_col, *small]
    return pl.pallas_call(
        _mix_sample_kernel,
        out_shape=(
            jax.ShapeDtypeStruct(x.shape, F32),
            jax.ShapeDtypeStruct((nseq, WINDOW, KV_W), F32),
            jax.ShapeDtypeStruct((nseq, WINDOW, KV_W), F32),
            jax.ShapeDtypeStruct((nseq, tq, LRU_W), F32),
            jax.ShapeDtypeStruct((nseq, tq, LRU_W), F32),
        ),
        grid=(nseq // sb,),
        in_specs=[
            xspec, *mspecs,
            seq_block((sb, WINDOW, KV_W)), seq_block((sb, WINDOW, KV_W)),
            seq_block((sb, tq, LRU_W)), seq_block((sb, 1, LRU_W)),
            *[_resident(a.shape) for a in consts],
        ],
        out_specs=(xspec, seq_block((sb, WINDOW, KV_W)), seq_block((sb, WINDOW, KV_W)),
                   seq_block((sb, tq, LRU_W)), seq_block((sb, tq, LRU_W))),
        compiler_params=pltpu.CompilerParams(
            dimension_semantics=("arbitrary",), vmem_limit_bytes=VMEM_LIMIT),
        name="mix_sample",
    )(x, mod, mod, mod, cache_k, cache_v, conv_pad, lru_state.reshape(nseq, 1, LRU_W), *consts)


def _block_diag(w):
    n, c, _ = w.shape
    eye = jnp.eye(n, dtype=w.dtype)
    return (w[:, :, None, :] * eye[:, None, :, None]).reshape(n * c, n * c)


def _diagonal_tiles(w):
    per_tile = MXU_DIM // w.shape[1]
    return jnp.stack([_block_diag(w[t:t + per_tile]) for t in range(0, w.shape[0], per_tile)])


def _column_tiles(w):
    k, n = w.shape
    return w.reshape(k, n // MXU_DIM, MXU_DIM).transpose(1, 0, 2)


def _rows8(v):
    return jnp.broadcast_to(v[..., None, :], v.shape[:-1] + (SUBLANES, v.shape[-1]))


def _layer_weights(l, w_in, q_gain, k_gain, sinks, conv_w, conv_b, w_rg, b_rg, w_ig, b_ig, lru_lambda,
                   beta_attn, beta_lru, w_out):
    head_of = np.arange(MXU_DIM) // HEAD_DIM
    ones = jnp.asarray((head_of[:, None] == head_of[None, :]) * (1.0 / HEAD_DIM), BF16)
    gain = jnp.concatenate([jnp.tile(q_gain[l] * (HEAD_DIM ** -0.5), N_HEADS), jnp.tile(k_gain[l], N_KV_HEADS)])
    return dict(
        w_in=w_in[l].astype(BF16),
        ones=ones,
        gain=_rows8(gain),
        sinks=sinks[l],
        conv_w=_rows8(conv_w[l]),
        conv_b=_rows8(conv_b[l]),
        w_gate=jnp.stack([_diagonal_tiles(w_rg[l]), _diagonal_tiles(w_ig[l])]).astype(BF16),
        b_gate=_rows8(jnp.concatenate([b_rg[l], b_ig[l]])),
        lam=_rows8(lru_lambda[l]),
        beta=_rows8(jnp.concatenate([beta_attn[l], beta_lru[l]])),
        w_out=_column_tiles(w_out[l].astype(BF16)),
    )


def kernel(x_prompt, x_sample, cache_k, cache_v, state_conv, state_lru, c_prompt, c_sample, w_ada, b_ada, w1_gate, w1_up, w1_down, w_in, q_gain, k_gain, sinks, conv_w, conv_b, w_rg, b_rg, w_ig, b_ig, lru_lambda, beta_attn, beta_lru, w_out, w2_gate, w2_up, w2_down):
    nb = x_prompt.shape[0]
    ns, tq, _ = x_sample.shape
    mod = _mod_call(jnp.concatenate([c_prompt, c_sample], axis=0), w_ada, b_ada)
    mod_p = mod[:, :nb].reshape(DEPTH, nb, 1, N_MOD * D_MODEL)
    mod_s = mod[:, nb:].reshape(DEPTH, ns, 1, N_MOD * D_MODEL)

    yp, ys = x_prompt, x_sample
    outs_p, outs_s = [], []
    for l in range(DEPTH):
        lw = _layer_weights(l, w_in, q_gain, k_gain, sinks, conv_w, conv_b, w_rg, b_rg, w_ig, b_ig,
                            lru_lambda, beta_attn, beta_lru, w_out)
        w1 = (w1_gate[l].astype(BF16), w1_up[l].astype(BF16), w1_down[l].astype(BF16))
        w2 = (w2_gate[l].astype(BF16), w2_up[l].astype(BF16), w2_down[l].astype(BF16))

        yp = _ffn_call(yp, mod_p[l], 0, *w1, 1, FFN_ROWS, "ffn1_prompt")
        ys = _ffn_call(ys, mod_s[l], 0, *w1, FFN_ROWS // tq, tq, "ffn1_sample")

        yp, kp, vp, cp, hp = _mix_prompt_call(yp, _rows8(mod_p[l, :, 0]), lw)
        ys, ks, vs, cs, hs = _mix_sample_call(
            ys, mod_s[l], cache_k[l].reshape(ns, WINDOW, KV_W), cache_v[l].reshape(ns, WINDOW, KV_W),
            state_conv[l], state_lru[l], lw)
        outs_p.append((kp.reshape(nb, WINDOW, N_KV_HEADS, HEAD_DIM), vp.reshape(nb, WINDOW, N_KV_HEADS, HEAD_DIM),
                       cp[:, SUBLANES - (CONV_WIDTH - 1):], hp[:, SUBLANES - 1]))
        outs_s.append((ks.reshape(ns, WINDOW, N_KV_HEADS, HEAD_DIM), vs.reshape(ns, WINDOW, N_KV_HEADS, HEAD_DIM),
                       cs[:, tq - (CONV_WIDTH - 1):], hs[:, tq - 1]))

        yp = _ffn_call(yp, mod_p[l], 6, *w2, 1, FFN_ROWS, "ffn2_prompt")
        ys = _ffn_call(ys, mod_s[l], 6, *w2, FFN_ROWS // tq, tq, "ffn2_sample")

    stack = lambda outs, k: jnp.stack([o[k] for o in outs])
    return (yp, ys,
            stack(outs_p, 0), stack(outs_p, 1), stack(outs_p, 2), stack(outs_p, 3),
            stack(outs_s, 0), stack(outs_s, 1), stack(outs_s, 2), stack(outs_s, 3))
```

```python
import functools

import numpy as np
import jax
import jax.numpy as jnp
from jax import lax
from jax.experimental import pallas as pl
from jax.experimental.pallas import tpu as pltpu

D_MODEL = 1024
DEPTH = 2
HEAD_DIM = 64
N_HEADS = 8
N_KV_HEADS = 2
GROUP = N_HEADS // N_KV_HEADS
ATTN_W = N_HEADS * HEAD_DIM
KV_W = N_KV_HEADS * HEAD_DIM
LRU_W = 512
N_LRU_BLOCKS = 8
LRU_BLOCK = LRU_W // N_LRU_BLOCKS
CONV_WIDTH = 4
RG_C = 8.0
WINDOW = 128
Q_BLOCK = 128
D_FF = 2816
N_MOD = 9
FFN_RES = 0.5
IN_COLS = ATTN_W + 2 * KV_W + 2 * LRU_W
QK_W = ATTN_W + KV_W
RMS_EPS = 1e-6
NEG_INF = -1e30

LANES = 128
SUBLANES = 8
HALF = LANES // 2
MXU_DIM = 256
SCAN_GROUP = 4
VMEM_LIMIT = 56 * 1024 * 1024

FFN_ROWS = 512
CAST_ROWS = 256
MIX_ROWS = 512
MIX_STEPS_PER_PROJECT_STEP = 2
SAMPLE_SEQS = 16

BF16 = jnp.bfloat16
F32 = jnp.float32


def _dot(a, b):
    return jnp.dot(a, b, preferred_element_type=F32)


def _dot_nt(a, b):
    return lax.dot_general(a, b, (((1,), (1,)), ((), ())), preferred_element_type=F32)


def _rms(x):
    return x * lax.rsqrt(jnp.mean(x * x, axis=-1, keepdims=True) + RMS_EPS)


def _resident(shape):
    nd = len(shape)
    return pl.BlockSpec(shape, lambda *_: (0,) * nd, pipeline_mode=pl.Buffered(1))


def _resident_layer(shape, layer):
    nd = len(shape)
    return pl.BlockSpec((None,) + tuple(shape[1:]), lambda *_: (layer,) + (0,) * (nd - 1),
                        pipeline_mode=pl.Buffered(1))


def _cast_kernel(*refs):
    n = len(refs) // 2
    for src, dst in zip(refs[:n], refs[n:]):
        dst[...] = src[...].astype(BF16)


def _cast_call(ws, rows):
    depth, r, c = ws[0].shape
    spec = pl.BlockSpec((1, rows, c), lambda l, i: (l, i, 0))
    return pl.pallas_call(
        _cast_kernel,
        out_shape=[jax.ShapeDtypeStruct(w.shape, BF16) for w in ws],
        grid=(depth, r // rows),
        in_specs=[spec] * len(ws),
        out_specs=[spec] * len(ws),
        compiler_params=pltpu.CompilerParams(
            dimension_semantics=("arbitrary", "arbitrary"), vmem_limit_bytes=VMEM_LIMIT),
        name="cast_weights",
    )(*ws)


def _cast_tiles_kernel(src, dst):
    for c in range(dst.shape[1]):
        dst[0, c] = src[0, :, c * MXU_DIM:(c + 1) * MXU_DIM].astype(BF16)


def _cast_column_tiles_call(w, rows):
    depth, k, n = w.shape
    return pl.pallas_call(
        _cast_tiles_kernel,
        out_shape=jax.ShapeDtypeStruct((depth, n // MXU_DIM, k, MXU_DIM), BF16),
        grid=(depth, k // rows),
        in_specs=[pl.BlockSpec((1, rows, n), lambda l, i: (l, i, 0))],
        out_specs=pl.BlockSpec((1, n // MXU_DIM, rows, MXU_DIM), lambda l, i: (l, 0, i, 0)),
        compiler_params=pltpu.CompilerParams(
            dimension_semantics=("arbitrary", "arbitrary"), vmem_limit_bytes=VMEM_LIMIT),
        name="cast_column_tiles",
    )(w)


def _mod_kernel(c_ref, w_ref, b_ref, o_ref):
    c = c_ref[...]
    h = (c * jax.nn.sigmoid(c)).astype(BF16)
    o_ref[0] = _dot(h, w_ref[0].astype(BF16)) + b_ref[0]


def _mod_call(c_all, w_ada, b_ada):
    n = c_all.shape[0]
    return pl.pallas_call(
        _mod_kernel,
        out_shape=jax.ShapeDtypeStruct((DEPTH, n, N_MOD * D_MODEL), F32),
        grid=(DEPTH, N_MOD),
        in_specs=[
            pl.BlockSpec((n, D_MODEL), lambda l, j: (0, 0)),
            pl.BlockSpec((1, D_MODEL, D_MODEL), lambda l, j: (l, 0, j)),
            pl.BlockSpec((1, 1, D_MODEL), lambda l, j: (l, 0, j)),
        ],
        out_specs=pl.BlockSpec((1, n, D_MODEL), lambda l, j: (l, 0, j)),
        compiler_params=pltpu.CompilerParams(
            dimension_semantics=("arbitrary", "arbitrary"), vmem_limit_bytes=VMEM_LIMIT),
        name="adaln_mod",
    )(c_all, w_ada, b_ada.reshape(DEPTH, 1, N_MOD * D_MODEL))


def _ffn_kernel(x_ref, sh_ref, sc_ref, g_ref, wg_ref, wu_ref, wd_ref, o_ref):
    x = x_ref[...]
    s, t, d = x.shape
    h = _rms(x) * (1.0 + sc_ref[...]) + sh_ref[...]
    h2 = h.reshape(s * t, d).astype(BF16)
    g = _dot(h2, wg_ref[...])
    u = _dot(h2, wu_ref[...])
    a = (g * jax.nn.sigmoid(g) * u).astype(BF16)
    y = _dot(a, wd_ref[...]).reshape(s, t, d)
    o_ref[...] = x + (FFN_RES * g_ref[...]) * y


STACKED_WEIGHTS = ("w_in", "w_out")


def _weight_spec(lw, key):
    if key in STACKED_WEIGHTS:
        return _resident_layer(lw[key].shape, lw["layer"])
    return _resident(lw[key].shape)


def _mod_specs(seqs, first_chunk, index, rows=1):
    return [pl.BlockSpec((seqs, rows, D_MODEL), functools.partial(index, chunk=first_chunk + k))
            for k in range(3)]


def _ffn_call(x, mod, first_chunk, weights, layer, seqs, rows, name):
    wg, wu, wd = weights
    nseq, t, d = x.shape
    grid = (nseq // seqs, t // rows)
    xspec = pl.BlockSpec((seqs, rows, d), lambda i, j: (i, j, 0))
    mspecs = _mod_specs(seqs, first_chunk, lambda i, j, chunk: (i, 0, chunk))
    return pl.pallas_call(
        _ffn_kernel,
        out_shape=jax.ShapeDtypeStruct(x.shape, F32),
        grid=grid,
        in_specs=[xspec, *mspecs, *[_resident_layer(w.shape, layer) for w in weights]],
        out_specs=xspec,
        compiler_params=pltpu.CompilerParams(
            dimension_semantics=("arbitrary", "arbitrary"), vmem_limit_bytes=VMEM_LIMIT),
        name=name,
    )(x, mod, mod, mod, wg, wu, wd)


def _head_mean_square(qk, avg_ref):
    parts = []
    for c0 in range(0, qk.shape[1], MXU_DIM):
        w = min(MXU_DIM, qk.shape[1] - c0)
        sq = qk[:, c0:c0 + w] * qk[:, c0:c0 + w]
        hi = sq.astype(BF16)
        lo = (sq - hi.astype(F32)).astype(BF16)
        avg = avg_ref[:w, :w]
        parts.append(_dot(hi, avg) + _dot(lo, avg))
    return jnp.concatenate(parts, axis=-1)


def _modulated(x2, sh, sc):
    return (_rms(x2) * (1.0 + sc) + sh).astype(BF16)


def _split_projection(z, ones_ref, gain):
    qk = z[:, :QK_W]
    qkn = qk * lax.rsqrt(_head_mean_square(qk, ones_ref) + RMS_EPS) * gain
    q = qkn[:, :ATTN_W]
    k = qkn[:, ATTN_W:QK_W]
    v = z[:, QK_W:QK_W + KV_W]
    xr = z[:, QK_W + KV_W:QK_W + KV_W + LRU_W]
    gr = z[:, QK_W + KV_W + LRU_W:]
    return q, k, v, xr, gr


def _shift_rows(x, k, t, fill):
    return jnp.where(t >= k, pltpu.roll(x, k, 0), fill)


def _conv(delayed, taps, bias):
    y = bias + delayed[3] * taps[0]
    y = y + delayed[2] * taps[1]
    y = y + delayed[1] * taps[2]
    return y + delayed[0] * taps[3]


def _lru_gates(xc, wgate_ref, bias):
    xb = xc.astype(BF16)
    cols = [_dot(xb[:, t * MXU_DIM:(t + 1) * MXU_DIM], wgate_ref[gate, t])
            for gate in range(2) for t in range(LRU_W // MXU_DIM)]
    return jnp.concatenate(cols, axis=-1) + bias


def _softplus_neg(lam):
    return jnp.maximum(-lam, 0.0) + jnp.log1p(jnp.exp(-jnp.abs(lam)))


def _lru_inputs(xc, gates, softplus):
    r = jax.nn.sigmoid(gates[:, :LRU_W])
    gi = jax.nn.sigmoid(gates[:, LRU_W:])
    log_a = (-RG_C * r) * softplus
    a = jnp.exp(log_a)
    w = -jnp.tanh(log_a) * (a * a + 1.0)
    u = jnp.where(w > 0.0, w * lax.rsqrt(w), 0.0) * (gi * xc)
    return a, u


def _log_scan(a, u, t, period):
    s = 1
    while s < period:
        a_prev = _shift_rows(a, s, t, 1.0)
        u_prev = _shift_rows(u, s, t, 0.0)
        u = u + a * u_prev
        a = a * a_prev
        s *= 2
    return a, u


def _slab_load(ref, start, n, stride=1):
    rows = pl.ds(start, n) if stride == 1 else pl.ds(start, n, stride=stride)
    return jnp.concatenate([ref[s, rows, :] for s in range(ref.shape[0])], axis=-1)


def _slab_store(ref, start, n, val, stride=1):
    rows = pl.ds(start, n) if stride == 1 else pl.ds(start, n, stride=stride)
    for s in range(ref.shape[0]):
        ref[s, rows, :] = val[:, s * LANES:(s + 1) * LANES]


def _blocked_scan(a_scr, u_scr, base, n, h_init):
    if n <= 2 * SUBLANES:
        t = lax.broadcasted_iota(jnp.int32, (n, 1), 0)
        a, u = _log_scan(_slab_load(a_scr, base, n), _slab_load(u_scr, base, n), t, n)
        return a * h_init + u
    m = n // SCAN_GROUP
    a_loc, h_loc = [], []
    for r in range(SCAN_GROUP):
        a_r = _slab_load(a_scr, base + r, m, SCAN_GROUP)
        u_r = _slab_load(u_scr, base + r, m, SCAN_GROUP)
        h_loc.append(u_r if r == 0 else a_r * h_loc[-1] + u_r)
        a_loc.append(a_r if r == 0 else a_r * a_loc[-1])
    _slab_store(a_scr, base + n, m, a_loc[-1])
    _slab_store(u_scr, base + n, m, h_loc[-1])
    ends = _blocked_scan(a_scr, u_scr, base + n, m, h_init)
    g = lax.broadcasted_iota(jnp.int32, (m, 1), 0)
    carry = jnp.where(g >= 1, pltpu.roll(ends, 1, 0), h_init)
    for r in range(SCAN_GROUP):
        _slab_store(u_scr, base + r, m, a_loc[r] * carry + h_loc[r], SCAN_GROUP)
    return _slab_load(u_scr, base, n)


def _scan_rows(n):
    return n if n <= 2 * SUBLANES else n + _scan_rows(n // SCAN_GROUP)


def _gelu_tanh(x):
    return 0.5 * x * (1.0 + jnp.tanh(np.sqrt(2.0 / np.pi) * (x + 0.044715 * (x * x * x))))


def _merged_heads(attn, lru, beta):
    return (jnp.concatenate([_rms(attn), _rms(lru)], axis=-1) * beta).astype(BF16)


def _softmax_with_sink(s, sink):
    m = jnp.maximum(jnp.max(s, axis=-1, keepdims=True), sink)
    p = jnp.exp(s - m)
    den = jnp.sum(p, axis=-1, keepdims=True) + jnp.exp(sink - m)
    return p, den


def _half_variants(x, low):
    xs = pltpu.roll(x, HALF, x.ndim - 1)
    zero = jnp.zeros_like(x)
    return (
        (jnp.where(low, x, zero).astype(BF16), jnp.where(low, zero, xs).astype(BF16)),
        (jnp.where(low, xs, zero).astype(BF16), jnp.where(low, zero, x).astype(BF16)),
    )


def _mix_prompt_kernel(sinks_ref, x_ref, sh_ref, sc_ref, g_ref, w_in_ref, ones_ref, gain_ref, bias_ref,
                       convw_ref, convb_ref, wgate_ref, bgate_ref, lam_ref, beta_ref, wout_ref,
                       y_ref, k_ref, v_ref, conv_ref, h_ref,
                       kprev, vprev, hcar, xr_scr, a_scr, u_scr):
    n = pl.program_id(1)

    @pl.when(n == 0)
    def _():
        kprev[...] = jnp.zeros_like(kprev)
        vprev[...] = jnp.zeros_like(vprev)
        hcar[...] = jnp.zeros_like(hcar)
        xr_scr[:, 0:SUBLANES, :] = jnp.zeros((xr_scr.shape[0], SUBLANES, LANES), F32)

    subs = x_ref.shape[1] // Q_BLOCK
    low = lax.broadcasted_iota(jnp.int32, (2 * Q_BLOCK, LANES), 1) < HALF
    carry = dict(k=kprev[...], v=vprev[...], h=hcar[...])
    st = [dict() for _ in range(subs)]

    def rep(r8):
        return jnp.broadcast_to(r8[None], (Q_BLOCK // SUBLANES,) + r8.shape).reshape(Q_BLOCK, r8.shape[-1])

    shift, scale1, gate = rep(sh_ref[0]), rep(1.0 + sc_ref[0]), rep(g_ref[0])
    gain, beta, gate_bias = rep(gain_ref[...]), rep(beta_ref[...]), rep(bgate_ref[...])
    taps, conv_bias = [rep(convw_ref[i]) for i in range(CONV_WIDTH)], rep(convb_ref[...])
    softplus = rep(_softplus_neg(lam_ref[...]))

    def rows(j):
        return pl.ds(j * Q_BLOCK, Q_BLOCK)

    def modulate(j):
        st[j]["h"] = (_rms(x_ref[0, rows(j), :]) * scale1 + shift).astype(BF16)
        st[j]["z"] = []

    def project(j, tiles):
        for c in tiles:
            st[j]["z"].append(_dot(st[j]["h"], w_in_ref[:, c * MXU_DIM:(c + 1) * MXU_DIM]))

    def split(j):
        z = jnp.concatenate(st[j].pop("z"), axis=-1)
        q, k, v, xr, gr = _split_projection(z, ones_ref, gain)
        st[j].update(q=q, k=k, v=v, xr=xr, gr=gr)

    def scores(j):
        s = st[j]
        bias_row = jnp.minimum(n, 1) if j == 0 else 1
        kvar = _half_variants(jnp.concatenate([carry["k"], s["k"]], axis=0), low)
        s["vvar"] = _half_variants(jnp.concatenate([carry["v"], s["v"]], axis=0), low)
        carry.update(k=s["k"], v=s["v"])
        s["s"] = []
        for head in range(N_HEADS):
            qt = s["q"][:, (head // 2) * LANES:(head // 2 + 1) * LANES].astype(BF16)
            s["s"].append(_dot_nt(qt, kvar[head // GROUP][head % 2]) + bias_ref[bias_row, head])

    def softmax(j):
        s = st[j]
        s["p"] = [_softmax_with_sink(sc, sinks_ref[head]) for head, sc in enumerate(s.pop("s"))]

    def attend(j):
        s = st[j]
        vvar = s.pop("vvar")
        outs = [_dot(p.astype(BF16), vvar[head // GROUP][head % 2]) / den
                for head, (p, den) in enumerate(s.pop("p"))]
        s["attn"] = jnp.concatenate([outs[2 * t] + outs[2 * t + 1] for t in range(N_HEADS // 2)], axis=-1)

    def conv(j):
        first = SUBLANES + j * Q_BLOCK
        _slab_store(xr_scr, first, Q_BLOCK, st[j]["xr"])
        st[j]["xc"] = _conv([_slab_load(xr_scr, first - kk, Q_BLOCK) for kk in range(CONV_WIDTH)],
                            taps, conv_bias)

    def gates(j):
        st[j]["gates"] = _lru_gates(st[j]["xc"], wgate_ref, gate_bias)

    def recur(j):
        s = st[j]
        a, u = _lru_inputs(s.pop("xc"), s.pop("gates"), softplus)
        _slab_store(a_scr.at[j], 0, Q_BLOCK, a)
        _slab_store(u_scr.at[j], 0, Q_BLOCK, u)
        hs = _blocked_scan(a_scr.at[j], u_scr.at[j], 0, Q_BLOCK, carry["h"])
        carry.update(h=hs[Q_BLOCK - 1:, :])
        s["lru"] = hs * _gelu_tanh(s.pop("gr"))
        if j == subs - 1:
            h_ref[0] = hs[Q_BLOCK - SUBLANES:, :]

    def merge(j):
        st[j]["merged"] = _merged_heads(st[j].pop("attn"), st[j].pop("lru"), beta)

    def output(j):
        merged = st[j].pop("merged")
        for c in range(D_MODEL // MXU_DIM):
            cols = slice(c * MXU_DIM, (c + 1) * MXU_DIM)
            y_ref[0, rows(j), cols] = x_ref[0, rows(j), cols] + gate[:, cols] * _dot(merged, wout_ref[c])

    first_tiles = range(0, 4)
    last_tiles = range(4, IN_COLS // MXU_DIM)
    modulate(0)
    project(0, first_tiles)
    project(0, last_tiles)
    for j in range(subs):
        nxt = j + 1 < subs
        split(j)
        if j > 0:
            merge(j - 1)
        scores(j)
        if j > 0:
            output(j - 1)
        conv(j)
        if nxt:
            modulate(j + 1)
        gates(j)
        if nxt:
            project(j + 1, first_tiles)
        softmax(j)
        attend(j)
        if nxt:
            project(j + 1, last_tiles)
        recur(j)
    merge(subs - 1)
    output(subs - 1)

    last = st[subs - 1]
    k_ref[0] = last["k"]
    v_ref[0] = last["v"]
    conv_ref[0] = last["xr"][Q_BLOCK - SUBLANES:, :]
    _slab_store(xr_scr, 0, SUBLANES, last["xr"][Q_BLOCK - SUBLANES:, :])
    kprev[...] = carry["k"]
    vprev[...] = carry["v"]
    hcar[...] = carry["h"]


def _prompt_bias():
    i = np.arange(Q_BLOCK)[:, None]
    j = np.arange(2 * Q_BLOCK)[None, :]
    dist = Q_BLOCK + i - j
    band = (dist >= 0) & (dist <= WINDOW)
    slopes = np.asarray([2.0 ** (-8.0 * (h + 1) / N_HEADS) for h in range(N_HEADS)], np.float32)
    alibi = -(slopes[:, None, None] * dist[None].astype(np.float32))
    general = np.where(band[None], alibi, np.float32(NEG_INF))
    first = np.where((band & (j >= Q_BLOCK))[None], alibi, np.float32(NEG_INF))
    return np.stack([first, general]).astype(np.float32)


def _mix_prompt_call(x, mod, lw):
    b, t, d = x.shape
    nb = t // MIX_ROWS
    subs = MIX_ROWS // Q_BLOCK
    xspec = pl.BlockSpec((1, MIX_ROWS, d), lambda i, j: (i, j, 0))
    mspecs = _mod_specs(1, 3, lambda i, j, chunk: (i, 0, chunk), rows=SUBLANES)
    bias = jnp.asarray(_prompt_bias())
    last = lambda shape: pl.BlockSpec(shape, lambda i, j: (i, 0, 0))
    head = ["w_in", "ones", "gain"]
    tail = ["conv_w", "conv_b", "w_gate", "b_gate", "lam", "beta", "w_out"]
    outs = pl.pallas_call(
        _mix_prompt_kernel,
        out_shape=(
            jax.ShapeDtypeStruct(x.shape, F32),
            jax.ShapeDtypeStruct((b, WINDOW, KV_W), F32),
            jax.ShapeDtypeStruct((b, WINDOW, KV_W), F32),
            jax.ShapeDtypeStruct((b, SUBLANES, LRU_W), F32),
            jax.ShapeDtypeStruct((b, SUBLANES, LRU_W), F32),
        ),
        grid=(b, nb),
        in_specs=[
            pl.BlockSpec(memory_space=pltpu.SMEM),
            xspec, *mspecs,
            *[_weight_spec(lw, key) for key in head],
            _resident(bias.shape),
            *[_weight_spec(lw, key) for key in tail],
        ],
        out_specs=(xspec, last((1, WINDOW, KV_W)), last((1, WINDOW, KV_W)),
                   last((1, SUBLANES, LRU_W)), last((1, SUBLANES, LRU_W))),
        scratch_shapes=[
            pltpu.VMEM((Q_BLOCK, KV_W), F32), pltpu.VMEM((Q_BLOCK, KV_W), F32),
            pltpu.VMEM((1, LRU_W), F32),
            pltpu.VMEM((LRU_W // LANES, SUBLANES + MIX_ROWS, LANES), F32),
            pltpu.VMEM((subs, LRU_W // LANES, _scan_rows(Q_BLOCK), LANES), F32),
            pltpu.VMEM((subs, LRU_W // LANES, _scan_rows(Q_BLOCK), LANES), F32),
        ],
        compiler_params=pltpu.CompilerParams(
            dimension_semantics=("arbitrary", "arbitrary"), vmem_limit_bytes=VMEM_LIMIT),
        name="mix_prompt",
    )(lw["sinks"], x, mod, mod, mod, *[lw[key] for key in head], bias, *[lw[key] for key in tail])
    return outs


def _mix_sample_kernel(x_ref, sh_ref, sc_ref, g_ref, ck_ref, cv_ref, cs_ref, h0_ref,
                       w_in_ref, ones_ref, gain_ref, bias_ref, sink_ref,
                       convw_ref, convb_ref, wgate_ref, bgate_ref, lam_ref, beta_ref, wout_ref,
                       y_ref, k_ref, v_ref, conv_ref, h_ref):
    sb, tq, d = x_ref.shape
    rows = sb * tq

    def flat(a):
        return jnp.broadcast_to(a, (sb, tq, a.shape[-1])).reshape(rows, a.shape[-1])

    x2 = x_ref[...].reshape(rows, d)
    z = _dot(_modulated(x2, flat(sh_ref[...]), flat(sc_ref[...])), w_in_ref[...])
    q, k, v, xr, gr = _split_projection(z, ones_ref, gain_ref[0:1])
    k3 = k.reshape(sb, tq, KV_W)
    v3 = v.reshape(sb, tq, KV_W)
    ck = ck_ref[...]
    cv = cv_ref[...]
    k_ref[...] = jnp.concatenate([ck[:, tq:, :], k3], axis=1)
    v_ref[...] = jnp.concatenate([cv[:, tq:, :], v3], axis=1)

    pad = jnp.zeros((sb, WINDOW - tq, KV_W), F32)
    kall = jnp.concatenate([ck, k3, pad], axis=1)
    vall = jnp.concatenate([cv, v3, pad], axis=1)
    low2 = lax.broadcasted_iota(jnp.int32, (rows, LANES), 1) < HALF
    zero2 = jnp.zeros((rows, LANES), F32)
    pieces = []
    for head in range(N_HEADS):
        tile, parity, kv = head // 2, head % 2, head // GROUP
        qt = q[:, tile * LANES:(tile + 1) * LANES]
        src = qt if parity == kv else pltpu.roll(qt, HALF, 1)
        piece = jnp.where(low2, src, zero2) if kv == 0 else jnp.where(low2, zero2, src)
        pieces.append(piece.reshape(sb, tq, LANES))
    qrows = jnp.concatenate(pieces, axis=1).astype(BF16)
    s = jnp.einsum("snc,sjc->snj", qrows, kall.astype(BF16), preferred_element_type=F32)
    s = s + bias_ref[...]
    p, den = _softmax_with_sink(s, sink_ref[...])
    o = jnp.einsum("snj,sjc->snc", p.astype(BF16), vall.astype(BF16), preferred_element_type=F32) / den
    tiles = []
    for tile in range(ATTN_W // LANES):
        kv = (2 * tile) // GROUP
        oe = o[:, (2 * tile) * tq:(2 * tile + 1) * tq, :].reshape(rows, LANES)
        oo = o[:, (2 * tile + 1) * tq:(2 * tile + 2) * tq, :].reshape(rows, LANES)
        if kv == 0:
            tiles.append(jnp.where(low2, oe, pltpu.roll(oo, HALF, 1)))
        else:
            tiles.append(jnp.where(low2, pltpu.roll(oe, HALF, 1), oo))
    attn = jnp.concatenate(tiles, axis=-1)

    t = lax.broadcasted_iota(jnp.int32, (sb, tq, 1), 1).reshape(rows, 1)
    state = cs_ref[...].reshape(rows, LRU_W)
    prev = {3: state, 2: pltpu.roll(state, rows - 1, 0), 1: pltpu.roll(state, rows - 2, 0)}
    delayed = [xr] + [_shift_rows(xr, kk, t, prev[kk]) for kk in (1, 2, 3)]
    xc = _conv(delayed, [convw_ref[i, 0:1] for i in range(CONV_WIDTH)], convb_ref[0:1])
    conv_ref[...] = xr.reshape(sb, tq, LRU_W)
    gates = _lru_gates(xc, wgate_ref, bgate_ref[0:1])
    a, u = _log_scan(*_lru_inputs(xc, gates, _softplus_neg(lam_ref[0:1])), t, tq)
    hs = a * flat(h0_ref[...]) + u
    h_ref[...] = hs.reshape(sb, tq, LRU_W)

    merged = _merged_heads(attn, hs * _gelu_tanh(gr), beta_ref[0:1])
    y = jnp.concatenate([_dot(merged, wout_ref[c]) for c in range(d // MXU_DIM)], axis=-1)
    y_ref[...] = (x2 + flat(g_ref[...]) * y).reshape(sb, tq, d)


def _sample_bias(tq):
    i = np.arange(tq)[:, None]
    j = np.arange(2 * WINDOW)[None, :]
    dist = WINDOW + i - j
    ok = (dist >= 0) & (dist <= WINDOW) & (j < WINDOW + tq)
    slopes = np.asarray([2.0 ** (-8.0 * (h + 1) / N_HEADS) for h in range(N_HEADS)], np.float32)
    alibi = -(slopes[:, None, None] * dist[None].astype(np.float32))
    return np.where(ok[None], alibi, np.float32(NEG_INF)).reshape(N_HEADS * tq, 2 * WINDOW).astype(np.float32)


def _mix_sample_call(x, mod, k_windows, v_windows, conv_state, lru_state, lw):
    nseq, tq, d = x.shape
    sb = SAMPLE_SEQS
    seq_block = lambda shape: pl.BlockSpec(shape, lambda i: (i, 0, 0))
    first_block = lw["layer"] * (nseq // sb)
    window_block = pl.BlockSpec((sb, WINDOW, KV_W), lambda i: (first_block + i, 0, 0))
    xspec = seq_block((sb, tq, d))
    mspecs = _mod_specs(sb, 3, lambda i, chunk: (i, 0, chunk))
    bias = jnp.asarray(_sample_bias(tq))
    sink_col = jnp.repeat(lw["sinks"], tq).reshape(N_HEADS * tq, 1)
    conv_pad = jnp.pad(conv_state, ((0, 0), (0, tq - (CONV_WIDTH - 1)), (0, 0)))
    head = ["w_in", "ones", "gain"]
    tail = ["conv_w", "conv_b", "w_gate", "b_gate", "lam", "beta", "w_out"]
    return pl.pallas_call(
        _mix_sample_kernel,
        out_shape=(
            jax.ShapeDtypeStruct(x.shape, F32),
            jax.ShapeDtypeStruct(k_windows.shape, F32),
            jax.ShapeDtypeStruct(v_windows.shape, F32),
            jax.ShapeDtypeStruct((nseq, tq, LRU_W), F32),
            jax.ShapeDtypeStruct((nseq, tq, LRU_W), F32),
        ),
        grid=(nseq // sb,),
        in_specs=[
            xspec, *mspecs,
            window_block, window_block,
            seq_block((sb, tq, LRU_W)), seq_block((sb, 1, LRU_W)),
            *[_weight_spec(lw, key) for key in head],
            _resident(bias.shape), _resident(sink_col.shape),
            *[_weight_spec(lw, key) for key in tail],
        ],
        out_specs=(xspec, window_block, window_block,
                   seq_block((sb, tq, LRU_W)), seq_block((sb, tq, LRU_W))),
        input_output_aliases={4: 1, 5: 2},
        compiler_params=pltpu.CompilerParams(
            dimension_semantics=("arbitrary",), vmem_limit_bytes=VMEM_LIMIT),
        name="mix_sample",
    )(x, mod, mod, mod, k_windows, v_windows, conv_pad, lru_state.reshape(nseq, 1, LRU_W),
      *[lw[key] for key in head], bias, sink_col, *[lw[key] for key in tail])


def _block_diag(w):
    n, c, _ = w.shape
    eye = jnp.eye(n, dtype=w.dtype)
    return (w[:, :, None, :] * eye[:, None, :, None]).reshape(n * c, n * c)


def _diagonal_tiles(w):
    per_tile = MXU_DIM // w.shape[1]
    return jnp.stack([_block_diag(w[t:t + per_tile]) for t in range(0, w.shape[0], per_tile)])


def _rows8(v):
    return jnp.broadcast_to(v[..., None, :], v.shape[:-1] + (SUBLANES, v.shape[-1]))


def _layer_weights(l, w_in_bf16, q_gain, k_gain, sinks, conv_w, conv_b, w_rg, b_rg, w_ig, b_ig, lru_lambda,
                   beta_attn, beta_lru, w_out_tiles):
    head_of = np.arange(MXU_DIM) // HEAD_DIM
    ones = jnp.asarray((head_of[:, None] == head_of[None, :]) * (1.0 / HEAD_DIM), BF16)
    gain = jnp.concatenate([jnp.tile(q_gain[l] * (HEAD_DIM ** -0.5), N_HEADS), jnp.tile(k_gain[l], N_KV_HEADS)])
    return dict(
        layer=l,
        w_in=w_in_bf16,
        ones=ones,
        gain=_rows8(gain),
        sinks=sinks[l],
        conv_w=_rows8(conv_w[l]),
        conv_b=_rows8(conv_b[l]),
        w_gate=jnp.stack([_diagonal_tiles(w_rg[l]), _diagonal_tiles(w_ig[l])]).astype(BF16),
        b_gate=_rows8(jnp.concatenate([b_rg[l], b_ig[l]])),
        lam=_rows8(lru_lambda[l]),
        beta=_rows8(jnp.concatenate([beta_attn[l], beta_lru[l]])),
        w_out=w_out_tiles,
    )


def kernel(x_prompt, x_sample, cache_k, cache_v, state_conv, state_lru, c_prompt, c_sample, w_ada, b_ada, w1_gate, w1_up, w1_down, w_in, q_gain, k_gain, sinks, conv_w, conv_b, w_rg, b_rg, w_ig, b_ig, lru_lambda, beta_attn, beta_lru, w_out, w2_gate, w2_up, w2_down):
    nb = x_prompt.shape[0]
    ns, tq, _ = x_sample.shape
    mod = _mod_call(jnp.concatenate([c_prompt, c_sample], axis=0), w_ada, b_ada)
    mod_p = mod[:, :nb].reshape(DEPTH, nb, 1, N_MOD * D_MODEL)
    mod_s = mod[:, nb:].reshape(DEPTH, ns, 1, N_MOD * D_MODEL)

    w1g, w1u, w2g, w2u = _cast_call([w1_gate, w1_up, w2_gate, w2_up], CAST_ROWS)
    w1d, w2d = _cast_call([w1_down, w2_down], D_FF // (D_MODEL // CAST_ROWS))
    (w_in_bf16,) = _cast_call([w_in], CAST_ROWS)
    w_out_tiles = _cast_column_tiles_call(w_out, CAST_ROWS)
    w1, w2 = (w1g, w1u, w1d), (w2g, w2u, w2d)

    k_windows = cache_k.reshape(DEPTH * ns, WINDOW, KV_W)
    v_windows = cache_v.reshape(DEPTH * ns, WINDOW, KV_W)
    yp, ys = x_prompt, x_sample
    outs_p, outs_s = [], []
    for l in range(DEPTH):
        lw = _layer_weights(l, w_in_bf16, q_gain, k_gain, sinks, conv_w, conv_b, w_rg, b_rg, w_ig, b_ig,
                            lru_lambda, beta_attn, beta_lru, w_out_tiles)

        yp = _ffn_call(yp, mod_p[l], 0, w1, l, 1, FFN_ROWS, "ffn1_prompt")
        ys = _ffn_call(ys, mod_s[l], 0, w1, l, FFN_ROWS // tq, tq, "ffn1_sample")

        yp, kp, vp, cp, hp = _mix_prompt_call(yp, _rows8(mod_p[l, :, 0]), lw)
        ys, k_windows, v_windows, cs, hs = _mix_sample_call(
            ys, mod_s[l], k_windows, v_windows, state_conv[l], state_lru[l], lw)
        outs_p.append((kp.reshape(nb, WINDOW, N_KV_HEADS, HEAD_DIM), vp.reshape(nb, WINDOW, N_KV_HEADS, HEAD_DIM),
                       cp[:, SUBLANES - (CONV_WIDTH - 1):], hp[:, SUBLANES - 1]))
        outs_s.append((cs[:, tq - (CONV_WIDTH - 1):], hs[:, tq - 1]))

        yp = _ffn_call(yp, mod_p[l], 6, w2, l, 1, FFN_ROWS, "ffn2_prompt")
        ys = _ffn_call(ys, mod_s[l], 6, w2, l, FFN_ROWS // tq, tq, "ffn2_sample")

    stack = lambda outs, k: jnp.stack([o[k] for o in outs])
    return (yp, ys,
            stack(outs_p, 0), stack(outs_p, 1), stack(outs_p, 2), stack(outs_p, 3),
            k_windows.reshape(cache_k.shape), v_windows.reshape(cache_v.shape),
            stack(outs_s, 0), stack(outs_s, 1))
```

```python
import functools

import numpy as np
import jax
import jax.numpy as jnp
from jax import lax
from jax.experimental import pallas as pl
from jax.experimental.pallas import tpu as pltpu

D_MODEL = 1024
DEPTH = 2
HEAD_DIM = 64
N_HEADS = 8
N_KV_HEADS = 2
GROUP = N_HEADS // N_KV_HEADS
ATTN_W = N_HEADS * HEAD_DIM
KV_W = N_KV_HEADS * HEAD_DIM
LRU_W = 512
N_LRU_BLOCKS = 8
LRU_BLOCK = LRU_W // N_LRU_BLOCKS
CONV_WIDTH = 4
RG_C = 8.0
WINDOW = 128
Q_BLOCK = 128
D_FF = 2816
N_MOD = 9
FFN_RES = 0.5
IN_COLS = ATTN_W + 2 * KV_W + 2 * LRU_W
QK_W = ATTN_W + KV_W
RMS_EPS = 1e-6
NEG_INF = -1e30

LANES = 128
SUBLANES = 8
HALF = LANES // 2
MXU_DIM = 256
SCAN_GROUP = 4
VMEM_LIMIT = 56 * 1024 * 1024

FFN_ROWS = 512
FFN_PARTS = 2
CAST_ROWS = 256
MIX_ROWS = 512
MIX_STEPS_PER_PROJECT_STEP = 2
SAMPLE_SEQS = 16

BF16 = jnp.bfloat16
F32 = jnp.float32


def _dot(a, b):
    return jnp.dot(a, b, preferred_element_type=F32)


def _dot_nt(a, b):
    return lax.dot_general(a, b, (((1,), (1,)), ((), ())), preferred_element_type=F32)


def _rms(x):
    return x * lax.rsqrt(jnp.mean(x * x, axis=-1, keepdims=True) + RMS_EPS)


def _resident(shape):
    nd = len(shape)
    return pl.BlockSpec(shape, lambda *_: (0,) * nd, pipeline_mode=pl.Buffered(1))


def _resident_layer(shape, layer):
    nd = len(shape)
    return pl.BlockSpec((None,) + tuple(shape[1:]), lambda *_: (layer,) + (0,) * (nd - 1),
                        pipeline_mode=pl.Buffered(1))


def _cast_kernel(*refs):
    n = len(refs) // 2
    for src, dst in zip(refs[:n], refs[n:]):
        dst[...] = src[...].astype(BF16)


def _cast_call(ws, rows):
    depth, r, c = ws[0].shape
    spec = pl.BlockSpec((1, rows, c), lambda l, i: (l, i, 0))
    return pl.pallas_call(
        _cast_kernel,
        out_shape=[jax.ShapeDtypeStruct(w.shape, BF16) for w in ws],
        grid=(depth, r // rows),
        in_specs=[spec] * len(ws),
        out_specs=[spec] * len(ws),
        compiler_params=pltpu.CompilerParams(
            dimension_semantics=("arbitrary", "arbitrary"), vmem_limit_bytes=VMEM_LIMIT),
        name="cast_weights",
    )(*ws)


def _cast_tiles_kernel(src, dst):
    for c in range(dst.shape[1]):
        dst[0, c] = src[0, :, c * MXU_DIM:(c + 1) * MXU_DIM].astype(BF16)


def _cast_column_tiles_call(w, rows):
    depth, k, n = w.shape
    return pl.pallas_call(
        _cast_tiles_kernel,
        out_shape=jax.ShapeDtypeStruct((depth, n // MXU_DIM, k, MXU_DIM), BF16),
        grid=(depth, k // rows),
        in_specs=[pl.BlockSpec((1, rows, n), lambda l, i: (l, i, 0))],
        out_specs=pl.BlockSpec((1, n // MXU_DIM, rows, MXU_DIM), lambda l, i: (l, 0, i, 0)),
        compiler_params=pltpu.CompilerParams(
            dimension_semantics=("arbitrary", "arbitrary"), vmem_limit_bytes=VMEM_LIMIT),
        name="cast_column_tiles",
    )(w)


def _mod_kernel(c_ref, w_ref, b_ref, o_ref):
    c = c_ref[...]
    h = (c * jax.nn.sigmoid(c)).astype(BF16)
    o_ref[0] = _dot(h, w_ref[0].astype(BF16)) + b_ref[0]


def _mod_call(c_all, w_ada, b_ada):
    n = c_all.shape[0]
    return pl.pallas_call(
        _mod_kernel,
        out_shape=jax.ShapeDtypeStruct((DEPTH, n, N_MOD * D_MODEL), F32),
        grid=(DEPTH, N_MOD),
        in_specs=[
            pl.BlockSpec((n, D_MODEL), lambda l, j: (0, 0)),
            pl.BlockSpec((1, D_MODEL, D_MODEL), lambda l, j: (l, 0, j)),
            pl.BlockSpec((1, 1, D_MODEL), lambda l, j: (l, 0, j)),
        ],
        out_specs=pl.BlockSpec((1, n, D_MODEL), lambda l, j: (l, 0, j)),
        compiler_params=pltpu.CompilerParams(
            dimension_semantics=("arbitrary", "arbitrary"), vmem_limit_bytes=VMEM_LIMIT),
        name="adaln_mod",
    )(c_all, w_ada, b_ada.reshape(DEPTH, 1, N_MOD * D_MODEL))


def _ffn_kernel(x_ref, sh_ref, sc_ref, g_ref, wg_ref, wu_ref, wd_ref, o_ref, *, parts):
    s, t, d = x_ref.shape
    along_t = s == 1
    ps, pt = (s, t // parts) if along_t else (s // parts, t)

    def piece(ref, p):
        if along_t:
            return ref[:, pl.ds(p * pt, pt), :] if ref.shape[1] == t else ref[...]
        return ref[pl.ds(p * ps, ps)]

    def modulated(p):
        h = _rms(piece(x_ref, p)) * (1.0 + piece(sc_ref, p)) + piece(sh_ref, p)
        return h.reshape(ps * pt, d).astype(BF16)

    h2 = modulated(0)
    g_next = _dot(h2, wg_ref[...])
    u_next = _dot(h2, wu_ref[...])
    for p in range(parts):
        g, u = g_next, u_next
        if p + 1 < parts:
            h2 = modulated(p + 1)
            g_next = _dot(h2, wg_ref[...])
        a = (g * jax.nn.sigmoid(g) * u).astype(BF16)
        y = _dot(a, wd_ref[...]).reshape(ps, pt, d)
        if along_t:
            o_ref[:, pl.ds(p * pt, pt), :] = piece(x_ref, p) + (FFN_RES * piece(g_ref, p)) * y
        else:
            o_ref[pl.ds(p * ps, ps)] = piece(x_ref, p) + (FFN_RES * piece(g_ref, p)) * y
        if p + 1 < parts:
            u_next = _dot(h2, wu_ref[...])


STACKED_WEIGHTS = ("w_in", "w_out")


def _weight_spec(lw, key):
    if key in STACKED_WEIGHTS:
        return _resident_layer(lw[key].shape, lw["layer"])
    return _resident(lw[key].shape)


def _mod_specs(seqs, first_chunk, index, rows=1):
    return [pl.BlockSpec((seqs, rows, D_MODEL), functools.partial(index, chunk=first_chunk + k))
            for k in range(3)]


def _ffn_call(x, mod, first_chunk, weights, layer, seqs, rows, name):
    wg, wu, wd = weights
    nseq, t, d = x.shape
    grid = (nseq // seqs, t // rows)
    xspec = pl.BlockSpec((seqs, rows, d), lambda i, j: (i, j, 0))
    mspecs = _mod_specs(seqs, first_chunk, lambda i, j, chunk: (i, 0, chunk))
    return pl.pallas_call(
        functools.partial(_ffn_kernel, parts=FFN_PARTS),
        out_shape=jax.ShapeDtypeStruct(x.shape, F32),
        grid=grid,
        in_specs=[xspec, *mspecs, *[_resident_layer(w.shape, layer) for w in weights]],
        out_specs=xspec,
        compiler_params=pltpu.CompilerParams(
            dimension_semantics=("arbitrary", "arbitrary"), vmem_limit_bytes=VMEM_LIMIT),
        name=name,
    )(x, mod, mod, mod, wg, wu, wd)


def _head_mean_square(qk, avg_ref):
    parts = []
    for c0 in range(0, qk.shape[1], MXU_DIM):
        w = min(MXU_DIM, qk.shape[1] - c0)
        sq = qk[:, c0:c0 + w] * qk[:, c0:c0 + w]
        hi = sq.astype(BF16)
        lo = (sq - hi.astype(F32)).astype(BF16)
        avg = avg_ref[:w, :w]
        parts.append(_dot(hi, avg) + _dot(lo, avg))
    return jnp.concatenate(parts, axis=-1)


def _modulated(x2, sh, sc):
    return (_rms(x2) * (1.0 + sc) + sh).astype(BF16)


def _split_projection(z, ones_ref, gain):
    qk = z[:, :QK_W]
    qkn = qk * lax.rsqrt(_head_mean_square(qk, ones_ref) + RMS_EPS) * gain
    q = qkn[:, :ATTN_W]
    k = qkn[:, ATTN_W:QK_W]
    v = z[:, QK_W:QK_W + KV_W]
    xr = z[:, QK_W + KV_W:QK_W + KV_W + LRU_W]
    gr = z[:, QK_W + KV_W + LRU_W:]
    return q, k, v, xr, gr


def _shift_rows(x, k, t, fill):
    return jnp.where(t >= k, pltpu.roll(x, k, 0), fill)


def _conv(delayed, taps, bias):
    y = bias + delayed[3] * taps[0]
    y = y + delayed[2] * taps[1]
    y = y + delayed[1] * taps[2]
    return y + delayed[0] * taps[3]


def _lru_gates(xc, wgate_ref, bias):
    xb = xc.astype(BF16)
    cols = [_dot(xb[:, t * MXU_DIM:(t + 1) * MXU_DIM], wgate_ref[gate, t])
            for gate in range(2) for t in range(LRU_W // MXU_DIM)]
    return jnp.concatenate(cols, axis=-1) + bias


def _softplus_neg(lam):
    return jnp.maximum(-lam, 0.0) + jnp.log1p(jnp.exp(-jnp.abs(lam)))


def _lru_inputs(xc, gates, softplus):
    r = jax.nn.sigmoid(gates[:, :LRU_W])
    gi = jax.nn.sigmoid(gates[:, LRU_W:])
    log_a = (-RG_C * r) * softplus
    a = jnp.exp(log_a)
    w = -jnp.tanh(log_a) * (a * a + 1.0)
    u = jnp.where(w > 0.0, w * lax.rsqrt(w), 0.0) * (gi * xc)
    return a, u


def _log_scan(a, u, t, period):
    s = 1
    while s < period:
        a_prev = _shift_rows(a, s, t, 1.0)
        u_prev = _shift_rows(u, s, t, 0.0)
        u = u + a * u_prev
        a = a * a_prev
        s *= 2
    return a, u


def _slab_load(ref, start, n, stride=1):
    rows = pl.ds(start, n) if stride == 1 else pl.ds(start, n, stride=stride)
    return jnp.concatenate([ref[s, rows, :] for s in range(ref.shape[0])], axis=-1)


def _slab_store(ref, start, n, val, stride=1):
    rows = pl.ds(start, n) if stride == 1 else pl.ds(start, n, stride=stride)
    for s in range(ref.shape[0]):
        ref[s, rows, :] = val[:, s * LANES:(s + 1) * LANES]


def _blocked_scan(a_scr, u_scr, base, n, h_init):
    if n <= 2 * SUBLANES:
        t = lax.broadcasted_iota(jnp.int32, (n, 1), 0)
        a, u = _log_scan(_slab_load(a_scr, base, n), _slab_load(u_scr, base, n), t, n)
        return a * h_init + u
    m = n // SCAN_GROUP
    a_loc, h_loc = [], []
    for r in range(SCAN_GROUP):
        a_r = _slab_load(a_scr, base + r, m, SCAN_GROUP)
        u_r = _slab_load(u_scr, base + r, m, SCAN_GROUP)
        h_loc.append(u_r if r == 0 else a_r * h_loc[-1] + u_r)
        a_loc.append(a_r if r == 0 else a_r * a_loc[-1])
    _slab_store(a_scr, base + n, m, a_loc[-1])
    _slab_store(u_scr, base + n, m, h_loc[-1])
    ends = _blocked_scan(a_scr, u_scr, base + n, m, h_init)
    g = lax.broadcasted_iota(jnp.int32, (m, 1), 0)
    carry = jnp.where(g >= 1, pltpu.roll(ends, 1, 0), h_init)
    for r in range(SCAN_GROUP):
        _slab_store(u_scr, base + r, m, a_loc[r] * carry + h_loc[r], SCAN_GROUP)
    return _slab_load(u_scr, base, n)


def _scan_rows(n):
    return n if n <= 2 * SUBLANES else n + _scan_rows(n // SCAN_GROUP)


def _gelu_tanh(x):
    return 0.5 * x * (1.0 + jnp.tanh(np.sqrt(2.0 / np.pi) * (x + 0.044715 * (x * x * x))))


def _merged_heads(attn, lru, beta):
    return (jnp.concatenate([_rms(attn), _rms(lru)], axis=-1) * beta).astype(BF16)


def _softmax_with_sink(s, sink):
    m = jnp.maximum(jnp.max(s, axis=-1, keepdims=True), sink)
    p = jnp.exp(s - m)
    den = jnp.sum(p, axis=-1, keepdims=True) + jnp.exp(sink - m)
    return p, den


def _half_variants(x, low):
    xs = pltpu.roll(x, HALF, x.ndim - 1)
    zero = jnp.zeros_like(x)
    return (
        (jnp.where(low, x, zero).astype(BF16), jnp.where(low, zero, xs).astype(BF16)),
        (jnp.where(low, xs, zero).astype(BF16), jnp.where(low, zero, x).astype(BF16)),
    )


def _mix_prompt_kernel(sinks_ref, x_ref, sh_ref, sc_ref, g_ref, w_in_ref, ones_ref, gain_ref, bias_ref,
                       convw_ref, convb_ref, wgate_ref, bgate_ref, lam_ref, beta_ref, wout_ref,
                       y_ref, k_ref, v_ref, conv_ref, h_ref,
                       kprev, vprev, hcar, xr_scr, a_scr, u_scr):
    n = pl.program_id(1)

    @pl.when(n == 0)
    def _():
        kprev[...] = jnp.zeros_like(kprev)
        vprev[...] = jnp.zeros_like(vprev)
        hcar[...] = jnp.zeros_like(hcar)
        xr_scr[:, 0:SUBLANES, :] = jnp.zeros((xr_scr.shape[0], SUBLANES, LANES), F32)

    subs = x_ref.shape[1] // Q_BLOCK
    low = lax.broadcasted_iota(jnp.int32, (2 * Q_BLOCK, LANES), 1) < HALF
    carry = dict(k=kprev[...], v=vprev[...], h=hcar[...])
    st = [dict() for _ in range(subs)]

    def rep(r8):
        return jnp.broadcast_to(r8[None], (Q_BLOCK // SUBLANES,) + r8.shape).reshape(Q_BLOCK, r8.shape[-1])

    shift, scale1, gate = rep(sh_ref[0]), rep(1.0 + sc_ref[0]), rep(g_ref[0])
    gain, beta, gate_bias = rep(gain_ref[...]), rep(beta_ref[...]), rep(bgate_ref[...])
    taps, conv_bias = [rep(convw_ref[i]) for i in range(CONV_WIDTH)], rep(convb_ref[...])
    softplus = rep(_softplus_neg(lam_ref[...]))

    def rows(j):
        return pl.ds(j * Q_BLOCK, Q_BLOCK)

    def modulate(j):
        st[j]["h"] = (_rms(x_ref[0, rows(j), :]) * scale1 + shift).astype(BF16)
        st[j]["z"] = []

    def project(j, tiles):
        for c in tiles:
            st[j]["z"].append(_dot(st[j]["h"], w_in_ref[:, c * MXU_DIM:(c + 1) * MXU_DIM]))

    def split(j):
        z = jnp.concatenate(st[j].pop("z"), axis=-1)
        q, k, v, xr, gr = _split_projection(z, ones_ref, gain)
        st[j].update(q=q, k=k, v=v, xr=xr, gr=gr)

    def scores(j):
        s = st[j]
        bias_row = jnp.minimum(n, 1) if j == 0 else 1
        kvar = _half_variants(jnp.concatenate([carry["k"], s["k"]], axis=0), low)
        s["vvar"] = _half_variants(jnp.concatenate([carry["v"], s["v"]], axis=0), low)
        carry.update(k=s["k"], v=s["v"])
        s["s"] = []
        for head in range(N_HEADS):
            qt = s["q"][:, (head // 2) * LANES:(head // 2 + 1) * LANES].astype(BF16)
            s["s"].append(_dot_nt(qt, kvar[head // GROUP][head % 2]) + bias_ref[bias_row, head])

    def softmax(j):
        s = st[j]
        s["p"] = [_softmax_with_sink(sc, sinks_ref[head]) for head, sc in enumerate(s.pop("s"))]

    def attend(j):
        s = st[j]
        vvar = s.pop("vvar")
        outs = [_dot(p.astype(BF16), vvar[head // GROUP][head % 2]) / den
                for head, (p, den) in enumerate(s.pop("p"))]
        s["attn"] = jnp.concatenate([outs[2 * t] + outs[2 * t + 1] for t in range(N_HEADS // 2)], axis=-1)

    def conv(j):
        first = SUBLANES + j * Q_BLOCK
        _slab_store(xr_scr, first, Q_BLOCK, st[j]["xr"])
        st[j]["xc"] = _conv([_slab_load(xr_scr, first - kk, Q_BLOCK) for kk in range(CONV_WIDTH)],
                            taps, conv_bias)

    def gates(j):
        st[j]["gates"] = _lru_gates(st[j]["xc"], wgate_ref, gate_bias)

    def recur(j):
        s = st[j]
        a, u = _lru_inputs(s.pop("xc"), s.pop("gates"), softplus)
        _slab_store(a_scr.at[j], 0, Q_BLOCK, a)
        _slab_store(u_scr.at[j], 0, Q_BLOCK, u)
        hs = _blocked_scan(a_scr.at[j], u_scr.at[j], 0, Q_BLOCK, carry["h"])
        carry.update(h=hs[Q_BLOCK - 1:, :])
        s["lru"] = hs * _gelu_tanh(s.pop("gr"))
        if j == subs - 1:
            h_ref[0] = hs[Q_BLOCK - SUBLANES:, :]

    def merge(j):
        st[j]["merged"] = _merged_heads(st[j].pop("attn"), st[j].pop("lru"), beta)

    def output(j):
        merged = st[j].pop("merged")
        for c in range(D_MODEL // MXU_DIM):
            cols = slice(c * MXU_DIM, (c + 1) * MXU_DIM)
            y_ref[0, rows(j), cols] = x_ref[0, rows(j), cols] + gate[:, cols] * _dot(merged, wout_ref[c])

    first_tiles = range(0, 4)
    last_tiles = range(4, IN_COLS // MXU_DIM)
    modulate(0)
    project(0, first_tiles)
    project(0, last_tiles)
    for j in range(subs):
        nxt = j + 1 < subs
        split(j)
        if j > 0:
            merge(j - 1)
        scores(j)
        if j > 0:
            output(j - 1)
        conv(j)
        if nxt:
            modulate(j + 1)
        gates(j)
        if nxt:
            project(j + 1, first_tiles)
        softmax(j)
        attend(j)
        if nxt:
            project(j + 1, last_tiles)
        recur(j)
    merge(subs - 1)
    output(subs - 1)

    last = st[subs - 1]
    k_ref[0] = last["k"]
    v_ref[0] = last["v"]
    conv_ref[0] = last["xr"][Q_BLOCK - SUBLANES:, :]
    _slab_store(xr_scr, 0, SUBLANES, last["xr"][Q_BLOCK - SUBLANES:, :])
    kprev[...] = carry["k"]
    vprev[...] = carry["v"]
    hcar[...] = carry["h"]


def _prompt_bias():
    i = np.arange(Q_BLOCK)[:, None]
    j = np.arange(2 * Q_BLOCK)[None, :]
    dist = Q_BLOCK + i - j
    band = (dist >= 0) & (dist <= WINDOW)
    slopes = np.asarray([2.0 ** (-8.0 * (h + 1) / N_HEADS) for h in range(N_HEADS)], np.float32)
    alibi = -(slopes[:, None, None] * dist[None].astype(np.float32))
    general = np.where(band[None], alibi, np.float32(NEG_INF))
    first = np.where((band & (j >= Q_BLOCK))[None], alibi, np.float32(NEG_INF))
    return np.stack([first, general]).astype(np.float32)


def _mix_prompt_call(x, mod, lw):
    b, t, d = x.shape
    nb = t // MIX_ROWS
    subs = MIX_ROWS // Q_BLOCK
    xspec = pl.BlockSpec((1, MIX_ROWS, d), lambda i, j: (i, j, 0))
    mspecs = _mod_specs(1, 3, lambda i, j, chunk: (i, 0, chunk), rows=SUBLANES)
    bias = jnp.asarray(_prompt_bias())
    last = lambda shape: pl.BlockSpec(shape, lambda i, j: (i, 0, 0))
    head = ["w_in", "ones", "gain"]
    tail = ["conv_w", "conv_b", "w_gate", "b_gate", "lam", "beta", "w_out"]
    outs = pl.pallas_call(
        _mix_prompt_kernel,
        out_shape=(
            jax.ShapeDtypeStruct(x.shape, F32),
            jax.ShapeDtypeStruct((b, WINDOW, KV_W), F32),
            jax.ShapeDtypeStruct((b, WINDOW, KV_W), F32),
            jax.ShapeDtypeStruct((b, SUBLANES, LRU_W), F32),
            jax.ShapeDtypeStruct((b, SUBLANES, LRU_W), F32),
        ),
        grid=(b, nb),
        in_specs=[
            pl.BlockSpec(memory_space=pltpu.SMEM),
            xspec, *mspecs,
            *[_weight_spec(lw, key) for key in head],
            _resident(bias.shape),
            *[_weight_spec(lw, key) for key in tail],
        ],
        out_specs=(xspec, last((1, WINDOW, KV_W)), last((1, WINDOW, KV_W)),
                   last((1, SUBLANES, LRU_W)), last((1, SUBLANES, LRU_W))),
        scratch_shapes=[
            pltpu.VMEM((Q_BLOCK, KV_W), F32), pltpu.VMEM((Q_BLOCK, KV_W), F32),
            pltpu.VMEM((1, LRU_W), F32),
            pltpu.VMEM((LRU_W // LANES, SUBLANES + MIX_ROWS, LANES), F32),
            pltpu.VMEM((subs, LRU_W // LANES, _scan_rows(Q_BLOCK), LANES), F32),
            pltpu.VMEM((subs, LRU_W // LANES, _scan_rows(Q_BLOCK), LANES), F32),
        ],
        compiler_params=pltpu.CompilerParams(
            dimension_semantics=("arbitrary", "arbitrary"), vmem_limit_bytes=VMEM_LIMIT),
        name="mix_prompt",
    )(lw["sinks"], x, mod, mod, mod, *[lw[key] for key in head], bias, *[lw[key] for key in tail])
    return outs


def _mix_sample_kernel(x_ref, sh_ref, sc_ref, g_ref, ck_ref, cv_ref, cs_ref, h0_ref,
                       w_in_ref, ones_ref, gain_ref, bias_ref, sink_ref,
                       convw_ref, convb_ref, wgate_ref, bgate_ref, lam_ref, beta_ref, wout_ref,
                       y_ref, k_ref, v_ref, conv_ref, h_ref):
    sb, tq, d = x_ref.shape
    rows = sb * tq

    def flat(a):
        return jnp.broadcast_to(a, (sb, tq, a.shape[-1])).reshape(rows, a.shape[-1])

    x2 = x_ref[...].reshape(rows, d)
    z = _dot(_modulated(x2, flat(sh_ref[...]), flat(sc_ref[...])), w_in_ref[...])
    q, k, v, xr, gr = _split_projection(z, ones_ref, gain_ref[0:1])
    k3 = k.reshape(sb, tq, KV_W)
    v3 = v.reshape(sb, tq, KV_W)
    ck = ck_ref[...]
    cv = cv_ref[...]
    k_ref[...] = jnp.concatenate([ck[:, tq:, :], k3], axis=1)
    v_ref[...] = jnp.concatenate([cv[:, tq:, :], v3], axis=1)

    pad = jnp.zeros((sb, WINDOW - tq, KV_W), F32)
    kall = jnp.concatenate([ck, k3, pad], axis=1)
    vall = jnp.concatenate([cv, v3, pad], axis=1)
    low2 = lax.broadcasted_iota(jnp.int32, (rows, LANES), 1) < HALF
    zero2 = jnp.zeros((rows, LANES), F32)
    pieces = []
    for head in range(N_HEADS):
        tile, parity, kv = head // 2, head % 2, head // GROUP
        qt = q[:, tile * LANES:(tile + 1) * LANES]
        src = qt if parity == kv else pltpu.roll(qt, HALF, 1)
        piece = jnp.where(low2, src, zero2) if kv == 0 else jnp.where(low2, zero2, src)
        pieces.append(piece.reshape(sb, tq, LANES))
    qrows = jnp.concatenate(pieces, axis=1).astype(BF16)
    s = jnp.einsum("snc,sjc->snj", qrows, kall.astype(BF16), preferred_element_type=F32)
    s = s + bias_ref[...]
    p, den = _softmax_with_sink(s, sink_ref[...])
    o = jnp.einsum("snj,sjc->snc", p.astype(BF16), vall.astype(BF16), preferred_element_type=F32) / den
    tiles = []
    for tile in range(ATTN_W // LANES):
        kv = (2 * tile) // GROUP
        oe = o[:, (2 * tile) * tq:(2 * tile + 1) * tq, :].reshape(rows, LANES)
        oo = o[:, (2 * tile + 1) * tq:(2 * tile + 2) * tq, :].reshape(rows, LANES)
        if kv == 0:
            tiles.append(jnp.where(low2, oe, pltpu.roll(oo, HALF, 1)))
        else:
            tiles.append(jnp.where(low2, pltpu.roll(oe, HALF, 1), oo))
    attn = jnp.concatenate(tiles, axis=-1)

    t = lax.broadcasted_iota(jnp.int32, (sb, tq, 1), 1).reshape(rows, 1)
    state = cs_ref[...].reshape(rows, LRU_W)
    prev = {3: state, 2: pltpu.roll(state, rows - 1, 0), 1: pltpu.roll(state, rows - 2, 0)}
    delayed = [xr] + [_shift_rows(xr, kk, t, prev[kk]) for kk in (1, 2, 3)]
    xc = _conv(delayed, [convw_ref[i, 0:1] for i in range(CONV_WIDTH)], convb_ref[0:1])
    conv_ref[...] = xr.reshape(sb, tq, LRU_W)
    gates = _lru_gates(xc, wgate_ref, bgate_ref[0:1])
    a, u = _log_scan(*_lru_inputs(xc, gates, _softplus_neg(lam_ref[0:1])), t, tq)
    hs = a * flat(h0_ref[...]) + u
    h_ref[...] = hs.reshape(sb, tq, LRU_W)

    merged = _merged_heads(attn, hs * _gelu_tanh(gr), beta_ref[0:1])
    y = jnp.concatenate([_dot(merged, wout_ref[c]) for c in range(d // MXU_DIM)], axis=-1)
    y_ref[...] = (x2 + flat(g_ref[...]) * y).reshape(sb, tq, d)


def _sample_bias(tq):
    i = np.arange(tq)[:, None]
    j = np.arange(2 * WINDOW)[None, :]
    dist = WINDOW + i - j
    ok = (dist >= 0) & (dist <= WINDOW) & (j < WINDOW + tq)
    slopes = np.asarray([2.0 ** (-8.0 * (h + 1) / N_HEADS) for h in range(N_HEADS)], np.float32)
    alibi = -(slopes[:, None, None] * dist[None].astype(np.float32))
    return np.where(ok[None], alibi, np.float32(NEG_INF)).reshape(N_HEADS * tq, 2 * WINDOW).astype(np.float32)


def _mix_sample_call(x, mod, k_windows, v_windows, conv_state, lru_state, lw):
    nseq, tq, d = x.shape
    sb = SAMPLE_SEQS
    seq_block = lambda shape: pl.BlockSpec(shape, lambda i: (i, 0, 0))
    first_block = lw["layer"] * (nseq // sb)
    window_block = pl.BlockSpec((sb, WINDOW, KV_W), lambda i: (first_block + i, 0, 0))
    xspec = seq_block((sb, tq, d))
    mspecs = _mod_specs(sb, 3, lambda i, chunk: (i, 0, chunk))
    bias = jnp.asarray(_sample_bias(tq))
    sink_col = jnp.repeat(lw["sinks"], tq).reshape(N_HEADS * tq, 1)
    conv_pad = jnp.pad(conv_state, ((0, 0), (0, tq - (CONV_WIDTH - 1)), (0, 0)))
    head = ["w_in", "ones", "gain"]
    tail = ["conv_w", "conv_b", "w_gate", "b_gate", "lam", "beta", "w_out"]
    return pl.pallas_call(
        _mix_sample_kernel,
        out_shape=(
            jax.ShapeDtypeStruct(x.shape, F32),
            jax.ShapeDtypeStruct(k_windows.shape, F32),
            jax.ShapeDtypeStruct(v_windows.shape, F32),
            jax.ShapeDtypeStruct((nseq, tq, LRU_W), F32),
            jax.ShapeDtypeStruct((nseq, tq, LRU_W), F32),
        ),
        grid=(nseq // sb,),
        in_specs=[
            xspec, *mspecs,
            window_block, window_block,
            seq_block((sb, tq, LRU_W)), seq_block((sb, 1, LRU_W)),
            *[_weight_spec(lw, key) for key in head],
            _resident(bias.shape), _resident(sink_col.shape),
            *[_weight_spec(lw, key) for key in tail],
        ],
        out_specs=(xspec, window_block, window_block,
                   seq_block((sb, tq, LRU_W)), seq_block((sb, tq, LRU_W))),
        input_output_aliases={4: 1, 5: 2},
        compiler_params=pltpu.CompilerParams(
            dimension_semantics=("arbitrary",), vmem_limit_bytes=VMEM_LIMIT),
        name="mix_sample",
    )(x, mod, mod, mod, k_windows, v_windows, conv_pad, lru_state.reshape(nseq, 1, LRU_W),
      *[lw[key] for key in head], bias, sink_col, *[lw[key] for key in tail])


def _block_diag(w):
    n, c, _ = w.shape
    eye = jnp.eye(n, dtype=w.dtype)
    return (w[:, :, None, :] * eye[:, None, :, None]).reshape(n * c, n * c)


def _diagonal_tiles(w):
    per_tile = MXU_DIM // w.shape[1]
    return jnp.stack([_block_diag(w[t:t + per_tile]) for t in range(0, w.shape[0], per_tile)])


def _rows8(v):
    return jnp.broadcast_to(v[..., None, :], v.shape[:-1] + (SUBLANES, v.shape[-1]))


def _layer_weights(l, w_in_bf16, q_gain, k_gain, sinks, conv_w, conv_b, w_rg, b_rg, w_ig, b_ig, lru_lambda,
                   beta_attn, beta_lru, w_out_tiles):
    head_of = np.arange(MXU_DIM) // HEAD_DIM
    ones = jnp.asarray((head_of[:, None] == head_of[None, :]) * (1.0 / HEAD_DIM), BF16)
    gain = jnp.concatenate([jnp.tile(q_gain[l] * (HEAD_DIM ** -0.5), N_HEADS), jnp.tile(k_gain[l], N_KV_HEADS)])
    return dict(
        layer=l,
        w_in=w_in_bf16,
        ones=ones,
        gain=_rows8(gain),
        sinks=sinks[l],
        conv_w=_rows8(conv_w[l]),
        conv_b=_rows8(conv_b[l]),
        w_gate=jnp.stack([_diagonal_tiles(w_rg[l]), _diagonal_tiles(w_ig[l])]).astype(BF16),
        b_gate=_rows8(jnp.concatenate([b_rg[l], b_ig[l]])),
        lam=_rows8(lru_lambda[l]),
        beta=_rows8(jnp.concatenate([beta_attn[l], beta_lru[l]])),
        w_out=w_out_tiles,
    )


def kernel(x_prompt, x_sample, cache_k, cache_v, state_conv, state_lru, c_prompt, c_sample, w_ada, b_ada, w1_gate, w1_up, w1_down, w_in, q_gain, k_gain, sinks, conv_w, conv_b, w_rg, b_rg, w_ig, b_ig, lru_lambda, beta_attn, beta_lru, w_out, w2_gate, w2_up, w2_down):
    nb = x_prompt.shape[0]
    ns, tq, _ = x_sample.shape
    mod = _mod_call(jnp.concatenate([c_prompt, c_sample], axis=0), w_ada, b_ada)
    mod_p = mod[:, :nb].reshape(DEPTH, nb, 1, N_MOD * D_MODEL)
    mod_s = mod[:, nb:].reshape(DEPTH, ns, 1, N_MOD * D_MODEL)

    w1g, w1u, w2g, w2u = _cast_call([w1_gate, w1_up, w2_gate, w2_up], CAST_ROWS)
    w1d, w2d = _cast_call([w1_down, w2_down], D_FF // (D_MODEL // CAST_ROWS))
    (w_in_bf16,) = _cast_call([w_in], CAST_ROWS)
    w_out_tiles = _cast_column_tiles_call(w_out, CAST_ROWS)
    w1, w2 = (w1g, w1u, w1d), (w2g, w2u, w2d)

    k_windows = cache_k.reshape(DEPTH * ns, WINDOW, KV_W)
    v_windows = cache_v.reshape(DEPTH * ns, WINDOW, KV_W)
    yp, ys = x_prompt, x_sample
    outs_p, outs_s = [], []
    for l in range(DEPTH):
        lw = _layer_weights(l, w_in_bf16, q_gain, k_gain, sinks, conv_w, conv_b, w_rg, b_rg, w_ig, b_ig,
                            lru_lambda, beta_attn, beta_lru, w_out_tiles)

        yp = _ffn_call(yp, mod_p[l], 0, w1, l, 1, FFN_ROWS * FFN_PARTS, "ffn1_prompt")
        ys = _ffn_call(ys, mod_s[l], 0, w1, l, FFN_ROWS * FFN_PARTS // tq, tq, "ffn1_sample")

        yp, kp, vp, cp, hp = _mix_prompt_call(yp, _rows8(mod_p[l, :, 0]), lw)
        ys, k_windows, v_windows, cs, hs = _mix_sample_call(
            ys, mod_s[l], k_windows, v_windows, state_conv[l], state_lru[l], lw)
        outs_p.append((kp.reshape(nb, WINDOW, N_KV_HEADS, HEAD_DIM), vp.reshape(nb, WINDOW, N_KV_HEADS, HEAD_DIM),
                       cp[:, SUBLANES - (CONV_WIDTH - 1):], hp[:, SUBLANES - 1]))
        outs_s.append((cs[:, tq - (CONV_WIDTH - 1):], hs[:, tq - 1]))

        yp = _ffn_call(yp, mod_p[l], 6, w2, l, 1, FFN_ROWS * FFN_PARTS, "ffn2_prompt")
        ys = _ffn_call(ys, mod_s[l], 6, w2, l, FFN_ROWS * FFN_PARTS // tq, tq, "ffn2_sample")

    stack = lambda outs, k: jnp.stack([o[k] for o in outs])
    return (yp, ys,
            stack(outs_p, 0), stack(outs_p, 1), stack(outs_p, 2), stack(outs_p, 3),
            k_windows.reshape(cache_k.shape), v_windows.reshape(cache_v.shape),
            stack(outs_s, 0), stack(outs_s, 1))
```

```python
import functools

import numpy as np
import jax
import jax.numpy as jnp
from jax import lax
from jax.experimental import pallas as pl
from jax.experimental.pallas import tpu as pltpu

D_MODEL = 1024
DEPTH = 2
HEAD_DIM = 64
N_HEADS = 8
N_KV_HEADS = 2
GROUP = N_HEADS // N_KV_HEADS
ATTN_W = N_HEADS * HEAD_DIM
KV_W = N_KV_HEADS * HEAD_DIM
LRU_W = 512
N_LRU_BLOCKS = 8
LRU_BLOCK = LRU_W // N_LRU_BLOCKS
CONV_WIDTH = 4
RG_C = 8.0
WINDOW = 128
Q_BLOCK = 128
D_FF = 2816
N_MOD = 9
FFN_RES = 0.5
IN_COLS = ATTN_W + 2 * KV_W + 2 * LRU_W
QK_W = ATTN_W + KV_W
RMS_EPS = 1e-6
NEG_INF = -1e30

LANES = 128
SUBLANES = 8
HALF = LANES // 2
MXU_DIM = 256
SCAN_GROUP = 4
VMEM_LIMIT = 56 * 1024 * 1024

FFN_ROWS = 512
FFN_PARTS = 2
CAST_ROWS = 256
MIX_ROWS = 1024
SUB_ROWS = 256
SAMPLE_SEQS = 16

BF16 = jnp.bfloat16
F32 = jnp.float32


def _dot(a, b):
    return jnp.dot(a, b, preferred_element_type=F32)


def _dot_nt(a, b):
    return lax.dot_general(a, b, (((1,), (1,)), ((), ())), preferred_element_type=F32)


def _rms(x):
    return x * lax.rsqrt(jnp.mean(x * x, axis=-1, keepdims=True) + RMS_EPS)


def _resident(shape):
    nd = len(shape)
    return pl.BlockSpec(shape, lambda *_: (0,) * nd, pipeline_mode=pl.Buffered(1))


def _resident_layer(shape, layer):
    nd = len(shape)
    return pl.BlockSpec((None,) + tuple(shape[1:]), lambda *_: (layer,) + (0,) * (nd - 1),
                        pipeline_mode=pl.Buffered(1))


def _cast_kernel(*refs):
    n = len(refs) // 2
    for src, dst in zip(refs[:n], refs[n:]):
        dst[...] = src[...].astype(BF16)


def _cast_call(ws, rows):
    depth, r, c = ws[0].shape
    spec = pl.BlockSpec((1, rows, c), lambda l, i: (l, i, 0))
    return pl.pallas_call(
        _cast_kernel,
        out_shape=[jax.ShapeDtypeStruct(w.shape, BF16) for w in ws],
        grid=(depth, r // rows),
        in_specs=[spec] * len(ws),
        out_specs=[spec] * len(ws),
        compiler_params=pltpu.CompilerParams(
            dimension_semantics=("arbitrary", "arbitrary"), vmem_limit_bytes=VMEM_LIMIT),
        name="cast_weights",
    )(*ws)


def _cast_tiles_kernel(src, dst):
    for c in range(dst.shape[1]):
        dst[0, c] = src[0, :, c * MXU_DIM:(c + 1) * MXU_DIM].astype(BF16)


def _cast_column_tiles_call(w, rows):
    depth, k, n = w.shape
    return pl.pallas_call(
        _cast_tiles_kernel,
        out_shape=jax.ShapeDtypeStruct((depth, n // MXU_DIM, k, MXU_DIM), BF16),
        grid=(depth, k // rows),
        in_specs=[pl.BlockSpec((1, rows, n), lambda l, i: (l, i, 0))],
        out_specs=pl.BlockSpec((1, n // MXU_DIM, rows, MXU_DIM), lambda l, i: (l, 0, i, 0)),
        compiler_params=pltpu.CompilerParams(
            dimension_semantics=("arbitrary", "arbitrary"), vmem_limit_bytes=VMEM_LIMIT),
        name="cast_column_tiles",
    )(w)


def _mod_kernel(c_ref, w_ref, b_ref, o_ref):
    c = c_ref[...]
    h = (c * jax.nn.sigmoid(c)).astype(BF16)
    o_ref[0] = _dot(h, w_ref[0].astype(BF16)) + b_ref[0]


def _mod_call(c_all, w_ada, b_ada):
    n = c_all.shape[0]
    return pl.pallas_call(
        _mod_kernel,
        out_shape=jax.ShapeDtypeStruct((DEPTH, n, N_MOD * D_MODEL), F32),
        grid=(DEPTH, N_MOD),
        in_specs=[
            pl.BlockSpec((n, D_MODEL), lambda l, j: (0, 0)),
            pl.BlockSpec((1, D_MODEL, D_MODEL), lambda l, j: (l, 0, j)),
            pl.BlockSpec((1, 1, D_MODEL), lambda l, j: (l, 0, j)),
        ],
        out_specs=pl.BlockSpec((1, n, D_MODEL), lambda l, j: (l, 0, j)),
        compiler_params=pltpu.CompilerParams(
            dimension_semantics=("arbitrary", "arbitrary"), vmem_limit_bytes=VMEM_LIMIT),
        name="adaln_mod",
    )(c_all, w_ada, b_ada.reshape(DEPTH, 1, N_MOD * D_MODEL))


def _ffn_kernel(x_ref, sh_ref, sc_ref, g_ref, wg_ref, wu_ref, wd_ref, o_ref, *, parts):
    s, t, d = x_ref.shape
    along_t = s == 1
    ps, pt = (s, t // parts) if along_t else (s // parts, t)

    def piece(ref, p):
        if along_t:
            return ref[:, pl.ds(p * pt, pt), :] if ref.shape[1] == t else ref[...]
        return ref[pl.ds(p * ps, ps)]

    def modulated(p):
        h = _rms(piece(x_ref, p)) * (1.0 + piece(sc_ref, p)) + piece(sh_ref, p)
        return h.reshape(ps * pt, d).astype(BF16)

    h2 = modulated(0)
    g_next = _dot(h2, wg_ref[...])
    u_next = _dot(h2, wu_ref[...])
    for p in range(parts):
        g, u = g_next, u_next
        if p + 1 < parts:
            h2 = modulated(p + 1)
            g_next = _dot(h2, wg_ref[...])
        a = (g * jax.nn.sigmoid(g) * u).astype(BF16)
        y = _dot(a, wd_ref[...]).reshape(ps, pt, d)
        if along_t:
            o_ref[:, pl.ds(p * pt, pt), :] = piece(x_ref, p) + (FFN_RES * piece(g_ref, p)) * y
        else:
            o_ref[pl.ds(p * ps, ps)] = piece(x_ref, p) + (FFN_RES * piece(g_ref, p)) * y
        if p + 1 < parts:
            u_next = _dot(h2, wu_ref[...])


STACKED_WEIGHTS = ("w_in", "w_out")


def _weight_spec(lw, key):
    if key in STACKED_WEIGHTS:
        return _resident_layer(lw[key].shape, lw["layer"])
    return _resident(lw[key].shape)


def _mod_specs(seqs, first_chunk, index, rows=1):
    return [pl.BlockSpec((seqs, rows, D_MODEL), functools.partial(index, chunk=first_chunk + k))
            for k in range(3)]


def _ffn_call(x, mod, first_chunk, weights, layer, seqs, rows, name):
    wg, wu, wd = weights
    nseq, t, d = x.shape
    parts = seqs * rows // FFN_ROWS
    grid = (nseq // seqs, t // rows)
    xspec = pl.BlockSpec((seqs, rows, d), lambda i, j: (i, j, 0))
    mspecs = _mod_specs(seqs, first_chunk, lambda i, j, chunk: (i, 0, chunk))
    return pl.pallas_call(
        functools.partial(_ffn_kernel, parts=parts),
        out_shape=jax.ShapeDtypeStruct(x.shape, F32),
        grid=grid,
        in_specs=[xspec, *mspecs, *[_resident_layer(w.shape, layer) for w in weights]],
        out_specs=xspec,
        compiler_params=pltpu.CompilerParams(
            dimension_semantics=("arbitrary", "arbitrary"), vmem_limit_bytes=VMEM_LIMIT),
        name=name,
    )(x, mod, mod, mod, wg, wu, wd)


def _head_mean_square(qk, avg_ref):
    parts = []
    for c0 in range(0, qk.shape[1], MXU_DIM):
        w = min(MXU_DIM, qk.shape[1] - c0)
        sq = qk[:, c0:c0 + w] * qk[:, c0:c0 + w]
        hi = sq.astype(BF16)
        lo = (sq - hi.astype(F32)).astype(BF16)
        avg = avg_ref[:w, :w]
        parts.append(_dot(hi, avg) + _dot(lo, avg))
    return jnp.concatenate(parts, axis=-1)


def _modulated(x2, sh, sc):
    return (_rms(x2) * (1.0 + sc) + sh).astype(BF16)


def _split_projection(z, ones_ref, gain):
    qk = z[:, :QK_W]
    qkn = qk * lax.rsqrt(_head_mean_square(qk, ones_ref) + RMS_EPS) * gain
    q = qkn[:, :ATTN_W]
    k = qkn[:, ATTN_W:QK_W]
    v = z[:, QK_W:QK_W + KV_W]
    xr = z[:, QK_W + KV_W:QK_W + KV_W + LRU_W]
    gr = z[:, QK_W + KV_W + LRU_W:]
    return q, k, v, xr, gr


def _shift_rows(x, k, t, fill):
    return jnp.where(t >= k, pltpu.roll(x, k, 0), fill)


def _conv(delayed, taps, bias):
    y = bias + delayed[3] * taps[0]
    y = y + delayed[2] * taps[1]
    y = y + delayed[1] * taps[2]
    return y + delayed[0] * taps[3]


def _lru_gates(xc, wgate_ref, bias):
    xb = xc.astype(BF16)
    cols = [_dot(xb[:, t * MXU_DIM:(t + 1) * MXU_DIM], wgate_ref[gate, t])
            for gate in range(2) for t in range(LRU_W // MXU_DIM)]
    return jnp.concatenate(cols, axis=-1) + bias


def _softplus_neg(lam):
    return jnp.maximum(-lam, 0.0) + jnp.log1p(jnp.exp(-jnp.abs(lam)))


def _lru_inputs(xc, gates, softplus):
    r = jax.nn.sigmoid(gates[:, :LRU_W])
    gi = jax.nn.sigmoid(gates[:, LRU_W:])
    log_a = (-RG_C * r) * softplus
    a = jnp.exp(log_a)
    w = -jnp.tanh(log_a) * (a * a + 1.0)
    u = jnp.where(w > 0.0, w * lax.rsqrt(w), 0.0) * (gi * xc)
    return a, u


def _log_scan(a, u, t, period):
    s = 1
    while s < period:
        a_prev = _shift_rows(a, s, t, 1.0)
        u_prev = _shift_rows(u, s, t, 0.0)
        u = u + a * u_prev
        a = a * a_prev
        s *= 2
    return a, u


def _slab_load(ref, start, n, stride=1):
    rows = pl.ds(start, n) if stride == 1 else pl.ds(start, n, stride=stride)
    return jnp.concatenate([ref[s, rows, :] for s in range(ref.shape[0])], axis=-1)


def _slab_store(ref, start, n, val, stride=1):
    rows = pl.ds(start, n) if stride == 1 else pl.ds(start, n, stride=stride)
    for s in range(ref.shape[0]):
        ref[s, rows, :] = val[:, s * LANES:(s + 1) * LANES]


def _blocked_scan(a_scr, u_scr, base, n, h_init):
    if n <= 2 * SUBLANES:
        t = lax.broadcasted_iota(jnp.int32, (n, 1), 0)
        a, u = _log_scan(_slab_load(a_scr, base, n), _slab_load(u_scr, base, n), t, n)
        return a * h_init + u
    m = n // SCAN_GROUP
    a_loc, h_loc = [], []
    for r in range(SCAN_GROUP):
        a_r = _slab_load(a_scr, base + r, m, SCAN_GROUP)
        u_r = _slab_load(u_scr, base + r, m, SCAN_GROUP)
        h_loc.append(u_r if r == 0 else a_r * h_loc[-1] + u_r)
        a_loc.append(a_r if r == 0 else a_r * a_loc[-1])
    _slab_store(a_scr, base + n, m, a_loc[-1])
    _slab_store(u_scr, base + n, m, h_loc[-1])
    ends = _blocked_scan(a_scr, u_scr, base + n, m, h_init)
    g = lax.broadcasted_iota(jnp.int32, (m, 1), 0)
    carry = jnp.where(g >= 1, pltpu.roll(ends, 1, 0), h_init)
    for r in range(SCAN_GROUP):
        _slab_store(u_scr, base + r, m, a_loc[r] * carry + h_loc[r], SCAN_GROUP)
    return _slab_load(u_scr, base, n)


def _scan_rows(n):
    return n if n <= 2 * SUBLANES else n + _scan_rows(n // SCAN_GROUP)


def _gelu_tanh(x):
    return 0.5 * x * (1.0 + jnp.tanh(np.sqrt(2.0 / np.pi) * (x + 0.044715 * (x * x * x))))


def _merged_heads(attn, lru, beta):
    return (jnp.concatenate([_rms(attn), _rms(lru)], axis=-1) * beta).astype(BF16)


def _softmax_with_sink(s, sink):
    m = jnp.maximum(jnp.max(s, axis=-1, keepdims=True), sink)
    p = jnp.exp(s - m)
    den = jnp.sum(p, axis=-1, keepdims=True) + jnp.exp(sink - m)
    return p, den


def _half_variants(x, low):
    xs = pltpu.roll(x, HALF, x.ndim - 1)
    zero = jnp.zeros_like(x)
    return (
        (jnp.where(low, x, zero).astype(BF16), jnp.where(low, zero, xs).astype(BF16)),
        (jnp.where(low, xs, zero).astype(BF16), jnp.where(low, zero, x).astype(BF16)),
    )


def _mix_prompt_kernel(sinks_ref, x_ref, sh_ref, sc_ref, g_ref, w_in_ref, ones_ref, gain_ref, bias_ref,
                       convw_ref, convb_ref, wgate_ref, bgate_ref, lam_ref, beta_ref, wout_ref,
                       y_ref, k_ref, v_ref, conv_ref, h_ref,
                       kprev, vprev, hcar, xr_scr, a_scr, u_scr):
    n = pl.program_id(1)

    @pl.when(n == 0)
    def _():
        kprev[...] = jnp.zeros_like(kprev)
        vprev[...] = jnp.zeros_like(vprev)
        hcar[...] = jnp.zeros_like(hcar)
        xr_scr[:, 0:SUBLANES, :] = jnp.zeros((xr_scr.shape[0], SUBLANES, LANES), F32)

    subs = x_ref.shape[1] // SUB_ROWS
    low = lax.broadcasted_iota(jnp.int32, (Q_BLOCK + SUB_ROWS, LANES), 1) < HALF
    second_head = lax.broadcasted_iota(jnp.int32, (2 * Q_BLOCK, 1), 0) >= Q_BLOCK
    carry = dict(k=kprev[...], v=vprev[...], h=hcar[...])
    st = [dict() for _ in range(subs)]

    def rep(r8):
        return jnp.broadcast_to(r8[None], (SUB_ROWS // SUBLANES,) + r8.shape).reshape(SUB_ROWS, r8.shape[-1])

    shift, scale1, gate = rep(sh_ref[0]), rep(1.0 + sc_ref[0]), rep(g_ref[0])
    gain, beta, gate_bias = rep(gain_ref[...]), rep(beta_ref[...]), rep(bgate_ref[...])
    taps, conv_bias = [rep(convw_ref[i]) for i in range(CONV_WIDTH)], rep(convb_ref[...])
    softplus = rep(_softplus_neg(lam_ref[...]))

    def rows(j):
        return pl.ds(j * SUB_ROWS, SUB_ROWS)

    def modulate(j):
        st[j]["h"] = (_rms(x_ref[0, rows(j), :]) * scale1 + shift).astype(BF16)
        st[j]["z"] = []

    def project(j, tiles):
        for c in tiles:
            st[j]["z"].append(_dot(st[j]["h"], w_in_ref[:, c * MXU_DIM:(c + 1) * MXU_DIM]))

    def split(j):
        z = jnp.concatenate(st[j].pop("z"), axis=-1)
        q, k, v, xr, gr = _split_projection(z, ones_ref, gain)
        st[j].update(q=q, k=k, v=v, xr=xr, gr=gr)

    pairs = [(kv, half) for kv in range(N_KV_HEADS) for half in range(2)]

    def scores(j):
        s = st[j]
        kvar = _half_variants(jnp.concatenate([carry["k"], s["k"]], axis=0), low)
        s["vvar"] = _half_variants(jnp.concatenate([carry["v"], s["v"]], axis=0), low)
        carry.update(k=s["k"][SUB_ROWS - Q_BLOCK:], v=s["v"][SUB_ROWS - Q_BLOCK:])
        s["s"] = []
        for qb in range(SUB_ROWS // Q_BLOCK):
            bias_row = jnp.minimum(n, 1) if (j == 0 and qb == 0) else 1
            q = s["q"][qb * Q_BLOCK:(qb + 1) * Q_BLOCK]
            for kv, half in pairs:
                stacked = jnp.concatenate([q[:, (2 * kv) * LANES:(2 * kv + 1) * LANES],
                                           q[:, (2 * kv + 1) * LANES:(2 * kv + 2) * LANES]], axis=0)
                keys = kvar[kv][half][qb * Q_BLOCK:(qb + 2) * Q_BLOCK]
                s["s"].append(_dot_nt(stacked.astype(BF16), keys) + bias_ref[bias_row, 2 * kv + half])

    def softmax(j):
        s = st[j]
        s["p"] = []
        for i, sc in enumerate(s.pop("s")):
            kv, half = pairs[i % len(pairs)]
            sink = jnp.where(second_head, sinks_ref[GROUP * kv + half + 2], sinks_ref[GROUP * kv + half])
            s["p"].append(_softmax_with_sink(sc, sink))

    def attend(j):
        s = st[j]
        vvar = s.pop("vvar")
        blocks = []
        for qb in range(SUB_ROWS // Q_BLOCK):
            out = {}
            for kv, half in pairs:
                p, den = s["p"][qb * len(pairs) + 2 * kv + half]
                out[kv, half] = _dot(p.astype(BF16), vvar[kv][half][qb * Q_BLOCK:(qb + 2) * Q_BLOCK]) / den
            tiles = []
            for kv in range(N_KV_HEADS):
                both = out[kv, 0] + out[kv, 1]
                tiles += [both[:Q_BLOCK], both[Q_BLOCK:]]
            blocks.append(jnp.concatenate(tiles, axis=-1))
        s.pop("p")
        s["attn"] = jnp.concatenate(blocks, axis=0)

    def conv(j):
        first = SUBLANES + j * SUB_ROWS
        _slab_store(xr_scr, first, SUB_ROWS, st[j]["xr"])
        st[j]["xc"] = _conv([_slab_load(xr_scr, first - kk, SUB_ROWS) for kk in range(CONV_WIDTH)],
                            taps, conv_bias)

    def gates(j):
        st[j]["gates"] = _lru_gates(st[j]["xc"], wgate_ref, gate_bias)

    def recur(j):
        s = st[j]
        a, u = _lru_inputs(s.pop("xc"), s.pop("gates"), softplus)
        _slab_store(a_scr.at[j], 0, SUB_ROWS, a)
        _slab_store(u_scr.at[j], 0, SUB_ROWS, u)
        hs = _blocked_scan(a_scr.at[j], u_scr.at[j], 0, SUB_ROWS, carry["h"])
        carry.update(h=hs[SUB_ROWS - 1:, :])
        s["lru"] = hs * _gelu_tanh(s.pop("gr"))
        if j == subs - 1:
            h_ref[0] = hs[SUB_ROWS - SUBLANES:, :]

    def merge(j):
        st[j]["merged"] = _merged_heads(st[j].pop("attn"), st[j].pop("lru"), beta)

    def output(j):
        merged = st[j].pop("merged")
        for c in range(D_MODEL // MXU_DIM):
            cols = slice(c * MXU_DIM, (c + 1) * MXU_DIM)
            y_ref[0, rows(j), cols] = x_ref[0, rows(j), cols] + gate[:, cols] * _dot(merged, wout_ref[c])

    first_tiles = range(0, 4)
    last_tiles = range(4, IN_COLS // MXU_DIM)
    modulate(0)
    project(0, first_tiles)
    project(0, last_tiles)
    for j in range(subs):
        nxt = j + 1 < subs
        split(j)
        if j > 0:
            merge(j - 1)
        scores(j)
        if j > 0:
            output(j - 1)
        conv(j)
        if nxt:
            modulate(j + 1)
        gates(j)
        if nxt:
            project(j + 1, first_tiles)
        softmax(j)
        attend(j)
        if nxt:
            project(j + 1, last_tiles)
        recur(j)
    merge(subs - 1)
    output(subs - 1)

    last = st[subs - 1]
    k_ref[0] = carry["k"]
    v_ref[0] = carry["v"]
    conv_ref[0] = last["xr"][SUB_ROWS - SUBLANES:, :]
    _slab_store(xr_scr, 0, SUBLANES, last["xr"][SUB_ROWS - SUBLANES:, :])
    kprev[...] = carry["k"]
    vprev[...] = carry["v"]
    hcar[...] = carry["h"]


def _prompt_bias():
    i = np.arange(Q_BLOCK)[:, None]
    j = np.arange(2 * Q_BLOCK)[None, :]
    dist = Q_BLOCK + i - j
    band = (dist >= 0) & (dist <= WINDOW)
    slopes = np.asarray([2.0 ** (-8.0 * (h + 1) / N_HEADS) for h in range(N_HEADS)], np.float32)
    alibi = -(slopes[:, None, None] * dist[None].astype(np.float32))
    general = np.where(band[None], alibi, np.float32(NEG_INF))
    first = np.where((band & (j >= Q_BLOCK))[None], alibi, np.float32(NEG_INF))
    per_head = np.stack([first, general]).astype(np.float32)
    pairs = [(GROUP * kv + half, GROUP * kv + half + 2) for kv in range(N_KV_HEADS) for half in range(2)]
    return np.stack([np.concatenate([per_head[:, a], per_head[:, b]], axis=1) for a, b in pairs], axis=1)


def _mix_prompt_call(x, mod, lw):
    b, t, d = x.shape
    nb = t // MIX_ROWS
    subs = MIX_ROWS // SUB_ROWS
    xspec = pl.BlockSpec((1, MIX_ROWS, d), lambda i, j: (i, j, 0))
    mspecs = _mod_specs(1, 3, lambda i, j, chunk: (i, 0, chunk), rows=SUBLANES)
    bias = jnp.asarray(_prompt_bias())
    last = lambda shape: pl.BlockSpec(shape, lambda i, j: (i, 0, 0))
    head = ["w_in", "ones", "gain"]
    tail = ["conv_w", "conv_b", "w_gate", "b_gate", "lam", "beta", "w_out"]
    outs = pl.pallas_call(
        _mix_prompt_kernel,
        out_shape=(
            jax.ShapeDtypeStruct(x.shape, F32),
            jax.ShapeDtypeStruct((b, WINDOW, KV_W), F32),
            jax.ShapeDtypeStruct((b, WINDOW, KV_W), F32),
            jax.ShapeDtypeStruct((b, SUBLANES, LRU_W), F32),
            jax.ShapeDtypeStruct((b, SUBLANES, LRU_W), F32),
        ),
        grid=(b, nb),
        in_specs=[
            pl.BlockSpec(memory_space=pltpu.SMEM),
            xspec, *mspecs,
            *[_weight_spec(lw, key) for key in head],
            _resident(bias.shape),
            *[_weight_spec(lw, key) for key in tail],
        ],
        out_specs=(xspec, last((1, WINDOW, KV_W)), last((1, WINDOW, KV_W)),
                   last((1, SUBLANES, LRU_W)), last((1, SUBLANES, LRU_W))),
        scratch_shapes=[
            pltpu.VMEM((Q_BLOCK, KV_W), F32), pltpu.VMEM((Q_BLOCK, KV_W), F32),
            pltpu.VMEM((1, LRU_W), F32),
            pltpu.VMEM((LRU_W // LANES, SUBLANES + MIX_ROWS, LANES), F32),
            pltpu.VMEM((subs, LRU_W // LANES, _scan_rows(SUB_ROWS), LANES), F32),
            pltpu.VMEM((subs, LRU_W // LANES, _scan_rows(SUB_ROWS), LANES), F32),
        ],
        compiler_params=pltpu.CompilerParams(
            dimension_semantics=("arbitrary", "arbitrary"), vmem_limit_bytes=VMEM_LIMIT),
        name="mix_prompt",
    )(lw["sinks"], x, mod, mod, mod, *[lw[key] for key in head], bias, *[lw[key] for key in tail])
    return outs


def _mix_sample_kernel(x_ref, sh_ref, sc_ref, g_ref, ck_ref, cv_ref, cs_ref, h0_ref,
                       w_in_ref, ones_ref, gain_ref, bias_ref, sink_ref,
                       convw_ref, convb_ref, wgate_ref, bgate_ref, lam_ref, beta_ref, wout_ref,
                       y_ref, k_ref, v_ref, conv_ref, h_ref):
    sb, tq, d = x_ref.shape
    rows = sb * tq

    def flat(a):
        return jnp.broadcast_to(a, (sb, tq, a.shape[-1])).reshape(rows, a.shape[-1])

    x2 = x_ref[...].reshape(rows, d)
    z = _dot(_modulated(x2, flat(sh_ref[...]), flat(sc_ref[...])), w_in_ref[...])
    q, k, v, xr, gr = _split_projection(z, ones_ref, gain_ref[0:1])
    k3 = k.reshape(sb, tq, KV_W)
    v3 = v.reshape(sb, tq, KV_W)
    ck = ck_ref[...]
    cv = cv_ref[...]
    k_ref[...] = jnp.concatenate([ck[:, tq:, :], k3], axis=1)
    v_ref[...] = jnp.concatenate([cv[:, tq:, :], v3], axis=1)

    pad = jnp.zeros((sb, WINDOW - tq, KV_W), F32)
    kall = jnp.concatenate([ck, k3, pad], axis=1)
    vall = jnp.concatenate([cv, v3, pad], axis=1)
    low2 = lax.broadcasted_iota(jnp.int32, (rows, LANES), 1) < HALF
    zero2 = jnp.zeros((rows, LANES), F32)
    pieces = []
    for head in range(N_HEADS):
        tile, parity, kv = head // 2, head % 2, head // GROUP
        qt = q[:, tile * LANES:(tile + 1) * LANES]
        src = qt if parity == kv else pltpu.roll(qt, HALF, 1)
        piece = jnp.where(low2, src, zero2) if kv == 0 else jnp.where(low2, zero2, src)
        pieces.append(piece.reshape(sb, tq, LANES))
    qrows = jnp.concatenate(pieces, axis=1).astype(BF16)
    s = jnp.einsum("snc,sjc->snj", qrows, kall.astype(BF16), preferred_element_type=F32)
    s = s + bias_ref[...]
    p, den = _softmax_with_sink(s, sink_ref[...])
    o = jnp.einsum("snj,sjc->snc", p.astype(BF16), vall.astype(BF16), preferred_element_type=F32) / den
    tiles = []
    for tile in range(ATTN_W // LANES):
        kv = (2 * tile) // GROUP
        oe = o[:, (2 * tile) * tq:(2 * tile + 1) * tq, :].reshape(rows, LANES)
        oo = o[:, (2 * tile + 1) * tq:(2 * tile + 2) * tq, :].reshape(rows, LANES)
        if kv == 0:
            tiles.append(jnp.where(low2, oe, pltpu.roll(oo, HALF, 1)))
        else:
            tiles.append(jnp.where(low2, pltpu.roll(oe, HALF, 1), oo))
    attn = jnp.concatenate(tiles, axis=-1)

    t = lax.broadcasted_iota(jnp.int32, (sb, tq, 1), 1).reshape(rows, 1)
    state = cs_ref[...].reshape(rows, LRU_W)
    prev = {3: state, 2: pltpu.roll(state, rows - 1, 0), 1: pltpu.roll(state, rows - 2, 0)}
    delayed = [xr] + [_shift_rows(xr, kk, t, prev[kk]) for kk in (1, 2, 3)]
    xc = _conv(delayed, [convw_ref[i, 0:1] for i in range(CONV_WIDTH)], convb_ref[0:1])
    conv_ref[...] = xr.reshape(sb, tq, LRU_W)
    gates = _lru_gates(xc, wgate_ref, bgate_ref[0:1])
    a, u = _log_scan(*_lru_inputs(xc, gates, _softplus_neg(lam_ref[0:1])), t, tq)
    hs = a * flat(h0_ref[...]) + u
    h_ref[...] = hs.reshape(sb, tq, LRU_W)

    merged = _merged_heads(attn, hs * _gelu_tanh(gr), beta_ref[0:1])
    y = jnp.concatenate([_dot(merged, wout_ref[c]) for c in range(d // MXU_DIM)], axis=-1)
    y_ref[...] = (x2 + flat(g_ref[...]) * y).reshape(sb, tq, d)


def _sample_bias(tq):
    i = np.arange(tq)[:, None]
    j = np.arange(2 * WINDOW)[None, :]
    dist = WINDOW + i - j
    ok = (dist >= 0) & (dist <= WINDOW) & (j < WINDOW + tq)
    slopes = np.asarray([2.0 ** (-8.0 * (h + 1) / N_HEADS) for h in range(N_HEADS)], np.float32)
    alibi = -(slopes[:, None, None] * dist[None].astype(np.float32))
    return np.where(ok[None], alibi, np.float32(NEG_INF)).reshape(N_HEADS * tq, 2 * WINDOW).astype(np.float32)


def _mix_sample_call(x, mod, k_windows, v_windows, conv_state, lru_state, lw):
    nseq, tq, d = x.shape
    sb = SAMPLE_SEQS
    seq_block = lambda shape: pl.BlockSpec(shape, lambda i: (i, 0, 0))
    first_block = lw["layer"] * (nseq // sb)
    window_block = pl.BlockSpec((sb, WINDOW, KV_W), lambda i: (first_block + i, 0, 0))
    xspec = seq_block((sb, tq, d))
    mspecs = _mod_specs(sb, 3, lambda i, chunk: (i, 0, chunk))
    bias = jnp.asarray(_sample_bias(tq))
    sink_col = jnp.repeat(lw["sinks"], tq).reshape(N_HEADS * tq, 1)
    conv_pad = jnp.pad(conv_state, ((0, 0), (0, tq - (CONV_WIDTH - 1)), (0, 0)))
    head = ["w_in", "ones", "gain"]
    tail = ["conv_w", "conv_b", "w_gate", "b_gate", "lam", "beta", "w_out"]
    return pl.pallas_call(
        _mix_sample_kernel,
        out_shape=(
            jax.ShapeDtypeStruct(x.shape, F32),
            jax.ShapeDtypeStruct(k_windows.shape, F32),
            jax.ShapeDtypeStruct(v_windows.shape, F32),
            jax.ShapeDtypeStruct((nseq, tq, LRU_W), F32),
            jax.ShapeDtypeStruct((nseq, tq, LRU_W), F32),
        ),
        grid=(nseq // sb,),
        in_specs=[
            xspec, *mspecs,
            window_block, window_block,
            seq_block((sb, tq, LRU_W)), seq_block((sb, 1, LRU_W)),
            *[_weight_spec(lw, key) for key in head],
            _resident(bias.shape), _resident(sink_col.shape),
            *[_weight_spec(lw, key) for key in tail],
        ],
        out_specs=(xspec, window_block, window_block,
                   seq_block((sb, tq, LRU_W)), seq_block((sb, tq, LRU_W))),
        input_output_aliases={4: 1, 5: 2},
        compiler_params=pltpu.CompilerParams(
            dimension_semantics=("arbitrary",), vmem_limit_bytes=VMEM_LIMIT),
        name="mix_sample",
    )(x, mod, mod, mod, k_windows, v_windows, conv_pad, lru_state.reshape(nseq, 1, LRU_W),
      *[lw[key] for key in head], bias, sink_col, *[lw[key] for key in tail])


def _block_diag(w):
    n, c, _ = w.shape
    eye = jnp.eye(n, dtype=w.dtype)
    return (w[:, :, None, :] * eye[:, None, :, None]).reshape(n * c, n * c)


def _diagonal_tiles(w):
    per_tile = MXU_DIM // w.shape[1]
    return jnp.stack([_block_diag(w[t:t + per_tile]) for t in range(0, w.shape[0], per_tile)])


def _rows8(v):
    return jnp.broadcast_to(v[..., None, :], v.shape[:-1] + (SUBLANES, v.shape[-1]))


def _layer_weights(l, w_in_bf16, q_gain, k_gain, sinks, conv_w, conv_b, w_rg, b_rg, w_ig, b_ig, lru_lambda,
                   beta_attn, beta_lru, w_out_tiles):
    head_of = np.arange(MXU_DIM) // HEAD_DIM
    ones = jnp.asarray((head_of[:, None] == head_of[None, :]) * (1.0 / HEAD_DIM), BF16)
    gain = jnp.concatenate([jnp.tile(q_gain[l] * (HEAD_DIM ** -0.5), N_HEADS), jnp.tile(k_gain[l], N_KV_HEADS)])
    return dict(
        layer=l,
        w_in=w_in_bf16,
        ones=ones,
        gain=_rows8(gain),
        sinks=sinks[l],
        conv_w=_rows8(conv_w[l]),
        conv_b=_rows8(conv_b[l]),
        w_gate=jnp.stack([_diagonal_tiles(w_rg[l]), _diagonal_tiles(w_ig[l])]).astype(BF16),
        b_gate=_rows8(jnp.concatenate([b_rg[l], b_ig[l]])),
        lam=_rows8(lru_lambda[l]),
        beta=_rows8(jnp.concatenate([beta_attn[l], beta_lru[l]])),
        w_out=w_out_tiles,
    )


def kernel(x_prompt, x_sample, cache_k, cache_v, state_conv, state_lru, c_prompt, c_sample, w_ada, b_ada, w1_gate, w1_up, w1_down, w_in, q_gain, k_gain, sinks, conv_w, conv_b, w_rg, b_rg, w_ig, b_ig, lru_lambda, beta_attn, beta_lru, w_out, w2_gate, w2_up, w2_down):
    nb = x_prompt.shape[0]
    ns, tq, _ = x_sample.shape
    mod = _mod_call(jnp.concatenate([c_prompt, c_sample], axis=0), w_ada, b_ada)
    mod_p = mod[:, :nb].reshape(DEPTH, nb, 1, N_MOD * D_MODEL)
    mod_s = mod[:, nb:].reshape(DEPTH, ns, 1, N_MOD * D_MODEL)

    w1g, w1u, w2g, w2u = _cast_call([w1_gate, w1_up, w2_gate, w2_up], CAST_ROWS)
    w1d, w2d = _cast_call([w1_down, w2_down], D_FF // (D_MODEL // CAST_ROWS))
    (w_in_bf16,) = _cast_call([w_in], CAST_ROWS)
    w_out_tiles = _cast_column_tiles_call(w_out, CAST_ROWS)
    w1, w2 = (w1g, w1u, w1d), (w2g, w2u, w2d)

    k_windows = cache_k.reshape(DEPTH * ns, WINDOW, KV_W)
    v_windows = cache_v.reshape(DEPTH * ns, WINDOW, KV_W)
    yp, ys = x_prompt, x_sample
    outs_p, outs_s = [], []
    for l in range(DEPTH):
        lw = _layer_weights(l, w_in_bf16, q_gain, k_gain, sinks, conv_w, conv_b, w_rg, b_rg, w_ig, b_ig,
                            lru_lambda, beta_attn, beta_lru, w_out_tiles)

        yp = _ffn_call(yp, mod_p[l], 0, w1, l, 1, FFN_ROWS * FFN_PARTS, "ffn1_prompt")
        ys = _ffn_call(ys, mod_s[l], 0, w1, l, FFN_ROWS // tq, tq, "ffn1_sample")

        yp, kp, vp, cp, hp = _mix_prompt_call(yp, _rows8(mod_p[l, :, 0]), lw)
        ys, k_windows, v_windows, cs, hs = _mix_sample_call(
            ys, mod_s[l], k_windows, v_windows, state_conv[l], state_lru[l], lw)
        outs_p.append((kp.reshape(nb, WINDOW, N_KV_HEADS, HEAD_DIM), vp.reshape(nb, WINDOW, N_KV_HEADS, HEAD_DIM),
                       cp[:, SUBLANES - (CONV_WIDTH - 1):], hp[:, SUBLANES - 1]))
        outs_s.append((cs[:, tq - (CONV_WIDTH - 1):], hs[:, tq - 1]))

        yp = _ffn_call(yp, mod_p[l], 6, w2, l, 1, FFN_ROWS * FFN_PARTS, "ffn2_prompt")
        ys = _ffn_call(ys, mod_s[l], 6, w2, l, FFN_ROWS // tq, tq, "ffn2_sample")

    stack = lambda outs, k: jnp.stack([o[k] for o in outs])
    return (yp, ys,
            stack(outs_p, 0), stack(outs_p, 1), stack(outs_p, 2), stack(outs_p, 3),
            k_windows.reshape(cache_k.shape), v_windows.reshape(cache_v.shape),
            stack(outs_s, 0), stack(outs_s, 1))
```

```python
import functools

import numpy as np
import jax
import jax.numpy as jnp
from jax import lax
from jax.experimental import pallas as pl
from jax.experimental.pallas import tpu as pltpu

D_MODEL = 1024
DEPTH = 2
HEAD_DIM = 64
N_HEADS = 8
N_KV_HEADS = 2
GROUP = N_HEADS // N_KV_HEADS
ATTN_W = N_HEADS * HEAD_DIM
KV_W = N_KV_HEADS * HEAD_DIM
LRU_W = 512
N_LRU_BLOCKS = 8
LRU_BLOCK = LRU_W // N_LRU_BLOCKS
CONV_WIDTH = 4
RG_C = 8.0
WINDOW = 128
Q_BLOCK = 128
D_FF = 2816
N_MOD = 9
FFN_RES = 0.5
IN_COLS = ATTN_W + 2 * KV_W + 2 * LRU_W
QK_W = ATTN_W + KV_W
RMS_EPS = 1e-6
NEG_INF = -1e30
LOG2E = float(np.log2(np.e))

LANES = 128
SUBLANES = 8
HALF = LANES // 2
MXU_DIM = 256
SCAN_GROUP = 4
VMEM_LIMIT = 56 * 1024 * 1024

FFN_ROWS = 512
FFN_PARTS = 2
CAST_ROWS = 256
MIX_ROWS = 1024
SUB_ROWS = 256
SAMPLE_SEQS = 16

BF16 = jnp.bfloat16
F32 = jnp.float32


def _dot(a, b):
    return jnp.dot(a, b, preferred_element_type=F32)


def _dot_nt(a, b):
    return lax.dot_general(a, b, (((1,), (1,)), ((), ())), preferred_element_type=F32)


def _rms(x):
    return x * lax.rsqrt(jnp.mean(x * x, axis=-1, keepdims=True) + RMS_EPS)


def _resident(shape):
    nd = len(shape)
    return pl.BlockSpec(shape, lambda *_: (0,) * nd, pipeline_mode=pl.Buffered(1))


def _resident_layer(shape, layer):
    nd = len(shape)
    return pl.BlockSpec((None,) + tuple(shape[1:]), lambda *_: (layer,) + (0,) * (nd - 1),
                        pipeline_mode=pl.Buffered(1))


def _cast_kernel(*refs):
    n = len(refs) // 2
    for src, dst in zip(refs[:n], refs[n:]):
        dst[...] = src[...].astype(BF16)


def _cast_call(ws, rows):
    depth, r, c = ws[0].shape
    spec = pl.BlockSpec((1, rows, c), lambda l, i: (l, i, 0))
    return pl.pallas_call(
        _cast_kernel,
        out_shape=[jax.ShapeDtypeStruct(w.shape, BF16) for w in ws],
        grid=(depth, r // rows),
        in_specs=[spec] * len(ws),
        out_specs=[spec] * len(ws),
        compiler_params=pltpu.CompilerParams(
            dimension_semantics=("arbitrary", "arbitrary"), vmem_limit_bytes=VMEM_LIMIT),
        name="cast_weights",
    )(*ws)


def _cast_tiles_kernel(src, dst):
    for c in range(dst.shape[1]):
        dst[0, c] = src[0, :, c * MXU_DIM:(c + 1) * MXU_DIM].astype(BF16)


def _cast_column_tiles_call(w, rows):
    depth, k, n = w.shape
    return pl.pallas_call(
        _cast_tiles_kernel,
        out_shape=jax.ShapeDtypeStruct((depth, n // MXU_DIM, k, MXU_DIM), BF16),
        grid=(depth, k // rows),
        in_specs=[pl.BlockSpec((1, rows, n), lambda l, i: (l, i, 0))],
        out_specs=pl.BlockSpec((1, n // MXU_DIM, rows, MXU_DIM), lambda l, i: (l, 0, i, 0)),
        compiler_params=pltpu.CompilerParams(
            dimension_semantics=("arbitrary", "arbitrary"), vmem_limit_bytes=VMEM_LIMIT),
        name="cast_column_tiles",
    )(w)


def _mod_kernel(c_ref, w_ref, b_ref, o_ref):
    c = c_ref[...]
    h = (c * jax.nn.sigmoid(c)).astype(BF16)
    o_ref[0] = _dot(h, w_ref[0].astype(BF16)) + b_ref[0]


def _mod_call(c_all, w_ada, b_ada):
    n = c_all.shape[0]
    return pl.pallas_call(
        _mod_kernel,
        out_shape=jax.ShapeDtypeStruct((DEPTH, n, N_MOD * D_MODEL), F32),
        grid=(DEPTH, N_MOD),
        in_specs=[
            pl.BlockSpec((n, D_MODEL), lambda l, j: (0, 0)),
            pl.BlockSpec((1, D_MODEL, D_MODEL), lambda l, j: (l, 0, j)),
            pl.BlockSpec((1, 1, D_MODEL), lambda l, j: (l, 0, j)),
        ],
        out_specs=pl.BlockSpec((1, n, D_MODEL), lambda l, j: (l, 0, j)),
        compiler_params=pltpu.CompilerParams(
            dimension_semantics=("arbitrary", "arbitrary"), vmem_limit_bytes=VMEM_LIMIT),
        name="adaln_mod",
    )(c_all, w_ada, b_ada.reshape(DEPTH, 1, N_MOD * D_MODEL))


def _ffn_kernel(x_ref, sh_ref, sc_ref, g_ref, wg_ref, wu_ref, wd_ref, o_ref, *, parts):
    s, t, d = x_ref.shape
    along_t = s == 1
    ps, pt = (s, t // parts) if along_t else (s // parts, t)

    def piece(ref, p):
        if along_t:
            return ref[:, pl.ds(p * pt, pt), :] if ref.shape[1] == t else ref[...]
        return ref[pl.ds(p * ps, ps)]

    def modulated(p):
        h = _rms(piece(x_ref, p)) * (1.0 + piece(sc_ref, p)) + piece(sh_ref, p)
        return h.reshape(ps * pt, d).astype(BF16)

    h2 = modulated(0)
    g_next = _dot(h2, wg_ref[...])
    u_next = _dot(h2, wu_ref[...])
    for p in range(parts):
        g, u = g_next, u_next
        if p + 1 < parts:
            h2 = modulated(p + 1)
            g_next = _dot(h2, wg_ref[...])
        a = (g * jax.nn.sigmoid(g) * u).astype(BF16)
        y = _dot(a, wd_ref[...]).reshape(ps, pt, d)
        if along_t:
            o_ref[:, pl.ds(p * pt, pt), :] = piece(x_ref, p) + (FFN_RES * piece(g_ref, p)) * y
        else:
            o_ref[pl.ds(p * ps, ps)] = piece(x_ref, p) + (FFN_RES * piece(g_ref, p)) * y
        if p + 1 < parts:
            u_next = _dot(h2, wu_ref[...])


STACKED_WEIGHTS = ("w_in", "w_out")


def _weight_spec(lw, key):
    if key in STACKED_WEIGHTS:
        return _resident_layer(lw[key].shape, lw["layer"])
    return _resident(lw[key].shape)


def _mod_specs(seqs, first_chunk, index, rows=1):
    return [pl.BlockSpec((seqs, rows, D_MODEL), functools.partial(index, chunk=first_chunk + k))
            for k in range(3)]


def _ffn_call(x, mod, first_chunk, weights, layer, seqs, rows, name):
    wg, wu, wd = weights
    nseq, t, d = x.shape
    parts = seqs * rows // FFN_ROWS
    grid = (nseq // seqs, t // rows)
    xspec = pl.BlockSpec((seqs, rows, d), lambda i, j: (i, j, 0))
    mspecs = _mod_specs(seqs, first_chunk, lambda i, j, chunk: (i, 0, chunk))
    return pl.pallas_call(
        functools.partial(_ffn_kernel, parts=parts),
        out_shape=jax.ShapeDtypeStruct(x.shape, F32),
        grid=grid,
        in_specs=[xspec, *mspecs, *[_resident_layer(w.shape, layer) for w in weights]],
        out_specs=xspec,
        compiler_params=pltpu.CompilerParams(
            dimension_semantics=("arbitrary", "arbitrary"), vmem_limit_bytes=VMEM_LIMIT),
        name=name,
    )(x, mod, mod, mod, wg, wu, wd)


def _head_mean_square(qk, avg_ref):
    parts = []
    for c0 in range(0, qk.shape[1], MXU_DIM):
        w = min(MXU_DIM, qk.shape[1] - c0)
        sq = qk[:, c0:c0 + w] * qk[:, c0:c0 + w]
        parts.append(_dot(sq.astype(BF16), avg_ref[:w, :w]))
    return jnp.concatenate(parts, axis=-1)


def _modulated(x2, sh, sc):
    return (_rms(x2) * (1.0 + sc) + sh).astype(BF16)


def _split_projection(z, ones_ref, gain):
    qk = z[:, :QK_W]
    qkn = qk * lax.rsqrt(_head_mean_square(qk, ones_ref) + RMS_EPS) * gain
    q = qkn[:, :ATTN_W]
    k = qkn[:, ATTN_W:QK_W]
    v = z[:, QK_W:QK_W + KV_W]
    xr = z[:, QK_W + KV_W:QK_W + KV_W + LRU_W]
    gr = z[:, QK_W + KV_W + LRU_W:]
    return q, k, v, xr, gr


def _shift_rows(x, k, t, fill):
    return jnp.where(t >= k, pltpu.roll(x, k, 0), fill)


def _conv(delayed, taps, bias):
    y = bias + delayed[3] * taps[0]
    y = y + delayed[2] * taps[1]
    y = y + delayed[1] * taps[2]
    return y + delayed[0] * taps[3]


def _lru_gates(xc, wgate_ref, bias):
    xb = xc.astype(BF16)
    cols = [_dot(xb[:, t * MXU_DIM:(t + 1) * MXU_DIM], wgate_ref[gate, t])
            for gate in range(2) for t in range(LRU_W // MXU_DIM)]
    return jnp.concatenate(cols, axis=-1) + bias


def _softplus_neg(lam):
    return jnp.maximum(-lam, 0.0) + jnp.log1p(jnp.exp(-jnp.abs(lam)))


def _lru_inputs(xc, gates, softplus):
    r = jax.nn.sigmoid(gates[:, :LRU_W])
    gi = jax.nn.sigmoid(gates[:, LRU_W:])
    log_a = (-RG_C * r) * softplus
    a = jnp.exp(log_a)
    w = -jnp.tanh(log_a) * (a * a + 1.0)
    u = jnp.where(w > 0.0, w * lax.rsqrt(w), 0.0) * (gi * xc)
    return a, u


def _log_scan(a, u, t, period):
    s = 1
    while s < period:
        a_prev = _shift_rows(a, s, t, 1.0)
        u_prev = _shift_rows(u, s, t, 0.0)
        u = u + a * u_prev
        a = a * a_prev
        s *= 2
    return a, u


def _slab_load(ref, start, n, stride=1):
    rows = pl.ds(start, n) if stride == 1 else pl.ds(start, n, stride=stride)
    return jnp.concatenate([ref[s, rows, :] for s in range(ref.shape[0])], axis=-1)


def _slab_store(ref, start, n, val, stride=1):
    rows = pl.ds(start, n) if stride == 1 else pl.ds(start, n, stride=stride)
    for s in range(ref.shape[0]):
        ref[s, rows, :] = val[:, s * LANES:(s + 1) * LANES]


def _blocked_scan(a_scr, u_scr, base, n, h_init):
    if n <= 2 * SUBLANES:
        t = lax.broadcasted_iota(jnp.int32, (n, 1), 0)
        a, u = _log_scan(_slab_load(a_scr, base, n), _slab_load(u_scr, base, n), t, n)
        return a * h_init + u
    m = n // SCAN_GROUP
    a_loc, h_loc = [], []
    for r in range(SCAN_GROUP):
        a_r = _slab_load(a_scr, base + r, m, SCAN_GROUP)
        u_r = _slab_load(u_scr, base + r, m, SCAN_GROUP)
        h_loc.append(u_r if r == 0 else a_r * h_loc[-1] + u_r)
        a_loc.append(a_r if r == 0 else a_r * a_loc[-1])
    _slab_store(a_scr, base + n, m, a_loc[-1])
    _slab_store(u_scr, base + n, m, h_loc[-1])
    ends = _blocked_scan(a_scr, u_scr, base + n, m, h_init)
    g = lax.broadcasted_iota(jnp.int32, (m, 1), 0)
    carry = jnp.where(g >= 1, pltpu.roll(ends, 1, 0), h_init)
    for r in range(SCAN_GROUP):
        _slab_store(u_scr, base + r, m, a_loc[r] * carry + h_loc[r], SCAN_GROUP)
    return _slab_load(u_scr, base, n)


def _scan_rows(n):
    return n if n <= 2 * SUBLANES else n + _scan_rows(n // SCAN_GROUP)


def _gelu_tanh(x):
    return 0.5 * x * (1.0 + jnp.tanh(np.sqrt(2.0 / np.pi) * (x + 0.044715 * (x * x * x))))


def _merged_heads(attn, lru, beta):
    return (jnp.concatenate([_rms(attn), _rms(lru)], axis=-1) * beta).astype(BF16)


def _softmax_with_sink(s, sink):
    m = jnp.maximum(jnp.max(s, axis=-1, keepdims=True), sink)
    p = jnp.exp2(s - m)
    den = jnp.sum(p, axis=-1, keepdims=True) + jnp.exp2(sink - m)
    return p, den


def _half_variants(x, low):
    xs = pltpu.roll(x, HALF, x.ndim - 1)
    zero = jnp.zeros_like(x)
    return (
        (jnp.where(low, x, zero).astype(BF16), jnp.where(low, zero, xs).astype(BF16)),
        (jnp.where(low, xs, zero).astype(BF16), jnp.where(low, zero, x).astype(BF16)),
    )


def _mix_prompt_kernel(sinks_ref, x_ref, sh_ref, sc_ref, g_ref, w_in_ref, ones_ref, gain_ref, bias_ref,
                       convw_ref, convb_ref, wgate_ref, bgate_ref, lam_ref, beta_ref, wout_ref,
                       y_ref, k_ref, v_ref, conv_ref, h_ref,
                       kprev, vprev, hcar, xr_scr, a_scr, u_scr):
    n = pl.program_id(1)

    @pl.when(n == 0)
    def _():
        kprev[...] = jnp.zeros_like(kprev)
        vprev[...] = jnp.zeros_like(vprev)
        hcar[...] = jnp.zeros_like(hcar)
        xr_scr[:, 0:SUBLANES, :] = jnp.zeros((xr_scr.shape[0], SUBLANES, LANES), F32)

    subs = x_ref.shape[1] // SUB_ROWS
    low = lax.broadcasted_iota(jnp.int32, (Q_BLOCK + SUB_ROWS, LANES), 1) < HALF
    second_head = lax.broadcasted_iota(jnp.int32, (2 * Q_BLOCK, 1), 0) >= Q_BLOCK
    carry = dict(k=kprev[...], v=vprev[...], h=hcar[...])
    st = [dict() for _ in range(subs)]

    def rep(r8):
        return jnp.broadcast_to(r8[None], (SUB_ROWS // SUBLANES,) + r8.shape).reshape(SUB_ROWS, r8.shape[-1])

    shift, scale1, gate = rep(sh_ref[0]), rep(1.0 + sc_ref[0]), rep(g_ref[0])
    gain, beta, gate_bias = rep(gain_ref[...]), rep(beta_ref[...]), rep(bgate_ref[...])
    taps, conv_bias = [rep(convw_ref[i]) for i in range(CONV_WIDTH)], rep(convb_ref[...])
    softplus = rep(_softplus_neg(lam_ref[...]))

    def rows(j):
        return pl.ds(j * SUB_ROWS, SUB_ROWS)

    def modulate(j):
        st[j]["h"] = (_rms(x_ref[0, rows(j), :]) * scale1 + shift).astype(BF16)
        st[j]["z"] = []

    def project(j, tiles):
        for c in tiles:
            st[j]["z"].append(_dot(st[j]["h"], w_in_ref[:, c * MXU_DIM:(c + 1) * MXU_DIM]))

    def split(j):
        z = jnp.concatenate(st[j].pop("z"), axis=-1)
        q, k, v, xr, gr = _split_projection(z, ones_ref, gain)
        st[j].update(q=q, k=k, v=v, xr=xr, gr=gr)

    pairs = [(kv, half) for kv in range(N_KV_HEADS) for half in range(2)]

    def scores(j):
        s = st[j]
        kvar = _half_variants(jnp.concatenate([carry["k"], s["k"]], axis=0), low)
        s["vvar"] = _half_variants(jnp.concatenate([carry["v"], s["v"]], axis=0), low)
        carry.update(k=s["k"][SUB_ROWS - Q_BLOCK:], v=s["v"][SUB_ROWS - Q_BLOCK:])
        s["s"] = []
        for qb in range(SUB_ROWS // Q_BLOCK):
            bias_row = jnp.minimum(n, 1) if (j == 0 and qb == 0) else 1
            q = s["q"][qb * Q_BLOCK:(qb + 1) * Q_BLOCK]
            for kv, half in pairs:
                stacked = jnp.concatenate([q[:, (2 * kv) * LANES:(2 * kv + 1) * LANES],
                                           q[:, (2 * kv + 1) * LANES:(2 * kv + 2) * LANES]], axis=0)
                keys = kvar[kv][half][qb * Q_BLOCK:(qb + 2) * Q_BLOCK]
                s["s"].append(_dot_nt(stacked.astype(BF16), keys) + bias_ref[bias_row, 2 * kv + half])

    def softmax(j):
        s = st[j]
        s["p"] = []
        for i, sc in enumerate(s.pop("s")):
            kv, half = pairs[i % len(pairs)]
            sink = jnp.where(second_head, sinks_ref[GROUP * kv + half + 2], sinks_ref[GROUP * kv + half])
            s["p"].append(_softmax_with_sink(sc, sink))

    def attend(j):
        s = st[j]
        vvar = s.pop("vvar")
        blocks = []
        for qb in range(SUB_ROWS // Q_BLOCK):
            out = {}
            for kv, half in pairs:
                p, den = s["p"][qb * len(pairs) + 2 * kv + half]
                out[kv, half] = _dot(p.astype(BF16), vvar[kv][half][qb * Q_BLOCK:(qb + 2) * Q_BLOCK]) / den
            tiles = []
            for kv in range(N_KV_HEADS):
                both = out[kv, 0] + out[kv, 1]
                tiles += [both[:Q_BLOCK], both[Q_BLOCK:]]
            blocks.append(jnp.concatenate(tiles, axis=-1))
        s.pop("p")
        s["attn"] = jnp.concatenate(blocks, axis=0)

    def conv(j):
        first = SUBLANES + j * SUB_ROWS
        _slab_store(xr_scr, first, SUB_ROWS, st[j]["xr"])
        st[j]["xc"] = _conv([_slab_load(xr_scr, first - kk, SUB_ROWS) for kk in range(CONV_WIDTH)],
                            taps, conv_bias)

    def gates(j):
        st[j]["gates"] = _lru_gates(st[j]["xc"], wgate_ref, gate_bias)

    def recur(j):
        s = st[j]
        a, u = _lru_inputs(s.pop("xc"), s.pop("gates"), softplus)
        _slab_store(a_scr.at[j], 0, SUB_ROWS, a)
        _slab_store(u_scr.at[j], 0, SUB_ROWS, u)
        hs = _blocked_scan(a_scr.at[j], u_scr.at[j], 0, SUB_ROWS, carry["h"])
        carry.update(h=hs[SUB_ROWS - 1:, :])
        s["lru"] = hs * _gelu_tanh(s.pop("gr"))
        if j == subs - 1:
            h_ref[0] = hs[SUB_ROWS - SUBLANES:, :]

    def merge(j):
        st[j]["merged"] = _merged_heads(st[j].pop("attn"), st[j].pop("lru"), beta)

    def output(j):
        merged = st[j].pop("merged")
        for c in range(D_MODEL // MXU_DIM):
            cols = slice(c * MXU_DIM, (c + 1) * MXU_DIM)
            y_ref[0, rows(j), cols] = x_ref[0, rows(j), cols] + gate[:, cols] * _dot(merged, wout_ref[c])

    first_tiles = range(0, 4)
    last_tiles = range(4, IN_COLS // MXU_DIM)
    modulate(0)
    project(0, first_tiles)
    project(0, last_tiles)
    for j in range(subs):
        nxt = j + 1 < subs
        split(j)
        if j > 0:
            merge(j - 1)
        scores(j)
        if j > 0:
            output(j - 1)
        conv(j)
        if nxt:
            modulate(j + 1)
        gates(j)
        if nxt:
            project(j + 1, first_tiles)
        softmax(j)
        attend(j)
        if nxt:
            project(j + 1, last_tiles)
        recur(j)
    merge(subs - 1)
    output(subs - 1)

    last = st[subs - 1]
    k_ref[0] = carry["k"]
    v_ref[0] = carry["v"]
    conv_ref[0] = last["xr"][SUB_ROWS - SUBLANES:, :]
    _slab_store(xr_scr, 0, SUBLANES, last["xr"][SUB_ROWS - SUBLANES:, :])
    kprev[...] = carry["k"]
    vprev[...] = carry["v"]
    hcar[...] = carry["h"]


def _prompt_bias():
    i = np.arange(Q_BLOCK)[:, None]
    j = np.arange(2 * Q_BLOCK)[None, :]
    dist = Q_BLOCK + i - j
    band = (dist >= 0) & (dist <= WINDOW)
    slopes = np.asarray([2.0 ** (-8.0 * (h + 1) / N_HEADS) for h in range(N_HEADS)], np.float32)
    alibi = -(slopes[:, None, None] * dist[None].astype(np.float64)) * LOG2E
    general = np.where(band[None], alibi, np.float32(NEG_INF))
    first = np.where((band & (j >= Q_BLOCK))[None], alibi, np.float32(NEG_INF))
    per_head = np.stack([first, general]).astype(np.float32)
    pairs = [(GROUP * kv + half, GROUP * kv + half + 2) for kv in range(N_KV_HEADS) for half in range(2)]
    return np.stack([np.concatenate([per_head[:, a], per_head[:, b]], axis=1) for a, b in pairs], axis=1)


def _mix_prompt_call(x, mod, lw):
    b, t, d = x.shape
    nb = t // MIX_ROWS
    subs = MIX_ROWS // SUB_ROWS
    xspec = pl.BlockSpec((1, MIX_ROWS, d), lambda i, j: (i, j, 0))
    mspecs = _mod_specs(1, 3, lambda i, j, chunk: (i, 0, chunk), rows=SUBLANES)
    bias = jnp.asarray(_prompt_bias())
    last = lambda shape: pl.BlockSpec(shape, lambda i, j: (i, 0, 0))
    head = ["w_in", "ones", "gain"]
    tail = ["conv_w", "conv_b", "w_gate", "b_gate", "lam", "beta", "w_out"]
    outs = pl.pallas_call(
        _mix_prompt_kernel,
        out_shape=(
            jax.ShapeDtypeStruct(x.shape, F32),
            jax.ShapeDtypeStruct((b, WINDOW, KV_W), F32),
            jax.ShapeDtypeStruct((b, WINDOW, KV_W), F32),
            jax.ShapeDtypeStruct((b, SUBLANES, LRU_W), F32),
            jax.ShapeDtypeStruct((b, SUBLANES, LRU_W), F32),
        ),
        grid=(b, nb),
        in_specs=[
            pl.BlockSpec(memory_space=pltpu.SMEM),
            xspec, *mspecs,
            *[_weight_spec(lw, key) for key in head],
            _resident(bias.shape),
            *[_weight_spec(lw, key) for key in tail],
        ],
        out_specs=(xspec, last((1, WINDOW, KV_W)), last((1, WINDOW, KV_W)),
                   last((1, SUBLANES, LRU_W)), last((1, SUBLANES, LRU_W))),
        scratch_shapes=[
            pltpu.VMEM((Q_BLOCK, KV_W), F32), pltpu.VMEM((Q_BLOCK, KV_W), F32),
            pltpu.VMEM((1, LRU_W), F32),
            pltpu.VMEM((LRU_W // LANES, SUBLANES + MIX_ROWS, LANES), F32),
            pltpu.VMEM((subs, LRU_W // LANES, _scan_rows(SUB_ROWS), LANES), F32),
            pltpu.VMEM((subs, LRU_W // LANES, _scan_rows(SUB_ROWS), LANES), F32),
        ],
        compiler_params=pltpu.CompilerParams(
            dimension_semantics=("arbitrary", "arbitrary"), vmem_limit_bytes=VMEM_LIMIT),
        name="mix_prompt",
    )(lw["sinks"], x, mod, mod, mod, *[lw[key] for key in head], bias, *[lw[key] for key in tail])
    return outs


def _mix_sample_kernel(x_ref, sh_ref, sc_ref, g_ref, ck_ref, cv_ref, cs_ref, h0_ref,
                       w_in_ref, ones_ref, gain_ref, bias_ref, sink_ref,
                       convw_ref, convb_ref, wgate_ref, bgate_ref, lam_ref, beta_ref, wout_ref,
                       y_ref, k_ref, v_ref, conv_ref, h_ref):
    sb, tq, d = x_ref.shape
    rows = sb * tq

    def flat(a):
        return jnp.broadcast_to(a, (sb, tq, a.shape[-1])).reshape(rows, a.shape[-1])

    x2 = x_ref[...].reshape(rows, d)
    z = _dot(_modulated(x2, flat(sh_ref[...]), flat(sc_ref[...])), w_in_ref[...])
    q, k, v, xr, gr = _split_projection(z, ones_ref, gain_ref[0:1])
    k3 = k.reshape(sb, tq, KV_W)
    v3 = v.reshape(sb, tq, KV_W)
    ck = ck_ref[...]
    cv = cv_ref[...]
    k_ref[...] = jnp.concatenate([ck[:, tq:, :], k3], axis=1)
    v_ref[...] = jnp.concatenate([cv[:, tq:, :], v3], axis=1)

    pad = jnp.zeros((sb, WINDOW - tq, KV_W), F32)
    kall = jnp.concatenate([ck, k3, pad], axis=1)
    vall = jnp.concatenate([cv, v3, pad], axis=1)
    low2 = lax.broadcasted_iota(jnp.int32, (rows, LANES), 1) < HALF
    zero2 = jnp.zeros((rows, LANES), F32)
    pieces = []
    for head in range(N_HEADS):
        tile, parity, kv = head // 2, head % 2, head // GROUP
        qt = q[:, tile * LANES:(tile + 1) * LANES]
        src = qt if parity == kv else pltpu.roll(qt, HALF, 1)
        piece = jnp.where(low2, src, zero2) if kv == 0 else jnp.where(low2, zero2, src)
        pieces.append(piece.reshape(sb, tq, LANES))
    qrows = jnp.concatenate(pieces, axis=1).astype(BF16)
    s = jnp.einsum("snc,sjc->snj", qrows, kall.astype(BF16), preferred_element_type=F32)
    s = s + bias_ref[...]
    p, den = _softmax_with_sink(s, sink_ref[...])
    o = jnp.einsum("snj,sjc->snc", p.astype(BF16), vall.astype(BF16), preferred_element_type=F32) / den
    tiles = []
    for tile in range(ATTN_W // LANES):
        kv = (2 * tile) // GROUP
        oe = o[:, (2 * tile) * tq:(2 * tile + 1) * tq, :].reshape(rows, LANES)
        oo = o[:, (2 * tile + 1) * tq:(2 * tile + 2) * tq, :].reshape(rows, LANES)
        if kv == 0:
            tiles.append(jnp.where(low2, oe, pltpu.roll(oo, HALF, 1)))
        else:
            tiles.append(jnp.where(low2, pltpu.roll(oe, HALF, 1), oo))
    attn = jnp.concatenate(tiles, axis=-1)

    t = lax.broadcasted_iota(jnp.int32, (sb, tq, 1), 1).reshape(rows, 1)
    state = cs_ref[...].reshape(rows, LRU_W)
    prev = {3: state, 2: pltpu.roll(state, rows - 1, 0), 1: pltpu.roll(state, rows - 2, 0)}
    delayed = [xr] + [_shift_rows(xr, kk, t, prev[kk]) for kk in (1, 2, 3)]
    xc = _conv(delayed, [convw_ref[i, 0:1] for i in range(CONV_WIDTH)], convb_ref[0:1])
    conv_ref[...] = xr.reshape(sb, tq, LRU_W)
    gates = _lru_gates(xc, wgate_ref, bgate_ref[0:1])
    a, u = _log_scan(*_lru_inputs(xc, gates, _softplus_neg(lam_ref[0:1])), t, tq)
    hs = a * flat(h0_ref[...]) + u
    h_ref[...] = hs.reshape(sb, tq, LRU_W)

    merged = _merged_heads(attn, hs * _gelu_tanh(gr), beta_ref[0:1])
    y = jnp.concatenate([_dot(merged, wout_ref[c]) for c in range(d // MXU_DIM)], axis=-1)
    y_ref[...] = (x2 + flat(g_ref[...]) * y).reshape(sb, tq, d)


def _sample_bias(tq):
    i = np.arange(tq)[:, None]
    j = np.arange(2 * WINDOW)[None, :]
    dist = WINDOW + i - j
    ok = (dist >= 0) & (dist <= WINDOW) & (j < WINDOW + tq)
    slopes = np.asarray([2.0 ** (-8.0 * (h + 1) / N_HEADS) for h in range(N_HEADS)], np.float32)
    alibi = -(slopes[:, None, None] * dist[None].astype(np.float64)) * LOG2E
    return np.where(ok[None], alibi, np.float32(NEG_INF)).reshape(N_HEADS * tq, 2 * WINDOW).astype(np.float32)


def _mix_sample_call(x, mod, k_windows, v_windows, conv_state, lru_state, lw):
    nseq, tq, d = x.shape
    sb = SAMPLE_SEQS
    seq_block = lambda shape: pl.BlockSpec(shape, lambda i: (i, 0, 0))
    first_block = lw["layer"] * (nseq // sb)
    window_block = pl.BlockSpec((sb, WINDOW, KV_W), lambda i: (first_block + i, 0, 0))
    xspec = seq_block((sb, tq, d))
    mspecs = _mod_specs(sb, 3, lambda i, chunk: (i, 0, chunk))
    bias = jnp.asarray(_sample_bias(tq))
    sink_col = jnp.repeat(lw["sinks"], tq).reshape(N_HEADS * tq, 1)
    conv_pad = jnp.pad(conv_state, ((0, 0), (0, tq - (CONV_WIDTH - 1)), (0, 0)))
    head = ["w_in", "ones", "gain"]
    tail = ["conv_w", "conv_b", "w_gate", "b_gate", "lam", "beta", "w_out"]
    return pl.pallas_call(
        _mix_sample_kernel,
        out_shape=(
            jax.ShapeDtypeStruct(x.shape, F32),
            jax.ShapeDtypeStruct(k_windows.shape, F32),
            jax.ShapeDtypeStruct(v_windows.shape, F32),
            jax.ShapeDtypeStruct((nseq, tq, LRU_W), F32),
            jax.ShapeDtypeStruct((nseq, tq, LRU_W), F32),
        ),
        grid=(nseq // sb,),
        in_specs=[
            xspec, *mspecs,
            window_block, window_block,
            seq_block((sb, tq, LRU_W)), seq_block((sb, 1, LRU_W)),
            *[_weight_spec(lw, key) for key in head],
            _resident(bias.shape), _resident(sink_col.shape),
            *[_weight_spec(lw, key) for key in tail],
        ],
        out_specs=(xspec, window_block, window_block,
                   seq_block((sb, tq, LRU_W)), seq_block((sb, tq, LRU_W))),
        input_output_aliases={4: 1, 5: 2},
        compiler_params=pltpu.CompilerParams(
            dimension_semantics=("arbitrary",), vmem_limit_bytes=VMEM_LIMIT),
        name="mix_sample",
    )(x, mod, mod, mod, k_windows, v_windows, conv_pad, lru_state.reshape(nseq, 1, LRU_W),
      *[lw[key] for key in head], bias, sink_col, *[lw[key] for key in tail])


def _block_diag(w):
    n, c, _ = w.shape
    eye = jnp.eye(n, dtype=w.dtype)
    return (w[:, :, None, :] * eye[:, None, :, None]).reshape(n * c, n * c)


def _diagonal_tiles(w):
    per_tile = MXU_DIM // w.shape[1]
    return jnp.stack([_block_diag(w[t:t + per_tile]) for t in range(0, w.shape[0], per_tile)])


def _rows8(v):
    return jnp.broadcast_to(v[..., None, :], v.shape[:-1] + (SUBLANES, v.shape[-1]))


def _layer_weights(l, w_in_bf16, q_gain, k_gain, sinks, conv_w, conv_b, w_rg, b_rg, w_ig, b_ig, lru_lambda,
                   beta_attn, beta_lru, w_out_tiles):
    head_of = np.arange(MXU_DIM) // HEAD_DIM
    ones = jnp.asarray((head_of[:, None] == head_of[None, :]) * (1.0 / HEAD_DIM), BF16)
    gain = jnp.concatenate([jnp.tile(q_gain[l] * (HEAD_DIM ** -0.5 * LOG2E), N_HEADS), jnp.tile(k_gain[l], N_KV_HEADS)])
    return dict(
        layer=l,
        w_in=w_in_bf16,
        ones=ones,
        gain=_rows8(gain),
        sinks=sinks[l] * LOG2E,
        conv_w=_rows8(conv_w[l]),
        conv_b=_rows8(conv_b[l]),
        w_gate=jnp.stack([_diagonal_tiles(w_rg[l]), _diagonal_tiles(w_ig[l])]).astype(BF16),
        b_gate=_rows8(jnp.concatenate([b_rg[l], b_ig[l]])),
        lam=_rows8(lru_lambda[l]),
        beta=_rows8(jnp.concatenate([beta_attn[l], beta_lru[l]])),
        w_out=w_out_tiles,
    )


def kernel(x_prompt, x_sample, cache_k, cache_v, state_conv, state_lru, c_prompt, c_sample, w_ada, b_ada, w1_gate, w1_up, w1_down, w_in, q_gain, k_gain, sinks, conv_w, conv_b, w_rg, b_rg, w_ig, b_ig, lru_lambda, beta_attn, beta_lru, w_out, w2_gate, w2_up, w2_down):
    nb = x_prompt.shape[0]
    ns, tq, _ = x_sample.shape
    mod = _mod_call(jnp.concatenate([c_prompt, c_sample], axis=0), w_ada, b_ada)
    mod_p = mod[:, :nb].reshape(DEPTH, nb, 1, N_MOD * D_MODEL)
    mod_s = mod[:, nb:].reshape(DEPTH, ns, 1, N_MOD * D_MODEL)

    w1g, w1u, w2g, w2u = _cast_call([w1_gate, w1_up, w2_gate, w2_up], CAST_ROWS)
    w1d, w2d = _cast_call([w1_down, w2_down], D_FF // (D_MODEL // CAST_ROWS))
    (w_in_bf16,) = _cast_call([w_in], CAST_ROWS)
    w_out_tiles = _cast_column_tiles_call(w_out, CAST_ROWS)
    w1, w2 = (w1g, w1u, w1d), (w2g, w2u, w2d)

    k_windows = cache_k.reshape(DEPTH * ns, WINDOW, KV_W)
    v_windows = cache_v.reshape(DEPTH * ns, WINDOW, KV_W)
    yp, ys = x_prompt, x_sample
    outs_p, outs_s = [], []
    for l in range(DEPTH):
        lw = _layer_weights(l, w_in_bf16, q_gain, k_gain, sinks, conv_w, conv_b, w_rg, b_rg, w_ig, b_ig,
                            lru_lambda, beta_attn, beta_lru, w_out_tiles)

        yp = _ffn_call(yp, mod_p[l], 0, w1, l, 1, FFN_ROWS * FFN_PARTS, "ffn1_prompt")
        ys = _ffn_call(ys, mod_s[l], 0, w1, l, FFN_ROWS // tq, tq, "ffn1_sample")

        yp, kp, vp, cp, hp = _mix_prompt_call(yp, _rows8(mod_p[l, :, 0]), lw)
        ys, k_windows, v_windows, cs, hs = _mix_sample_call(
            ys, mod_s[l], k_windows, v_windows, state_conv[l], state_lru[l], lw)
        outs_p.append((kp.reshape(nb, WINDOW, N_KV_HEADS, HEAD_DIM), vp.reshape(nb, WINDOW, N_KV_HEADS, HEAD_DIM),
                       cp[:, SUBLANES - (CONV_WIDTH - 1):], hp[:, SUBLANES - 1]))
        outs_s.append((cs[:, tq - (CONV_WIDTH - 1):], hs[:, tq - 1]))

        yp = _ffn_call(yp, mod_p[l], 6, w2, l, 1, FFN_ROWS * FFN_PARTS, "ffn2_prompt")
        ys = _ffn_call(ys, mod_s[l], 6, w2, l, FFN_ROWS // tq, tq, "ffn2_sample")

    stack = lambda outs, k: jnp.stack([o[k] for o in outs])
    return (yp, ys,
            stack(outs_p, 0), stack(outs_p, 1), stack(outs_p, 2), stack(outs_p, 3),
            k_windows.reshape(cache_k.shape), v_windows.reshape(cache_v.shape),
            stack(outs_s, 0), stack(outs_s, 1))
```

```python
import functools

import numpy as np
import jax
import jax.numpy as jnp
from jax import lax
from jax.experimental import pallas as pl
from jax.experimental.pallas import tpu as pltpu

D_MODEL = 1024
DEPTH = 2
HEAD_DIM = 64
N_HEADS = 8
N_KV_HEADS = 2
GROUP = N_HEADS // N_KV_HEADS
ATTN_W = N_HEADS * HEAD_DIM
KV_W = N_KV_HEADS * HEAD_DIM
LRU_W = 512
N_LRU_BLOCKS = 8
LRU_BLOCK = LRU_W // N_LRU_BLOCKS
CONV_WIDTH = 4
RG_C = 8.0
WINDOW = 128
Q_BLOCK = 128
D_FF = 2816
N_MOD = 9
FFN_RES = 0.5
IN_COLS = ATTN_W + 2 * KV_W + 2 * LRU_W
QK_W = ATTN_W + KV_W
RMS_EPS = 1e-6
NEG_INF = -1e30
LOG2E = float(np.log2(np.e))

LANES = 128
SUBLANES = 8
HALF = LANES // 2
MXU_DIM = 256
SCAN_GROUP = 4
VMEM_LIMIT = 56 * 1024 * 1024

FFN_ROWS = 512
FFN_PARTS = 2
CAST_ROWS = 256
MIX_ROWS = 1024
SUB_ROWS = 256
SAMPLE_SEQS = 32

BF16 = jnp.bfloat16
F32 = jnp.float32


def _dot(a, b):
    return jnp.dot(a, b, preferred_element_type=F32)


def _dot_nt(a, b):
    return lax.dot_general(a, b, (((1,), (1,)), ((), ())), preferred_element_type=F32)


def _rms(x):
    return x * lax.rsqrt(jnp.mean(x * x, axis=-1, keepdims=True) + RMS_EPS)


def _resident(shape):
    nd = len(shape)
    return pl.BlockSpec(shape, lambda *_: (0,) * nd, pipeline_mode=pl.Buffered(1))


def _resident_layer(shape, layer):
    nd = len(shape)
    return pl.BlockSpec((None,) + tuple(shape[1:]), lambda *_: (layer,) + (0,) * (nd - 1),
                        pipeline_mode=pl.Buffered(1))


def _cast_kernel(*refs):
    n = len(refs) // 2
    for src, dst in zip(refs[:n], refs[n:]):
        dst[...] = src[...].astype(BF16)


def _cast_call(ws, rows):
    depth, r, c = ws[0].shape
    spec = pl.BlockSpec((1, rows, c), lambda l, i: (l, i, 0))
    return pl.pallas_call(
        _cast_kernel,
        out_shape=[jax.ShapeDtypeStruct(w.shape, BF16) for w in ws],
        grid=(depth, r // rows),
        in_specs=[spec] * len(ws),
        out_specs=[spec] * len(ws),
        compiler_params=pltpu.CompilerParams(
            dimension_semantics=("arbitrary", "arbitrary"), vmem_limit_bytes=VMEM_LIMIT),
        name="cast_weights",
    )(*ws)


def _cast_tiles_kernel(src, dst):
    for c in range(dst.shape[1]):
        dst[0, c] = src[0, :, c * MXU_DIM:(c + 1) * MXU_DIM].astype(BF16)


def _cast_column_tiles_call(w, rows):
    depth, k, n = w.shape
    return pl.pallas_call(
        _cast_tiles_kernel,
        out_shape=jax.ShapeDtypeStruct((depth, n // MXU_DIM, k, MXU_DIM), BF16),
        grid=(depth, k // rows),
        in_specs=[pl.BlockSpec((1, rows, n), lambda l, i: (l, i, 0))],
        out_specs=pl.BlockSpec((1, n // MXU_DIM, rows, MXU_DIM), lambda l, i: (l, 0, i, 0)),
        compiler_params=pltpu.CompilerParams(
            dimension_semantics=("arbitrary", "arbitrary"), vmem_limit_bytes=VMEM_LIMIT),
        name="cast_column_tiles",
    )(w)


def _mod_kernel(c_ref, w_ref, b_ref, o_ref):
    c = c_ref[...]
    h = (c * jax.nn.sigmoid(c)).astype(BF16)
    o_ref[0] = _dot(h, w_ref[0].astype(BF16)) + b_ref[0]


def _mod_call(c_all, w_ada, b_ada):
    n = c_all.shape[0]
    return pl.pallas_call(
        _mod_kernel,
        out_shape=jax.ShapeDtypeStruct((DEPTH, n, N_MOD * D_MODEL), F32),
        grid=(DEPTH, N_MOD),
        in_specs=[
            pl.BlockSpec((n, D_MODEL), lambda l, j: (0, 0)),
            pl.BlockSpec((1, D_MODEL, D_MODEL), lambda l, j: (l, 0, j)),
            pl.BlockSpec((1, 1, D_MODEL), lambda l, j: (l, 0, j)),
        ],
        out_specs=pl.BlockSpec((1, n, D_MODEL), lambda l, j: (l, 0, j)),
        compiler_params=pltpu.CompilerParams(
            dimension_semantics=("arbitrary", "arbitrary"), vmem_limit_bytes=VMEM_LIMIT),
        name="adaln_mod",
    )(c_all, w_ada, b_ada.reshape(DEPTH, 1, N_MOD * D_MODEL))


def _ffn_kernel(x_ref, sh_ref, sc_ref, g_ref, wg_ref, wu_ref, wd_ref, o_ref, *, parts):
    s, t, d = x_ref.shape
    along_t = s == 1
    ps, pt = (s, t // parts) if along_t else (s // parts, t)

    def piece(ref, p):
        if along_t:
            return ref[:, pl.ds(p * pt, pt), :] if ref.shape[1] == t else ref[...]
        return ref[pl.ds(p * ps, ps)]

    def modulated(p):
        h = _rms(piece(x_ref, p)) * (1.0 + piece(sc_ref, p)) + piece(sh_ref, p)
        return h.reshape(ps * pt, d).astype(BF16)

    h2 = modulated(0)
    g_next = _dot(h2, wg_ref[...])
    u_next = _dot(h2, wu_ref[...])
    for p in range(parts):
        g, u = g_next, u_next
        if p + 1 < parts:
            h2 = modulated(p + 1)
            g_next = _dot(h2, wg_ref[...])
        a = (g * jax.nn.sigmoid(g) * u).astype(BF16)
        y = _dot(a, wd_ref[...]).reshape(ps, pt, d)
        if along_t:
            o_ref[:, pl.ds(p * pt, pt), :] = piece(x_ref, p) + (FFN_RES * piece(g_ref, p)) * y
        else:
            o_ref[pl.ds(p * ps, ps)] = piece(x_ref, p) + (FFN_RES * piece(g_ref, p)) * y
        if p + 1 < parts:
            u_next = _dot(h2, wu_ref[...])


def _weight_spec(lw, key):
    if key == "ones":
        return _resident(lw[key].shape)
    return _resident_layer(lw[key].shape, lw["layer"])


def _mod_specs(layer, seqs, first_chunk, index, rows=1):
    def spec(chunk):
        return pl.BlockSpec((None, seqs, rows, D_MODEL), lambda *g: (layer, index(*g), 0, chunk))
    return [spec(first_chunk + k) for k in range(3)]


def _ffn_call(x, mod, first_chunk, weights, layer, seqs, rows, name):
    wg, wu, wd = weights
    nseq, t, d = x.shape
    parts = seqs * rows // FFN_ROWS
    grid = (nseq // seqs, t // rows)
    xspec = pl.BlockSpec((seqs, rows, d), lambda i, j: (i, j, 0))
    mspecs = _mod_specs(layer, seqs, first_chunk, lambda i, j: i)
    return pl.pallas_call(
        functools.partial(_ffn_kernel, parts=parts),
        out_shape=jax.ShapeDtypeStruct(x.shape, F32),
        grid=grid,
        in_specs=[xspec, *mspecs, *[_resident_layer(w.shape, layer) for w in weights]],
        out_specs=xspec,
        compiler_params=pltpu.CompilerParams(
            dimension_semantics=("arbitrary", "arbitrary"), vmem_limit_bytes=VMEM_LIMIT),
        name=name,
    )(x, mod, mod, mod, wg, wu, wd)


def _head_mean_square(qk, avg_ref):
    parts = []
    for c0 in range(0, qk.shape[1], MXU_DIM):
        w = min(MXU_DIM, qk.shape[1] - c0)
        sq = qk[:, c0:c0 + w] * qk[:, c0:c0 + w]
        parts.append(_dot(sq.astype(BF16), avg_ref[:w, :w]))
    return jnp.concatenate(parts, axis=-1)


def _modulated(x2, sh, sc):
    return (_rms(x2) * (1.0 + sc) + sh).astype(BF16)


def _split_projection(z, ones_ref, gain):
    qk = z[:, :QK_W]
    qkn = qk * lax.rsqrt(_head_mean_square(qk, ones_ref) + RMS_EPS) * gain
    q = qkn[:, :ATTN_W]
    k = qkn[:, ATTN_W:QK_W]
    v = z[:, QK_W:QK_W + KV_W]
    xr = z[:, QK_W + KV_W:QK_W + KV_W + LRU_W]
    gr = z[:, QK_W + KV_W + LRU_W:]
    return q, k, v, xr, gr


def _shift_rows(x, k, t, fill):
    return jnp.where(t >= k, pltpu.roll(x, k, 0), fill)


def _conv(delayed, taps, bias):
    y = bias + delayed[3] * taps[0]
    y = y + delayed[2] * taps[1]
    y = y + delayed[1] * taps[2]
    return y + delayed[0] * taps[3]


def _lru_gates(xc, wgate_ref, bias):
    xb = xc.astype(BF16)
    cols = [_dot(xb[:, t * MXU_DIM:(t + 1) * MXU_DIM], wgate_ref[gate, t])
            for gate in range(2) for t in range(LRU_W // MXU_DIM)]
    return jnp.concatenate(cols, axis=-1) + bias


def _softplus_neg(lam):
    return jnp.maximum(-lam, 0.0) + jnp.log1p(jnp.exp(-jnp.abs(lam)))


def _lru_inputs(xc, gates, softplus):
    r = jax.nn.sigmoid(gates[:, :LRU_W])
    gi = jax.nn.sigmoid(gates[:, LRU_W:])
    log_a = (-RG_C * r) * softplus
    a = jnp.exp(log_a)
    w = -jnp.tanh(log_a) * (a * a + 1.0)
    u = jnp.where(w > 0.0, w * lax.rsqrt(w), 0.0) * (gi * xc)
    return a, u


def _log_scan(a, u, t, period):
    s = 1
    while s < period:
        a_prev = _shift_rows(a, s, t, 1.0)
        u_prev = _shift_rows(u, s, t, 0.0)
        u = u + a * u_prev
        a = a * a_prev
        s *= 2
    return a, u


def _slab_load(ref, start, n, stride=1):
    rows = pl.ds(start, n) if stride == 1 else pl.ds(start, n, stride=stride)
    return jnp.concatenate([ref[s, rows, :] for s in range(ref.shape[0])], axis=-1)


def _slab_store(ref, start, n, val, stride=1):
    rows = pl.ds(start, n) if stride == 1 else pl.ds(start, n, stride=stride)
    for s in range(ref.shape[0]):
        ref[s, rows, :] = val[:, s * LANES:(s + 1) * LANES]


def _blocked_scan(a_scr, u_scr, base, n, h_init):
    if n <= 2 * SUBLANES:
        t = lax.broadcasted_iota(jnp.int32, (n, 1), 0)
        a, u = _log_scan(_slab_load(a_scr, base, n), _slab_load(u_scr, base, n), t, n)
        return a * h_init + u
    m = n // SCAN_GROUP
    a_loc, h_loc = [], []
    for r in range(SCAN_GROUP):
        a_r = _slab_load(a_scr, base + r, m, SCAN_GROUP)
        u_r = _slab_load(u_scr, base + r, m, SCAN_GROUP)
        h_loc.append(u_r if r == 0 else a_r * h_loc[-1] + u_r)
        a_loc.append(a_r if r == 0 else a_r * a_loc[-1])
    _slab_store(a_scr, base + n, m, a_loc[-1])
    _slab_store(u_scr, base + n, m, h_loc[-1])
    ends = _blocked_scan(a_scr, u_scr, base + n, m, h_init)
    g = lax.broadcasted_iota(jnp.int32, (m, 1), 0)
    carry = jnp.where(g >= 1, pltpu.roll(ends, 1, 0), h_init)
    for r in range(SCAN_GROUP):
        _slab_store(u_scr, base + r, m, a_loc[r] * carry + h_loc[r], SCAN_GROUP)
    return _slab_load(u_scr, base, n)


def _scan_rows(n):
    return n if n <= 2 * SUBLANES else n + _scan_rows(n // SCAN_GROUP)


def _gelu_tanh(x):
    return 0.5 * x * (1.0 + jnp.tanh(np.sqrt(2.0 / np.pi) * (x + 0.044715 * (x * x * x))))


def _merged_heads(attn, lru, beta):
    return (jnp.concatenate([_rms(attn), _rms(lru)], axis=-1) * beta).astype(BF16)


def _softmax_with_sink(s, sink):
    m = jnp.maximum(jnp.max(s, axis=-1, keepdims=True), sink)
    p = jnp.exp2(s - m)
    den = jnp.sum(p, axis=-1, keepdims=True) + jnp.exp2(sink - m)
    return p, den


def _half_variants(x, low):
    xs = pltpu.roll(x, HALF, x.ndim - 1)
    zero = jnp.zeros_like(x)
    return (
        (jnp.where(low, x, zero).astype(BF16), jnp.where(low, zero, xs).astype(BF16)),
        (jnp.where(low, xs, zero).astype(BF16), jnp.where(low, zero, x).astype(BF16)),
    )


def _mix_prompt_kernel(sinks_ref, x_ref, sh_ref, sc_ref, g_ref, w_in_ref, ones_ref, gain_ref, bias_ref,
                       convw_ref, convb_ref, wgate_ref, bgate_ref, lam_ref, beta_ref, wout_ref,
                       y_ref, k_ref, v_ref, conv_ref, h_ref,
                       kprev, vprev, hcar, xr_scr, a_scr, u_scr, *, layer):
    n = pl.program_id(1)

    @pl.when(n == 0)
    def _():
        kprev[...] = jnp.zeros_like(kprev)
        vprev[...] = jnp.zeros_like(vprev)
        hcar[...] = jnp.zeros_like(hcar)
        xr_scr[:, 0:SUBLANES, :] = jnp.zeros((xr_scr.shape[0], SUBLANES, LANES), F32)

    subs = x_ref.shape[1] // SUB_ROWS
    low = lax.broadcasted_iota(jnp.int32, (Q_BLOCK + SUB_ROWS, LANES), 1) < HALF
    second_head = lax.broadcasted_iota(jnp.int32, (2 * Q_BLOCK, 1), 0) >= Q_BLOCK
    carry = dict(k=kprev[...], v=vprev[...], h=hcar[...])
    st = [dict() for _ in range(subs)]

    def rep(r8):
        return jnp.broadcast_to(r8[None], (SUB_ROWS // SUBLANES,) + r8.shape).reshape(SUB_ROWS, r8.shape[-1])

    shift, scale1, gate = rep(sh_ref[0]), rep(1.0 + sc_ref[0]), rep(g_ref[0])
    gain, beta, gate_bias = rep(gain_ref[...]), rep(beta_ref[...]), rep(bgate_ref[...])
    taps, conv_bias = [rep(convw_ref[i]) for i in range(CONV_WIDTH)], rep(convb_ref[...])
    softplus = rep(_softplus_neg(lam_ref[...]))

    def rows(j):
        return pl.ds(j * SUB_ROWS, SUB_ROWS)

    def modulate(j):
        st[j]["h"] = (_rms(x_ref[0, rows(j), :]) * scale1 + shift).astype(BF16)
        st[j]["z"] = []

    def project(j, tiles):
        for c in tiles:
            st[j]["z"].append(_dot(st[j]["h"], w_in_ref[:, c * MXU_DIM:(c + 1) * MXU_DIM]))

    def split(j):
        z = jnp.concatenate(st[j].pop("z"), axis=-1)
        q, k, v, xr, gr = _split_projection(z, ones_ref, gain)
        st[j].update(q=q, k=k, v=v, xr=xr, gr=gr)

    pairs = [(kv, half) for kv in range(N_KV_HEADS) for half in range(2)]

    def scores(j):
        s = st[j]
        kvar = _half_variants(jnp.concatenate([carry["k"], s["k"]], axis=0), low)
        s["vvar"] = _half_variants(jnp.concatenate([carry["v"], s["v"]], axis=0), low)
        carry.update(k=s["k"][SUB_ROWS - Q_BLOCK:], v=s["v"][SUB_ROWS - Q_BLOCK:])
        s["s"] = []
        for qb in range(SUB_ROWS // Q_BLOCK):
            bias_row = jnp.minimum(n, 1) if (j == 0 and qb == 0) else 1
            q = s["q"][qb * Q_BLOCK:(qb + 1) * Q_BLOCK]
            for kv, half in pairs:
                stacked = jnp.concatenate([q[:, (2 * kv) * LANES:(2 * kv + 1) * LANES],
                                           q[:, (2 * kv + 1) * LANES:(2 * kv + 2) * LANES]], axis=0)
                keys = kvar[kv][half][qb * Q_BLOCK:(qb + 2) * Q_BLOCK]
                s["s"].append(_dot_nt(stacked.astype(BF16), keys) + bias_ref[bias_row, 2 * kv + half])

    def softmax(j):
        s = st[j]
        s["p"] = []
        for i, sc in enumerate(s.pop("s")):
            kv, half = pairs[i % len(pairs)]
            first, second = GROUP * kv + half, GROUP * kv + half + 2
            sink = jnp.where(second_head, sinks_ref[layer, second], sinks_ref[layer, first])
            s["p"].append(_softmax_with_sink(sc, sink))

    def attend(j):
        s = st[j]
        vvar = s.pop("vvar")
        blocks = []
        for qb in range(SUB_ROWS // Q_BLOCK):
            out = {}
            for kv, half in pairs:
                p, den = s["p"][qb * len(pairs) + 2 * kv + half]
                out[kv, half] = _dot(p.astype(BF16), vvar[kv][half][qb * Q_BLOCK:(qb + 2) * Q_BLOCK]) / den
            tiles = []
            for kv in range(N_KV_HEADS):
                both = out[kv, 0] + out[kv, 1]
                tiles += [both[:Q_BLOCK], both[Q_BLOCK:]]
            blocks.append(jnp.concatenate(tiles, axis=-1))
        s.pop("p")
        s["attn"] = jnp.concatenate(blocks, axis=0)

    def conv(j):
        first = SUBLANES + j * SUB_ROWS
        _slab_store(xr_scr, first, SUB_ROWS, st[j]["xr"])
        st[j]["xc"] = _conv([_slab_load(xr_scr, first - kk, SUB_ROWS) for kk in range(CONV_WIDTH)],
                            taps, conv_bias)

    def gates(j):
        st[j]["gates"] = _lru_gates(st[j]["xc"], wgate_ref, gate_bias)

    def recur(j):
        s = st[j]
        a, u = _lru_inputs(s.pop("xc"), s.pop("gates"), softplus)
        _slab_store(a_scr.at[j], 0, SUB_ROWS, a)
        _slab_store(u_scr.at[j], 0, SUB_ROWS, u)
        hs = _blocked_scan(a_scr.at[j], u_scr.at[j], 0, SUB_ROWS, carry["h"])
        carry.update(h=hs[SUB_ROWS - 1:, :])
        s["lru"] = hs * _gelu_tanh(s.pop("gr"))
        if j == subs - 1:
            h_ref[0] = hs[SUB_ROWS - SUBLANES:, :]

    def merge(j):
        st[j]["merged"] = _merged_heads(st[j].pop("attn"), st[j].pop("lru"), beta)

    def output(j):
        merged = st[j].pop("merged")
        for c in range(D_MODEL // MXU_DIM):
            cols = slice(c * MXU_DIM, (c + 1) * MXU_DIM)
            y_ref[0, rows(j), cols] = x_ref[0, rows(j), cols] + gate[:, cols] * _dot(merged, wout_ref[c])

    first_tiles = range(0, 4)
    last_tiles = range(4, IN_COLS // MXU_DIM)
    modulate(0)
    project(0, first_tiles)
    project(0, last_tiles)
    for j in range(subs):
        nxt = j + 1 < subs
        split(j)
        if j > 0:
            merge(j - 1)
        scores(j)
        if j > 0:
            output(j - 1)
        conv(j)
        if nxt:
            modulate(j + 1)
        gates(j)
        if nxt:
            project(j + 1, first_tiles)
        softmax(j)
        attend(j)
        if nxt:
            project(j + 1, last_tiles)
        recur(j)
    merge(subs - 1)
    output(subs - 1)

    last = st[subs - 1]
    k_ref[0] = carry["k"]
    v_ref[0] = carry["v"]
    conv_ref[0] = last["xr"][SUB_ROWS - SUBLANES:, :]
    _slab_store(xr_scr, 0, SUBLANES, last["xr"][SUB_ROWS - SUBLANES:, :])
    kprev[...] = carry["k"]
    vprev[...] = carry["v"]
    hcar[...] = carry["h"]


def _prompt_bias():
    i = np.arange(Q_BLOCK)[:, None]
    j = np.arange(2 * Q_BLOCK)[None, :]
    dist = Q_BLOCK + i - j
    band = (dist >= 0) & (dist <= WINDOW)
    slopes = np.asarray([2.0 ** (-8.0 * (h + 1) / N_HEADS) for h in range(N_HEADS)], np.float32)
    alibi = -(slopes[:, None, None] * dist[None].astype(np.float64)) * LOG2E
    general = np.where(band[None], alibi, np.float32(NEG_INF))
    first = np.where((band & (j >= Q_BLOCK))[None], alibi, np.float32(NEG_INF))
    per_head = np.stack([first, general]).astype(np.float32)
    pairs = [(GROUP * kv + half, GROUP * kv + half + 2) for kv in range(N_KV_HEADS) for half in range(2)]
    return np.stack([np.concatenate([per_head[:, a], per_head[:, b]], axis=1) for a, b in pairs], axis=1)


def _mix_prompt_call(x, mod, lw):
    b, t, d = x.shape
    nb = t // MIX_ROWS
    subs = MIX_ROWS // SUB_ROWS
    xspec = pl.BlockSpec((1, MIX_ROWS, d), lambda i, j: (i, j, 0))
    mspecs = _mod_specs(lw["layer"], 1, 3, lambda i, j: i, rows=SUBLANES)
    bias = jnp.asarray(_prompt_bias())
    last = lambda shape: pl.BlockSpec(shape, lambda i, j: (i, 0, 0))
    head = ["w_in", "ones", "gain"]
    tail = ["conv_w", "conv_b", "w_gate", "b_gate", "lam", "beta", "w_out"]
    outs = pl.pallas_call(
        functools.partial(_mix_prompt_kernel, layer=lw["layer"]),
        out_shape=(
            jax.ShapeDtypeStruct(x.shape, F32),
            jax.ShapeDtypeStruct((b, WINDOW, KV_W), F32),
            jax.ShapeDtypeStruct((b, WINDOW, KV_W), F32),
            jax.ShapeDtypeStruct((b, SUBLANES, LRU_W), F32),
            jax.ShapeDtypeStruct((b, SUBLANES, LRU_W), F32),
        ),
        grid=(b, nb),
        in_specs=[
            pl.BlockSpec(memory_space=pltpu.SMEM),
            xspec, *mspecs,
            *[_weight_spec(lw, key) for key in head],
            _resident(bias.shape),
            *[_weight_spec(lw, key) for key in tail],
        ],
        out_specs=(xspec, last((1, WINDOW, KV_W)), last((1, WINDOW, KV_W)),
                   last((1, SUBLANES, LRU_W)), last((1, SUBLANES, LRU_W))),
        scratch_shapes=[
            pltpu.VMEM((Q_BLOCK, KV_W), F32), pltpu.VMEM((Q_BLOCK, KV_W), F32),
            pltpu.VMEM((1, LRU_W), F32),
            pltpu.VMEM((LRU_W // LANES, SUBLANES + MIX_ROWS, LANES), F32),
            pltpu.VMEM((subs, LRU_W // LANES, _scan_rows(SUB_ROWS), LANES), F32),
            pltpu.VMEM((subs, LRU_W // LANES, _scan_rows(SUB_ROWS), LANES), F32),
        ],
        compiler_params=pltpu.CompilerParams(
            dimension_semantics=("arbitrary", "arbitrary"), vmem_limit_bytes=VMEM_LIMIT),
        name="mix_prompt",
    )(lw["sinks"], x, mod, mod, mod, *[lw[key] for key in head], bias, *[lw[key] for key in tail])
    return outs


def _mix_sample_kernel(x_ref, sh_ref, sc_ref, g_ref, ck_ref, cv_ref, cs_ref, h0_ref,
                       w_in_ref, ones_ref, gain_ref, bias_ref, sink_ref,
                       convw_ref, convb_ref, wgate_ref, bgate_ref, lam_ref, beta_ref, wout_ref,
                       y_ref, k_ref, v_ref, conv_ref, h_ref):
    sb, tq, d = x_ref.shape
    rows = sb * tq

    def flat(a):
        return jnp.broadcast_to(a, (sb, tq, a.shape[-1])).reshape(rows, a.shape[-1])

    x2 = x_ref[...].reshape(rows, d)
    z = _dot(_modulated(x2, flat(sh_ref[...]), flat(sc_ref[...])), w_in_ref[...])
    q, k, v, xr, gr = _split_projection(z, ones_ref, gain_ref[0:1])
    k3 = k.reshape(sb, tq, KV_W)
    v3 = v.reshape(sb, tq, KV_W)
    ck = ck_ref[...]
    cv = cv_ref[...]
    k_ref[...] = jnp.concatenate([ck[:, tq:, :], k3], axis=1)
    v_ref[...] = jnp.concatenate([cv[:, tq:, :], v3], axis=1)

    pad = jnp.zeros((sb, WINDOW - tq, KV_W), F32)
    kall = jnp.concatenate([ck, k3, pad], axis=1)
    vall = jnp.concatenate([cv, v3, pad], axis=1)
    low2 = lax.broadcasted_iota(jnp.int32, (rows, LANES), 1) < HALF
    zero2 = jnp.zeros((rows, LANES), F32)
    pieces = []
    for head in range(N_HEADS):
        tile, parity, kv = head // 2, head % 2, head // GROUP
        qt = q[:, tile * LANES:(tile + 1) * LANES]
        src = qt if parity == kv else pltpu.roll(qt, HALF, 1)
        piece = jnp.where(low2, src, zero2) if kv == 0 else jnp.where(low2, zero2, src)
        pieces.append(piece.reshape(sb, tq, LANES))
    qrows = jnp.concatenate(pieces, axis=1).astype(BF16)
    s = jnp.einsum("snc,sjc->snj", qrows, kall.astype(BF16), preferred_element_type=F32)
    s = s + bias_ref[...]
    p, den = _softmax_with_sink(s, sink_ref[...])
    o = jnp.einsum("snj,sjc->snc", p.astype(BF16), vall.astype(BF16), preferred_element_type=F32) / den
    tiles = []
    for tile in range(ATTN_W // LANES):
        kv = (2 * tile) // GROUP
        oe = o[:, (2 * tile) * tq:(2 * tile + 1) * tq, :].reshape(rows, LANES)
        oo = o[:, (2 * tile + 1) * tq:(2 * tile + 2) * tq, :].reshape(rows, LANES)
        if kv == 0:
            tiles.append(jnp.where(low2, oe, pltpu.roll(oo, HALF, 1)))
        else:
            tiles.append(jnp.where(low2, pltpu.roll(oe, HALF, 1), oo))
    attn = jnp.concatenate(tiles, axis=-1)

    t = lax.broadcasted_iota(jnp.int32, (sb, tq, 1), 1).reshape(rows, 1)
    state = cs_ref[...].reshape(rows, LRU_W)
    prev = {3: state, 2: pltpu.roll(state, rows - 1, 0), 1: pltpu.roll(state, rows - 2, 0)}
    delayed = [xr] + [_shift_rows(xr, kk, t, prev[kk]) for kk in (1, 2, 3)]
    xc = _conv(delayed, [convw_ref[i, 0:1] for i in range(CONV_WIDTH)], convb_ref[0:1])
    conv_ref[...] = xr.reshape(sb, tq, LRU_W)
    gates = _lru_gates(xc, wgate_ref, bgate_ref[0:1])
    a, u = _log_scan(*_lru_inputs(xc, gates, _softplus_neg(lam_ref[0:1])), t, tq)
    hs = a * flat(h0_ref[...]) + u
    h_ref[...] = hs.reshape(sb, tq, LRU_W)

    merged = _merged_heads(attn, hs * _gelu_tanh(gr), beta_ref[0:1])
    y = jnp.concatenate([_dot(merged, wout_ref[c]) for c in range(d // MXU_DIM)], axis=-1)
    y_ref[...] = (x2 + flat(g_ref[...]) * y).reshape(sb, tq, d)


def _sample_bias(tq):
    i = np.arange(tq)[:, None]
    j = np.arange(2 * WINDOW)[None, :]
    dist = WINDOW + i - j
    ok = (dist >= 0) & (dist <= WINDOW) & (j < WINDOW + tq)
    slopes = np.asarray([2.0 ** (-8.0 * (h + 1) / N_HEADS) for h in range(N_HEADS)], np.float32)
    alibi = -(slopes[:, None, None] * dist[None].astype(np.float64)) * LOG2E
    return np.where(ok[None], alibi, np.float32(NEG_INF)).reshape(N_HEADS * tq, 2 * WINDOW).astype(np.float32)


def _mix_sample_call(x, mod, k_windows, v_windows, conv_states, lru_states, lw):
    nseq, tq, d = x.shape
    sb = SAMPLE_SEQS
    layer = lw["layer"]
    seq_block = lambda shape: pl.BlockSpec(shape, lambda i: (i, 0, 0))
    layer_seq_block = lambda shape: pl.BlockSpec((None,) + shape, lambda i: (layer, i, 0, 0))
    first_block = layer * (nseq // sb)
    window_block = pl.BlockSpec((sb, WINDOW, KV_W), lambda i: (first_block + i, 0, 0))
    xspec = seq_block((sb, tq, d))
    mspecs = _mod_specs(layer, sb, 3, lambda i: i)
    bias = jnp.asarray(_sample_bias(tq))
    head = ["w_in", "ones", "gain"]
    tail = ["sink_col", "conv_w", "conv_b", "w_gate", "b_gate", "lam", "beta", "w_out"]
    return pl.pallas_call(
        _mix_sample_kernel,
        out_shape=(
            jax.ShapeDtypeStruct(x.shape, F32),
            jax.ShapeDtypeStruct(k_windows.shape, F32),
            jax.ShapeDtypeStruct(v_windows.shape, F32),
            jax.ShapeDtypeStruct((nseq, tq, LRU_W), F32),
            jax.ShapeDtypeStruct((nseq, tq, LRU_W), F32),
        ),
        grid=(nseq // sb,),
        in_specs=[
            xspec, *mspecs,
            window_block, window_block,
            layer_seq_block((sb, tq, LRU_W)), layer_seq_block((sb, 1, LRU_W)),
            *[_weight_spec(lw, key) for key in head],
            _resident(bias.shape),
            *[_weight_spec(lw, key) for key in tail],
        ],
        out_specs=(xspec, window_block, window_block,
                   seq_block((sb, tq, LRU_W)), seq_block((sb, tq, LRU_W))),
        input_output_aliases={4: 1, 5: 2},
        compiler_params=pltpu.CompilerParams(
            dimension_semantics=("arbitrary",), vmem_limit_bytes=VMEM_LIMIT),
        name="mix_sample",
    )(x, mod, mod, mod, k_windows, v_windows, conv_states, lru_states,
      *[lw[key] for key in head], bias, *[lw[key] for key in tail])


def _diagonal_tiles(w):
    depth, n, c, _ = w.shape
    per_tile = MXU_DIM // c
    tiles = w.reshape(depth, n // per_tile, per_tile, c, c)
    eye = jnp.eye(per_tile, dtype=w.dtype)
    dense = tiles[:, :, :, :, None, :] * eye[None, None, :, None, :, None]
    return dense.reshape(depth, n // per_tile, MXU_DIM, MXU_DIM)


def _rows8(v):
    return jnp.broadcast_to(v[..., None, :], v.shape[:-1] + (SUBLANES, v.shape[-1]))


def _mixer_weights(tq, w_in_bf16, q_gain, k_gain, sinks, conv_w, conv_b, w_rg, b_rg, w_ig, b_ig, lru_lambda,
                   beta_attn, beta_lru, w_out_tiles):
    head_of = np.arange(MXU_DIM) // HEAD_DIM
    ones = jnp.asarray((head_of[:, None] == head_of[None, :]) * (1.0 / HEAD_DIM), BF16)
    gain = jnp.concatenate([jnp.tile(q_gain * (HEAD_DIM ** -0.5 * LOG2E), (1, N_HEADS)),
                            jnp.tile(k_gain, (1, N_KV_HEADS))], axis=1)
    return dict(
        w_in=w_in_bf16,
        ones=ones,
        gain=_rows8(gain),
        sinks=sinks * LOG2E,
        sink_col=jnp.repeat(sinks * LOG2E, tq, axis=1)[..., None],
        conv_w=_rows8(conv_w),
        conv_b=_rows8(conv_b),
        w_gate=jnp.stack([_diagonal_tiles(w_rg), _diagonal_tiles(w_ig)], axis=1).astype(BF16),
        b_gate=_rows8(jnp.concatenate([b_rg, b_ig], axis=1)),
        lam=_rows8(lru_lambda),
        beta=_rows8(jnp.concatenate([beta_attn, beta_lru], axis=1)),
        w_out=w_out_tiles,
    )


def kernel(x_prompt, x_sample, cache_k, cache_v, state_conv, state_lru, c_prompt, c_sample, w_ada, b_ada, w1_gate, w1_up, w1_down, w_in, q_gain, k_gain, sinks, conv_w, conv_b, w_rg, b_rg, w_ig, b_ig, lru_lambda, beta_attn, beta_lru, w_out, w2_gate, w2_up, w2_down):
    nb = x_prompt.shape[0]
    ns, tq, _ = x_sample.shape
    mod = _mod_call(jnp.concatenate([c_prompt, c_sample], axis=0), w_ada, b_ada)
    mod_p = mod[:, :nb].reshape(DEPTH, nb, 1, N_MOD * D_MODEL)
    mod_p8 = _rows8(mod[:, :nb])
    mod_s = mod[:, nb:].reshape(DEPTH, ns, 1, N_MOD * D_MODEL)

    w1g, w1u, w2g, w2u = _cast_call([w1_gate, w1_up, w2_gate, w2_up], CAST_ROWS)
    w1d, w2d = _cast_call([w1_down, w2_down], D_FF // (D_MODEL // CAST_ROWS))
    (w_in_bf16,) = _cast_call([w_in], CAST_ROWS)
    w_out_tiles = _cast_column_tiles_call(w_out, CAST_ROWS)
    w1, w2 = (w1g, w1u, w1d), (w2g, w2u, w2d)

    k_windows = cache_k.reshape(DEPTH * ns, WINDOW, KV_W)
    v_windows = cache_v.reshape(DEPTH * ns, WINDOW, KV_W)
    conv_states = jnp.pad(state_conv, ((0, 0), (0, 0), (0, tq - (CONV_WIDTH - 1)), (0, 0)))
    lru_states = state_lru.reshape(DEPTH, ns, 1, LRU_W)
    weights = _mixer_weights(tq, w_in_bf16, q_gain, k_gain, sinks, conv_w, conv_b, w_rg, b_rg, w_ig, b_ig,
                             lru_lambda, beta_attn, beta_lru, w_out_tiles)
    yp, ys = x_prompt, x_sample
    outs_p, outs_s = [], []
    for l in range(DEPTH):
        lw = dict(weights, layer=l)

        yp = _ffn_call(yp, mod_p, 0, w1, l, 1, FFN_ROWS * FFN_PARTS, "ffn1_prompt")
        ys = _ffn_call(ys, mod_s, 0, w1, l, FFN_ROWS // tq, tq, "ffn1_sample")

        yp, kp, vp, cp, hp = _mix_prompt_call(yp, mod_p8, lw)
        ys, k_windows, v_windows, cs, hs = _mix_sample_call(
            ys, mod_s, k_windows, v_windows, conv_states, lru_states, lw)
        outs_p.append((kp.reshape(nb, WINDOW, N_KV_HEADS, HEAD_DIM), vp.reshape(nb, WINDOW, N_KV_HEADS, HEAD_DIM),
                       cp[:, SUBLANES - (CONV_WIDTH - 1):], hp[:, SUBLANES - 1]))
        outs_s.append((cs[:, tq - (CONV_WIDTH - 1):], hs[:, tq - 1]))

        yp = _ffn_call(yp, mod_p, 6, w2, l, 1, FFN_ROWS * FFN_PARTS, "ffn2_prompt")
        ys = _ffn_call(ys, mod_s, 6, w2, l, FFN_ROWS // tq, tq, "ffn2_sample")

    stack = lambda outs, k: jnp.stack([o[k] for o in outs])
    return (yp, ys,
            stack(outs_p, 0), stack(outs_p, 1), stack(outs_p, 2), stack(outs_p, 3),
            k_windows.reshape(cache_k.shape), v_windows.reshape(cache_v.shape),
            stack(outs_s, 0), stack(outs_s, 1))
```

```python
import functools

import numpy as np
import jax
import jax.numpy as jnp
from jax import lax
from jax.experimental import pallas as pl
from jax.experimental.pallas import tpu as pltpu

D_MODEL = 1024
DEPTH = 2
HEAD_DIM = 64
N_HEADS = 8
N_KV_HEADS = 2
GROUP = N_HEADS // N_KV_HEADS
ATTN_W = N_HEADS * HEAD_DIM
KV_W = N_KV_HEADS * HEAD_DIM
LRU_W = 512
N_LRU_BLOCKS = 8
LRU_BLOCK = LRU_W // N_LRU_BLOCKS
CONV_WIDTH = 4
RG_C = 8.0
WINDOW = 128
Q_BLOCK = 128
D_FF = 2816
N_MOD = 9
FFN_RES = 0.5
IN_COLS = ATTN_W + 2 * KV_W + 2 * LRU_W
QK_W = ATTN_W + KV_W
RMS_EPS = 1e-6
NEG_INF = -1e30
LOG2E = float(np.log2(np.e))

LANES = 128
SUBLANES = 8
HALF = LANES // 2
MXU_DIM = 256
SCAN_GROUP = 4
VMEM_LIMIT = 56 * 1024 * 1024

FFN_ROWS = 512
FFN_PARTS = 2
CAST_ROWS = 256
MIX_ROWS = 2048
SUB_ROWS = 256
SAMPLE_SEQS = 32

BF16 = jnp.bfloat16
F32 = jnp.float32


def _dot(a, b):
    return jnp.dot(a, b, preferred_element_type=F32)


def _dot_nt(a, b):
    return lax.dot_general(a, b, (((1,), (1,)), ((), ())), preferred_element_type=F32)


def _rms(x):
    return x * lax.rsqrt(jnp.mean(x * x, axis=-1, keepdims=True) + RMS_EPS)


def _resident(shape):
    nd = len(shape)
    return pl.BlockSpec(shape, lambda *_: (0,) * nd, pipeline_mode=pl.Buffered(1))


def _resident_layer(shape, layer):
    nd = len(shape)
    return pl.BlockSpec((None,) + tuple(shape[1:]), lambda *_: (layer,) + (0,) * (nd - 1),
                        pipeline_mode=pl.Buffered(1))


def _cast_kernel(*refs):
    n = len(refs) // 2
    for src, dst in zip(refs[:n], refs[n:]):
        dst[...] = src[...].astype(BF16)


def _cast_call(ws, rows):
    depth, r, c = ws[0].shape
    spec = pl.BlockSpec((1, rows, c), lambda l, i: (l, i, 0))
    return pl.pallas_call(
        _cast_kernel,
        out_shape=[jax.ShapeDtypeStruct(w.shape, BF16) for w in ws],
        grid=(depth, r // rows),
        in_specs=[spec] * len(ws),
        out_specs=[spec] * len(ws),
        compiler_params=pltpu.CompilerParams(
            dimension_semantics=("arbitrary", "arbitrary"), vmem_limit_bytes=VMEM_LIMIT),
        name="cast_weights",
    )(*ws)


def _cast_tiles_kernel(src, dst):
    for c in range(dst.shape[1]):
        dst[0, c] = src[0, :, c * MXU_DIM:(c + 1) * MXU_DIM].astype(BF16)


def _cast_column_tiles_call(w, rows):
    depth, k, n = w.shape
    return pl.pallas_call(
        _cast_tiles_kernel,
        out_shape=jax.ShapeDtypeStruct((depth, n // MXU_DIM, k, MXU_DIM), BF16),
        grid=(depth, k // rows),
        in_specs=[pl.BlockSpec((1, rows, n), lambda l, i: (l, i, 0))],
        out_specs=pl.BlockSpec((1, n // MXU_DIM, rows, MXU_DIM), lambda l, i: (l, 0, i, 0)),
        compiler_params=pltpu.CompilerParams(
            dimension_semantics=("arbitrary", "arbitrary"), vmem_limit_bytes=VMEM_LIMIT),
        name="cast_column_tiles",
    )(w)


def _mod_kernel(c_ref, w_ref, b_ref, o_ref):
    c = c_ref[...]
    h = (c * jax.nn.sigmoid(c)).astype(BF16)
    o_ref[0] = _dot(h, w_ref[0].astype(BF16)) + b_ref[0]


def _mod_call(c_all, w_ada, b_ada):
    n = c_all.shape[0]
    return pl.pallas_call(
        _mod_kernel,
        out_shape=jax.ShapeDtypeStruct((DEPTH, n, N_MOD * D_MODEL), F32),
        grid=(DEPTH, N_MOD),
        in_specs=[
            pl.BlockSpec((n, D_MODEL), lambda l, j: (0, 0)),
            pl.BlockSpec((1, D_MODEL, D_MODEL), lambda l, j: (l, 0, j)),
            pl.BlockSpec((1, 1, D_MODEL), lambda l, j: (l, 0, j)),
        ],
        out_specs=pl.BlockSpec((1, n, D_MODEL), lambda l, j: (l, 0, j)),
        compiler_params=pltpu.CompilerParams(
            dimension_semantics=("arbitrary", "arbitrary"), vmem_limit_bytes=VMEM_LIMIT),
        name="adaln_mod",
    )(c_all, w_ada, b_ada.reshape(DEPTH, 1, N_MOD * D_MODEL))


def _ffn_kernel(x_ref, sh_ref, sc_ref, g_ref, wg_ref, wu_ref, wd_ref, o_ref, *, parts):
    s, t, d = x_ref.shape
    along_t = s == 1
    ps, pt = (s, t // parts) if along_t else (s // parts, t)

    def piece(ref, p):
        if along_t:
            return ref[:, pl.ds(p * pt, pt), :] if ref.shape[1] == t else ref[...]
        return ref[pl.ds(p * ps, ps)]

    def modulated(p):
        h = _rms(piece(x_ref, p)) * (1.0 + piece(sc_ref, p)) + piece(sh_ref, p)
        return h.reshape(ps * pt, d).astype(BF16)

    h2 = modulated(0)
    g_next = _dot(h2, wg_ref[...])
    u_next = _dot(h2, wu_ref[...])
    for p in range(parts):
        g, u = g_next, u_next
        if p + 1 < parts:
            h2 = modulated(p + 1)
            g_next = _dot(h2, wg_ref[...])
        a = (g * jax.nn.sigmoid(g) * u).astype(BF16)
        y = _dot(a, wd_ref[...]).reshape(ps, pt, d)
        if along_t:
            o_ref[:, pl.ds(p * pt, pt), :] = piece(x_ref, p) + (FFN_RES * piece(g_ref, p)) * y
        else:
            o_ref[pl.ds(p * ps, ps)] = piece(x_ref, p) + (FFN_RES * piece(g_ref, p)) * y
        if p + 1 < parts:
            u_next = _dot(h2, wu_ref[...])


def _weight_spec(lw, key):
    if key == "ones":
        return _resident(lw[key].shape)
    return _resident_layer(lw[key].shape, lw["layer"])


def _mod_specs(layer, seqs, first_chunk, index, rows=1):
    def spec(chunk):
        return pl.BlockSpec((None, seqs, rows, D_MODEL), lambda *g: (layer, index(*g), 0, chunk))
    return [spec(first_chunk + k) for k in range(3)]


def _ffn_call(x, mod, first_chunk, weights, layer, seqs, rows, name):
    wg, wu, wd = weights
    nseq, t, d = x.shape
    parts = seqs * rows // FFN_ROWS
    grid = (nseq // seqs, t // rows)
    xspec = pl.BlockSpec((seqs, rows, d), lambda i, j: (i, j, 0))
    mspecs = _mod_specs(layer, seqs, first_chunk, lambda i, j: i)
    return pl.pallas_call(
        functools.partial(_ffn_kernel, parts=parts),
        out_shape=jax.ShapeDtypeStruct(x.shape, F32),
        grid=grid,
        in_specs=[xspec, *mspecs, *[_resident_layer(w.shape, layer) for w in weights]],
        out_specs=xspec,
        compiler_params=pltpu.CompilerParams(
            dimension_semantics=("arbitrary", "arbitrary"), vmem_limit_bytes=VMEM_LIMIT),
        name=name,
    )(x, mod, mod, mod, wg, wu, wd)


def _head_mean_square(qk, avg_ref):
    parts = []
    for c0 in range(0, qk.shape[1], MXU_DIM):
        w = min(MXU_DIM, qk.shape[1] - c0)
        sq = qk[:, c0:c0 + w] * qk[:, c0:c0 + w]
        parts.append(_dot(sq.astype(BF16), avg_ref[:w, :w]))
    return jnp.concatenate(parts, axis=-1)


def _modulated(x2, sh, sc):
    return (_rms(x2) * (1.0 + sc) + sh).astype(BF16)


def _split_projection(z, ones_ref, gain):
    qk = z[:, :QK_W]
    qkn = qk * lax.rsqrt(_head_mean_square(qk, ones_ref) + RMS_EPS) * gain
    q = qkn[:, :ATTN_W]
    k = qkn[:, ATTN_W:QK_W]
    v = z[:, QK_W:QK_W + KV_W]
    xr = z[:, QK_W + KV_W:QK_W + KV_W + LRU_W]
    gr = z[:, QK_W + KV_W + LRU_W:]
    return q, k, v, xr, gr


def _shift_rows(x, k, t, fill):
    return jnp.where(t >= k, pltpu.roll(x, k, 0), fill)


def _conv(delayed, taps, bias):
    y = bias + delayed[3] * taps[0]
    y = y + delayed[2] * taps[1]
    y = y + delayed[1] * taps[2]
    return y + delayed[0] * taps[3]


def _lru_gates(xc, wgate_ref, bias):
    xb = xc.astype(BF16)
    cols = [_dot(xb[:, t * MXU_DIM:(t + 1) * MXU_DIM], wgate_ref[gate, t])
            for gate in range(2) for t in range(LRU_W // MXU_DIM)]
    return jnp.concatenate(cols, axis=-1) + bias


def _softplus_neg(lam):
    return jnp.maximum(-lam, 0.0) + jnp.log1p(jnp.exp(-jnp.abs(lam)))


def _lru_inputs(xc, gates, softplus):
    r = jax.nn.sigmoid(gates[:, :LRU_W])
    gi = jax.nn.sigmoid(gates[:, LRU_W:])
    log_a = (-RG_C * r) * softplus
    a = jnp.exp(log_a)
    w = -jnp.tanh(log_a) * (a * a + 1.0)
    u = jnp.where(w > 0.0, w * lax.rsqrt(w), 0.0) * (gi * xc)
    return a, u


def _log_scan(a, u, t, period):
    s = 1
    while s < period:
        a_prev = _shift_rows(a, s, t, 1.0)
        u_prev = _shift_rows(u, s, t, 0.0)
        u = u + a * u_prev
        a = a * a_prev
        s *= 2
    return a, u


def _slab_load(ref, start, n, stride=1):
    rows = pl.ds(start, n) if stride == 1 else pl.ds(start, n, stride=stride)
    return jnp.concatenate([ref[s, rows, :] for s in range(ref.shape[0])], axis=-1)


def _slab_store(ref, start, n, val, stride=1):
    rows = pl.ds(start, n) if stride == 1 else pl.ds(start, n, stride=stride)
    for s in range(ref.shape[0]):
        ref[s, rows, :] = val[:, s * LANES:(s + 1) * LANES]


def _blocked_scan(a_scr, u_scr, base, n, h_init):
    if n <= 2 * SUBLANES:
        t = lax.broadcasted_iota(jnp.int32, (n, 1), 0)
        a, u = _log_scan(_slab_load(a_scr, base, n), _slab_load(u_scr, base, n), t, n)
        return a * h_init + u
    m = n // SCAN_GROUP
    a_loc, h_loc = [], []
    for r in range(SCAN_GROUP):
        a_r = _slab_load(a_scr, base + r, m, SCAN_GROUP)
        u_r = _slab_load(u_scr, base + r, m, SCAN_GROUP)
        h_loc.append(u_r if r == 0 else a_r * h_loc[-1] + u_r)
        a_loc.append(a_r if r == 0 else a_r * a_loc[-1])
    _slab_store(a_scr, base + n, m, a_loc[-1])
    _slab_store(u_scr, base + n, m, h_loc[-1])
    ends = _blocked_scan(a_scr, u_scr, base + n, m, h_init)
    g = lax.broadcasted_iota(jnp.int32, (m, 1), 0)
    carry = jnp.where(g >= 1, pltpu.roll(ends, 1, 0), h_init)
    for r in range(SCAN_GROUP):
        _slab_store(u_scr, base + r, m, a_loc[r] * carry + h_loc[r], SCAN_GROUP)
    return _slab_load(u_scr, base, n)


def _scan_rows(n):
    return n if n <= 2 * SUBLANES else n + _scan_rows(n // SCAN_GROUP)


def _gelu_tanh(x):
    return 0.5 * x * (1.0 + jnp.tanh(np.sqrt(2.0 / np.pi) * (x + 0.044715 * (x * x * x))))


def _merged_heads(attn, lru, beta):
    return (jnp.concatenate([_rms(attn), _rms(lru)], axis=-1) * beta).astype(BF16)


def _softmax_with_sink(s, sink):
    m = jnp.maximum(jnp.max(s, axis=-1, keepdims=True), sink)
    p = jnp.exp2(s - m)
    den = jnp.sum(p, axis=-1, keepdims=True) + jnp.exp2(sink - m)
    return p, den


def _half_variants(x, low):
    xs = pltpu.roll(x, HALF, x.ndim - 1)
    zero = jnp.zeros_like(x)
    return (
        (jnp.where(low, x, zero).astype(BF16), jnp.where(low, zero, xs).astype(BF16)),
        (jnp.where(low, xs, zero).astype(BF16), jnp.where(low, zero, x).astype(BF16)),
    )


def _mix_prompt_kernel(sinks_ref, x_ref, sh_ref, sc_ref, g_ref, w_in_ref, ones_ref, gain_ref, bias_ref,
                       convw_ref, convb_ref, wgate_ref, bgate_ref, lam_ref, beta_ref, wout_ref,
                       y_ref, k_ref, v_ref, conv_ref, h_ref,
                       kprev, vprev, hcar, xr_scr, a_scr, u_scr, *, layer):
    n = pl.program_id(1)

    @pl.when(n == 0)
    def _():
        kprev[...] = jnp.zeros_like(kprev)
        vprev[...] = jnp.zeros_like(vprev)
        hcar[...] = jnp.zeros_like(hcar)
        xr_scr[:, 0:SUBLANES, :] = jnp.zeros((xr_scr.shape[0], SUBLANES, LANES), F32)

    subs = x_ref.shape[1] // SUB_ROWS
    low = lax.broadcasted_iota(jnp.int32, (Q_BLOCK + SUB_ROWS, LANES), 1) < HALF
    second_head = lax.broadcasted_iota(jnp.int32, (2 * Q_BLOCK, 1), 0) >= Q_BLOCK
    carry = dict(k=kprev[...], v=vprev[...], h=hcar[...])
    st = [dict() for _ in range(subs)]

    def rep(r8):
        return jnp.broadcast_to(r8[None], (SUB_ROWS // SUBLANES,) + r8.shape).reshape(SUB_ROWS, r8.shape[-1])

    shift, scale1, gate = rep(sh_ref[0]), rep(1.0 + sc_ref[0]), rep(g_ref[0])
    gain, beta, gate_bias = rep(gain_ref[...]), rep(beta_ref[...]), rep(bgate_ref[...])
    taps, conv_bias = [rep(convw_ref[i]) for i in range(CONV_WIDTH)], rep(convb_ref[...])
    softplus = rep(_softplus_neg(lam_ref[...]))

    def rows(j):
        return pl.ds(j * SUB_ROWS, SUB_ROWS)

    def modulate(j):
        st[j]["h"] = (_rms(x_ref[0, rows(j), :]) * scale1 + shift).astype(BF16)
        st[j]["z"] = []

    def project(j, tiles):
        for c in tiles:
            st[j]["z"].append(_dot(st[j]["h"], w_in_ref[:, c * MXU_DIM:(c + 1) * MXU_DIM]))

    def split(j):
        z = jnp.concatenate(st[j].pop("z"), axis=-1)
        q, k, v, xr, gr = _split_projection(z, ones_ref, gain)
        st[j].update(q=q, k=k, v=v, xr=xr, gr=gr)

    pairs = [(kv, half) for kv in range(N_KV_HEADS) for half in range(2)]

    def scores(j):
        s = st[j]
        kvar = _half_variants(jnp.concatenate([carry["k"], s["k"]], axis=0), low)
        s["vvar"] = _half_variants(jnp.concatenate([carry["v"], s["v"]], axis=0), low)
        carry.update(k=s["k"][SUB_ROWS - Q_BLOCK:], v=s["v"][SUB_ROWS - Q_BLOCK:])
        s["s"] = []
        for qb in range(SUB_ROWS // Q_BLOCK):
            bias_row = jnp.minimum(n, 1) if (j == 0 and qb == 0) else 1
            q = s["q"][qb * Q_BLOCK:(qb + 1) * Q_BLOCK]
            for kv, half in pairs:
                stacked = jnp.concatenate([q[:, (2 * kv) * LANES:(2 * kv + 1) * LANES],
                                           q[:, (2 * kv + 1) * LANES:(2 * kv + 2) * LANES]], axis=0)
                keys = kvar[kv][half][qb * Q_BLOCK:(qb + 2) * Q_BLOCK]
                s["s"].append(_dot_nt(stacked.astype(BF16), keys) + bias_ref[bias_row, 2 * kv + half])

    def softmax(j):
        s = st[j]
        s["p"] = []
        for i, sc in enumerate(s.pop("s")):
            kv, half = pairs[i % len(pairs)]
            first, second = GROUP * kv + half, GROUP * kv + half + 2
            sink = jnp.where(second_head, sinks_ref[layer, second], sinks_ref[layer, first])
            s["p"].append(_softmax_with_sink(sc, sink))

    def attend(j):
        s = st[j]
        vvar = s.pop("vvar")
        blocks = []
        for qb in range(SUB_ROWS // Q_BLOCK):
            out = {}
            for kv, half in pairs:
                p, den = s["p"][qb * len(pairs) + 2 * kv + half]
                out[kv, half] = _dot(p.astype(BF16), vvar[kv][half][qb * Q_BLOCK:(qb + 2) * Q_BLOCK]) / den
            tiles = []
            for kv in range(N_KV_HEADS):
                both = out[kv, 0] + out[kv, 1]
                tiles += [both[:Q_BLOCK], both[Q_BLOCK:]]
            blocks.append(jnp.concatenate(tiles, axis=-1))
        s.pop("p")
        s["attn"] = jnp.concatenate(blocks, axis=0)

    def conv(j):
        first = SUBLANES + j * SUB_ROWS
        _slab_store(xr_scr, first, SUB_ROWS, st[j]["xr"])
        st[j]["xc"] = _conv([_slab_load(xr_scr, first - kk, SUB_ROWS) for kk in range(CONV_WIDTH)],
                            taps, conv_bias)

    def gates(j):
        st[j]["gates"] = _lru_gates(st[j]["xc"], wgate_ref, gate_bias)

    def recur(j):
        s = st[j]
        a, u = _lru_inputs(s.pop("xc"), s.pop("gates"), softplus)
        slot = j % a_scr.shape[0]
        _slab_store(a_scr.at[slot], 0, SUB_ROWS, a)
        _slab_store(u_scr.at[slot], 0, SUB_ROWS, u)
        hs = _blocked_scan(a_scr.at[slot], u_scr.at[slot], 0, SUB_ROWS, carry["h"])
        carry.update(h=hs[SUB_ROWS - 1:, :])
        s["lru"] = hs * _gelu_tanh(s.pop("gr"))
        if j == subs - 1:
            h_ref[0] = hs[SUB_ROWS - SUBLANES:, :]

    def merge(j):
        st[j]["merged"] = _merged_heads(st[j].pop("attn"), st[j].pop("lru"), beta)

    def output(j):
        merged = st[j].pop("merged")
        for c in range(D_MODEL // MXU_DIM):
            cols = slice(c * MXU_DIM, (c + 1) * MXU_DIM)
            y_ref[0, rows(j), cols] = x_ref[0, rows(j), cols] + gate[:, cols] * _dot(merged, wout_ref[c])

    first_tiles = range(0, 4)
    last_tiles = range(4, IN_COLS // MXU_DIM)
    modulate(0)
    project(0, first_tiles)
    project(0, last_tiles)
    for j in range(subs):
        nxt = j + 1 < subs
        split(j)
        if j > 0:
            merge(j - 1)
        scores(j)
        if j > 0:
            output(j - 1)
        conv(j)
        if nxt:
            modulate(j + 1)
        gates(j)
        if nxt:
            project(j + 1, first_tiles)
        softmax(j)
        attend(j)
        if nxt:
            project(j + 1, last_tiles)
        recur(j)
    merge(subs - 1)
    output(subs - 1)

    last = st[subs - 1]
    k_ref[0] = carry["k"]
    v_ref[0] = carry["v"]
    conv_ref[0] = last["xr"][SUB_ROWS - SUBLANES:, :]
    _slab_store(xr_scr, 0, SUBLANES, last["xr"][SUB_ROWS - SUBLANES:, :])
    kprev[...] = carry["k"]
    vprev[...] = carry["v"]
    hcar[...] = carry["h"]


def _prompt_bias():
    i = np.arange(Q_BLOCK)[:, None]
    j = np.arange(2 * Q_BLOCK)[None, :]
    dist = Q_BLOCK + i - j
    band = (dist >= 0) & (dist <= WINDOW)
    slopes = np.asarray([2.0 ** (-8.0 * (h + 1) / N_HEADS) for h in range(N_HEADS)], np.float32)
    alibi = -(slopes[:, None, None] * dist[None].astype(np.float64)) * LOG2E
    general = np.where(band[None], alibi, np.float32(NEG_INF))
    first = np.where((band & (j >= Q_BLOCK))[None], alibi, np.float32(NEG_INF))
    per_head = np.stack([first, general]).astype(np.float32)
    pairs = [(GROUP * kv + half, GROUP * kv + half + 2) for kv in range(N_KV_HEADS) for half in range(2)]
    return np.stack([np.concatenate([per_head[:, a], per_head[:, b]], axis=1) for a, b in pairs], axis=1)


def _mix_prompt_call(x, mod, lw):
    b, t, d = x.shape
    nb = t // MIX_ROWS
    subs = MIX_ROWS // SUB_ROWS
    xspec = pl.BlockSpec((1, MIX_ROWS, d), lambda i, j: (i, j, 0))
    mspecs = _mod_specs(lw["layer"], 1, 3, lambda i, j: i, rows=SUBLANES)
    bias = jnp.asarray(_prompt_bias())
    last = lambda shape: pl.BlockSpec(shape, lambda i, j: (i, 0, 0))
    head = ["w_in", "ones", "gain"]
    tail = ["conv_w", "conv_b", "w_gate", "b_gate", "lam", "beta", "w_out"]
    outs = pl.pallas_call(
        functools.partial(_mix_prompt_kernel, layer=lw["layer"]),
        out_shape=(
            jax.ShapeDtypeStruct(x.shape, F32),
            jax.ShapeDtypeStruct((b, WINDOW, KV_W), F32),
            jax.ShapeDtypeStruct((b, WINDOW, KV_W), F32),
            jax.ShapeDtypeStruct((b, SUBLANES, LRU_W), F32),
            jax.ShapeDtypeStruct((b, SUBLANES, LRU_W), F32),
        ),
        grid=(b, nb),
        in_specs=[
            pl.BlockSpec(memory_space=pltpu.SMEM),
            xspec, *mspecs,
            *[_weight_spec(lw, key) for key in head],
            _resident(bias.shape),
            *[_weight_spec(lw, key) for key in tail],
        ],
        out_specs=(xspec, last((1, WINDOW, KV_W)), last((1, WINDOW, KV_W)),
                   last((1, SUBLANES, LRU_W)), last((1, SUBLANES, LRU_W))),
        scratch_shapes=[
            pltpu.VMEM((Q_BLOCK, KV_W), F32), pltpu.VMEM((Q_BLOCK, KV_W), F32),
            pltpu.VMEM((1, LRU_W), F32),
            pltpu.VMEM((LRU_W // LANES, SUBLANES + MIX_ROWS, LANES), F32),
            pltpu.VMEM((2, LRU_W // LANES, _scan_rows(SUB_ROWS), LANES), F32),
            pltpu.VMEM((2, LRU_W // LANES, _scan_rows(SUB_ROWS), LANES), F32),
        ],
        compiler_params=pltpu.CompilerParams(
            dimension_semantics=("arbitrary", "arbitrary"), vmem_limit_bytes=VMEM_LIMIT),
        name="mix_prompt",
    )(lw["sinks"], x, mod, mod, mod, *[lw[key] for key in head], bias, *[lw[key] for key in tail])
    return outs


def _mix_sample_kernel(x_ref, sh_ref, sc_ref, g_ref, ck_ref, cv_ref, cs_ref, h0_ref,
                       w_in_ref, ones_ref, gain_ref, bias_ref, sink_ref,
                       convw_ref, convb_ref, wgate_ref, bgate_ref, lam_ref, beta_ref, wout_ref,
                       y_ref, k_ref, v_ref, conv_ref, h_ref):
    sb, tq, d = x_ref.shape
    rows = sb * tq

    def flat(a):
        return jnp.broadcast_to(a, (sb, tq, a.shape[-1])).reshape(rows, a.shape[-1])

    x2 = x_ref[...].reshape(rows, d)
    z = _dot(_modulated(x2, flat(sh_ref[...]), flat(sc_ref[...])), w_in_ref[...])
    q, k, v, xr, gr = _split_projection(z, ones_ref, gain_ref[0:1])
    k3 = k.reshape(sb, tq, KV_W)
    v3 = v.reshape(sb, tq, KV_W)
    ck = ck_ref[...]
    cv = cv_ref[...]
    k_ref[...] = jnp.concatenate([ck[:, tq:, :], k3], axis=1)
    v_ref[...] = jnp.concatenate([cv[:, tq:, :], v3], axis=1)

    pad = jnp.zeros((sb, WINDOW - tq, KV_W), F32)
    kall = jnp.concatenate([ck, k3, pad], axis=1)
    vall = jnp.concatenate([cv, v3, pad], axis=1)
    low2 = lax.broadcasted_iota(jnp.int32, (rows, LANES), 1) < HALF
    zero2 = jnp.zeros((rows, LANES), F32)
    pieces = []
    for head in range(N_HEADS):
        tile, parity, kv = head // 2, head % 2, head // GROUP
        qt = q[:, tile * LANES:(tile + 1) * LANES]
        src = qt if parity == kv else pltpu.roll(qt, HALF, 1)
        piece = jnp.where(low2, src, zero2) if kv == 0 else jnp.where(low2, zero2, src)
        pieces.append(piece.reshape(sb, tq, LANES))
    qrows = jnp.concatenate(pieces, axis=1).astype(BF16)
    s = jnp.einsum("snc,sjc->snj", qrows, kall.astype(BF16), preferred_element_type=F32)
    s = s + bias_ref[...]
    p, den = _softmax_with_sink(s, sink_ref[...])
    o = jnp.einsum("snj,sjc->snc", p.astype(BF16), vall.astype(BF16), preferred_element_type=F32) / den
    tiles = []
    for tile in range(ATTN_W // LANES):
        kv = (2 * tile) // GROUP
        oe = o[:, (2 * tile) * tq:(2 * tile + 1) * tq, :].reshape(rows, LANES)
        oo = o[:, (2 * tile + 1) * tq:(2 * tile + 2) * tq, :].reshape(rows, LANES)
        if kv == 0:
            tiles.append(jnp.where(low2, oe, pltpu.roll(oo, HALF, 1)))
        else:
            tiles.append(jnp.where(low2, pltpu.roll(oe, HALF, 1), oo))
    attn = jnp.concatenate(tiles, axis=-1)

    t = lax.broadcasted_iota(jnp.int32, (sb, tq, 1), 1).reshape(rows, 1)
    state = cs_ref[...].reshape(rows, LRU_W)
    prev = {3: state, 2: pltpu.roll(state, rows - 1, 0), 1: pltpu.roll(state, rows - 2, 0)}
    delayed = [xr] + [_shift_rows(xr, kk, t, prev[kk]) for kk in (1, 2, 3)]
    xc = _conv(delayed, [convw_ref[i, 0:1] for i in range(CONV_WIDTH)], convb_ref[0:1])
    conv_ref[...] = xr.reshape(sb, tq, LRU_W)
    gates = _lru_gates(xc, wgate_ref, bgate_ref[0:1])
    a, u = _log_scan(*_lru_inputs(xc, gates, _softplus_neg(lam_ref[0:1])), t, tq)
    hs = a * flat(h0_ref[...]) + u
    h_ref[...] = hs.reshape(sb, tq, LRU_W)

    merged = _merged_heads(attn, hs * _gelu_tanh(gr), beta_ref[0:1])
    y = jnp.concatenate([_dot(merged, wout_ref[c]) for c in range(d // MXU_DIM)], axis=-1)
    y_ref[...] = (x2 + flat(g_ref[...]) * y).reshape(sb, tq, d)


def _sample_bias(tq):
    i = np.arange(tq)[:, None]
    j = np.arange(2 * WINDOW)[None, :]
    dist = WINDOW + i - j
    ok = (dist >= 0) & (dist <= WINDOW) & (j < WINDOW + tq)
    slopes = np.asarray([2.0 ** (-8.0 * (h + 1) / N_HEADS) for h in range(N_HEADS)], np.float32)
    alibi = -(slopes[:, None, None] * dist[None].astype(np.float64)) * LOG2E
    return np.where(ok[None], alibi, np.float32(NEG_INF)).reshape(N_HEADS * tq, 2 * WINDOW).astype(np.float32)


def _mix_sample_call(x, mod, k_windows, v_windows, conv_states, lru_states, lw):
    nseq, tq, d = x.shape
    sb = SAMPLE_SEQS
    layer = lw["layer"]
    seq_block = lambda shape: pl.BlockSpec(shape, lambda i: (i, 0, 0))
    layer_seq_block = lambda shape: pl.BlockSpec((None,) + shape, lambda i: (layer, i, 0, 0))
    first_block = layer * (nseq // sb)
    window_block = pl.BlockSpec((sb, WINDOW, KV_W), lambda i: (first_block + i, 0, 0))
    xspec = seq_block((sb, tq, d))
    mspecs = _mod_specs(layer, sb, 3, lambda i: i)
    bias = jnp.asarray(_sample_bias(tq))
    head = ["w_in", "ones", "gain"]
    tail = ["sink_col", "conv_w", "conv_b", "w_gate", "b_gate", "lam", "beta", "w_out"]
    return pl.pallas_call(
        _mix_sample_kernel,
        out_shape=(
            jax.ShapeDtypeStruct(x.shape, F32),
            jax.ShapeDtypeStruct(k_windows.shape, F32),
            jax.ShapeDtypeStruct(v_windows.shape, F32),
            jax.ShapeDtypeStruct((nseq, tq, LRU_W), F32),
            jax.ShapeDtypeStruct((nseq, tq, LRU_W), F32),
        ),
        grid=(nseq // sb,),
        in_specs=[
            xspec, *mspecs,
            window_block, window_block,
            layer_seq_block((sb, tq, LRU_W)), layer_seq_block((sb, 1, LRU_W)),
            *[_weight_spec(lw, key) for key in head],
            _resident(bias.shape),
            *[_weight_spec(lw, key) for key in tail],
        ],
        out_specs=(xspec, window_block, window_block,
                   seq_block((sb, tq, LRU_W)), seq_block((sb, tq, LRU_W))),
        input_output_aliases={4: 1, 5: 2},
        compiler_params=pltpu.CompilerParams(
            dimension_semantics=("arbitrary",), vmem_limit_bytes=VMEM_LIMIT),
        name="mix_sample",
    )(x, mod, mod, mod, k_windows, v_windows, conv_states, lru_states,
      *[lw[key] for key in head], bias, *[lw[key] for key in tail])


def _diagonal_tiles(w):
    depth, n, c, _ = w.shape
    per_tile = MXU_DIM // c
    tiles = w.reshape(depth, n // per_tile, per_tile, c, c)
    eye = jnp.eye(per_tile, dtype=w.dtype)
    dense = tiles[:, :, :, :, None, :] * eye[None, None, :, None, :, None]
    return dense.reshape(depth, n // per_tile, MXU_DIM, MXU_DIM)


def _rows8(v):
    return jnp.broadcast_to(v[..., None, :], v.shape[:-1] + (SUBLANES, v.shape[-1]))


def _mixer_weights(tq, w_in_bf16, q_gain, k_gain, sinks, conv_w, conv_b, w_rg, b_rg, w_ig, b_ig, lru_lambda,
                   beta_attn, beta_lru, w_out_tiles):
    head_of = np.arange(MXU_DIM) // HEAD_DIM
    ones = jnp.asarray((head_of[:, None] == head_of[None, :]) * (1.0 / HEAD_DIM), BF16)
    gain = jnp.concatenate([jnp.tile(q_gain * (HEAD_DIM ** -0.5 * LOG2E), (1, N_HEADS)),
                            jnp.tile(k_gain, (1, N_KV_HEADS))], axis=1)
    return dict(
        w_in=w_in_bf16,
        ones=ones,
        gain=_rows8(gain),
        sinks=sinks * LOG2E,
        sink_col=jnp.repeat(sinks * LOG2E, tq, axis=1)[..., None],
        conv_w=_rows8(conv_w),
        conv_b=_rows8(conv_b),
        w_gate=jnp.stack([_diagonal_tiles(w_rg), _diagonal_tiles(w_ig)], axis=1).astype(BF16),
        b_gate=_rows8(jnp.concatenate([b_rg, b_ig], axis=1)),
        lam=_rows8(lru_lambda),
        beta=_rows8(jnp.concatenate([beta_attn, beta_lru], axis=1)),
        w_out=w_out_tiles,
    )


def kernel(x_prompt, x_sample, cache_k, cache_v, state_conv, state_lru, c_prompt, c_sample, w_ada, b_ada, w1_gate, w1_up, w1_down, w_in, q_gain, k_gain, sinks, conv_w, conv_b, w_rg, b_rg, w_ig, b_ig, lru_lambda, beta_attn, beta_lru, w_out, w2_gate, w2_up, w2_down):
    nb = x_prompt.shape[0]
    ns, tq, _ = x_sample.shape
    mod = _mod_call(jnp.concatenate([c_prompt, c_sample], axis=0), w_ada, b_ada)
    mod_p = mod[:, :nb].reshape(DEPTH, nb, 1, N_MOD * D_MODEL)
    mod_p8 = _rows8(mod[:, :nb])
    mod_s = mod[:, nb:].reshape(DEPTH, ns, 1, N_MOD * D_MODEL)

    w1g, w1u, w2g, w2u = _cast_call([w1_gate, w1_up, w2_gate, w2_up], CAST_ROWS)
    w1d, w2d = _cast_call([w1_down, w2_down], D_FF // (D_MODEL // CAST_ROWS))
    (w_in_bf16,) = _cast_call([w_in], CAST_ROWS)
    w_out_tiles = _cast_column_tiles_call(w_out, CAST_ROWS)
    w1, w2 = (w1g, w1u, w1d), (w2g, w2u, w2d)

    k_windows = cache_k.reshape(DEPTH * ns, WINDOW, KV_W)
    v_windows = cache_v.reshape(DEPTH * ns, WINDOW, KV_W)
    conv_states = jnp.pad(state_conv, ((0, 0), (0, 0), (0, tq - (CONV_WIDTH - 1)), (0, 0)))
    lru_states = state_lru.reshape(DEPTH, ns, 1, LRU_W)
    weights = _mixer_weights(tq, w_in_bf16, q_gain, k_gain, sinks, conv_w, conv_b, w_rg, b_rg, w_ig, b_ig,
                             lru_lambda, beta_attn, beta_lru, w_out_tiles)
    yp, ys = x_prompt, x_sample
    outs_p, outs_s = [], []
    for l in range(DEPTH):
        lw = dict(weights, layer=l)

        yp = _ffn_call(yp, mod_p, 0, w1, l, 1, FFN_ROWS * FFN_PARTS, "ffn1_prompt")
        ys = _ffn_call(ys, mod_s, 0, w1, l, FFN_ROWS // tq, tq, "ffn1_sample")

        yp, kp, vp, cp, hp = _mix_prompt_call(yp, mod_p8, lw)
        ys, k_windows, v_windows, cs, hs = _mix_sample_call(
            ys, mod_s, k_windows, v_windows, conv_states, lru_states, lw)
        outs_p.append((kp.reshape(nb, WINDOW, N_KV_HEADS, HEAD_DIM), vp.reshape(nb, WINDOW, N_KV_HEADS, HEAD_DIM),
                       cp[:, SUBLANES - (CONV_WIDTH - 1):], hp[:, SUBLANES - 1]))
        outs_s.append((cs[:, tq - (CONV_WIDTH - 1):], hs[:, tq - 1]))

        yp = _ffn_call(yp, mod_p, 6, w2, l, 1, FFN_ROWS * FFN_PARTS, "ffn2_prompt")
        ys = _ffn_call(ys, mod_s, 6, w2, l, FFN_ROWS // tq, tq, "ffn2_sample")

    stack = lambda outs, k: jnp.stack([o[k] for o in outs])
    return (yp, ys,
            stack(outs_p, 0), stack(outs_p, 1), stack(outs_p, 2), stack(outs_p, 3),
            k_windows.reshape(cache_k.shape), v_windows.reshape(cache_v.shape),
            stack(outs_s, 0), stack(outs_s, 1))
```

```python
import functools

import numpy as np
import jax
import jax.numpy as jnp
from jax import lax
from jax.experimental import pallas as pl
from jax.experimental.pallas import tpu as pltpu

D_MODEL = 1024
DEPTH = 2
HEAD_DIM = 64
N_HEADS = 8
N_KV_HEADS = 2
GROUP = N_HEADS // N_KV_HEADS
ATTN_W = N_HEADS * HEAD_DIM
KV_W = N_KV_HEADS * HEAD_DIM
LRU_W = 512
N_LRU_BLOCKS = 8
LRU_BLOCK = LRU_W // N_LRU_BLOCKS
CONV_WIDTH = 4
RG_C = 8.0
WINDOW = 128
Q_BLOCK = 128
D_FF = 2816
N_MOD = 9
FFN_RES = 0.5
IN_COLS = ATTN_W + 2 * KV_W + 2 * LRU_W
QK_W = ATTN_W + KV_W
RMS_EPS = 1e-6
NEG_INF = -1e30
LOG2E = float(np.log2(np.e))

LANES = 128
SUBLANES = 8
HALF = LANES // 2
MXU_DIM = 256
SCAN_GROUP = 4
VMEM_LIMIT = 56 * 1024 * 1024

FFN_ROWS = 512
FFN_PARTS = 2
CAST_ROWS = 256
MIX_ROWS = 1024
SUB_ROWS = 256
SAMPLE_SEQS = 32

BF16 = jnp.bfloat16
F32 = jnp.float32


def _dot(a, b):
    return jnp.dot(a, b, preferred_element_type=F32)


def _dot_nt(a, b):
    return lax.dot_general(a, b, (((1,), (1,)), ((), ())), preferred_element_type=F32)


def _rms(x):
    return x * lax.rsqrt(jnp.mean(x * x, axis=-1, keepdims=True) + RMS_EPS)


def _resident(shape):
    nd = len(shape)
    return pl.BlockSpec(shape, lambda *_: (0,) * nd, pipeline_mode=pl.Buffered(1))


def _resident_layer(shape, layer):
    nd = len(shape)
    return pl.BlockSpec((None,) + tuple(shape[1:]), lambda *_: (layer,) + (0,) * (nd - 1),
                        pipeline_mode=pl.Buffered(1))


def _cast_kernel(*refs):
    n = len(refs) // 2
    for src, dst in zip(refs[:n], refs[n:]):
        dst[...] = src[...].astype(BF16)


def _cast_call(ws, rows):
    depth, r, c = ws[0].shape
    spec = pl.BlockSpec((1, rows, c), lambda l, i: (l, i, 0))
    return pl.pallas_call(
        _cast_kernel,
        out_shape=[jax.ShapeDtypeStruct(w.shape, BF16) for w in ws],
        grid=(depth, r // rows),
        in_specs=[spec] * len(ws),
        out_specs=[spec] * len(ws),
        compiler_params=pltpu.CompilerParams(
            dimension_semantics=("arbitrary", "arbitrary"), vmem_limit_bytes=VMEM_LIMIT),
        name="cast_weights",
    )(*ws)


def _cast_tiles_kernel(src, dst):
    for c in range(dst.shape[1]):
        dst[0, c] = src[0, :, c * MXU_DIM:(c + 1) * MXU_DIM].astype(BF16)


def _cast_column_tiles_call(w, rows):
    depth, k, n = w.shape
    return pl.pallas_call(
        _cast_tiles_kernel,
        out_shape=jax.ShapeDtypeStruct((depth, n // MXU_DIM, k, MXU_DIM), BF16),
        grid=(depth, k // rows),
        in_specs=[pl.BlockSpec((1, rows, n), lambda l, i: (l, i, 0))],
        out_specs=pl.BlockSpec((1, n // MXU_DIM, rows, MXU_DIM), lambda l, i: (l, 0, i, 0)),
        compiler_params=pltpu.CompilerParams(
            dimension_semantics=("arbitrary", "arbitrary"), vmem_limit_bytes=VMEM_LIMIT),
        name="cast_column_tiles",
    )(w)


def _mod_kernel(c_ref, w_ref, b_ref, o_ref):
    c = c_ref[...]
    h = (c * jax.nn.sigmoid(c)).astype(BF16)
    o_ref[0] = _dot(h, w_ref[0].astype(BF16)) + b_ref[0]


def _mod_call(c_all, w_ada, b_ada):
    n = c_all.shape[0]
    return pl.pallas_call(
        _mod_kernel,
        out_shape=jax.ShapeDtypeStruct((DEPTH, n, N_MOD * D_MODEL), F32),
        grid=(DEPTH, N_MOD),
        in_specs=[
            pl.BlockSpec((n, D_MODEL), lambda l, j: (0, 0)),
            pl.BlockSpec((1, D_MODEL, D_MODEL), lambda l, j: (l, 0, j)),
            pl.BlockSpec((1, 1, D_MODEL), lambda l, j: (l, 0, j)),
        ],
        out_specs=pl.BlockSpec((1, n, D_MODEL), lambda l, j: (l, 0, j)),
        compiler_params=pltpu.CompilerParams(
            dimension_semantics=("arbitrary", "arbitrary"), vmem_limit_bytes=VMEM_LIMIT),
        name="adaln_mod",
    )(c_all, w_ada, b_ada.reshape(DEPTH, 1, N_MOD * D_MODEL))


def _ffn_kernel(x_ref, sh_ref, sc_ref, g_ref, wg_ref, wu_ref, wd_ref, o_ref, *, parts):
    s, t, d = x_ref.shape
    along_t = s == 1
    ps, pt = (s, t // parts) if along_t else (s // parts, t)

    def piece(ref, p):
        if along_t:
            return ref[:, pl.ds(p * pt, pt), :] if ref.shape[1] == t else ref[...]
        return ref[pl.ds(p * ps, ps)]

    def modulated(p):
        h = _rms(piece(x_ref, p)) * (1.0 + piece(sc_ref, p)) + piece(sh_ref, p)
        return h.reshape(ps * pt, d).astype(BF16)

    h2 = modulated(0)
    g_next = _dot(h2, wg_ref[...])
    u_next = _dot(h2, wu_ref[...])
    for p in range(parts):
        g, u = g_next, u_next
        if p + 1 < parts:
            h2 = modulated(p + 1)
            g_next = _dot(h2, wg_ref[...])
        a = (g * jax.nn.sigmoid(g) * u).astype(BF16)
        y = _dot(a, wd_ref[...]).reshape(ps, pt, d)
        if along_t:
            o_ref[:, pl.ds(p * pt, pt), :] = piece(x_ref, p) + (FFN_RES * piece(g_ref, p)) * y
        else:
            o_ref[pl.ds(p * ps, ps)] = piece(x_ref, p) + (FFN_RES * piece(g_ref, p)) * y
        if p + 1 < parts:
            u_next = _dot(h2, wu_ref[...])


def _weight_spec(lw, key):
    if key == "ones":
        return _resident(lw[key].shape)
    return _resident_layer(lw[key].shape, lw["layer"])


def _mod_specs(layer, seqs, first_chunk, index, rows=1):
    def spec(chunk):
        return pl.BlockSpec((None, seqs, rows, D_MODEL), lambda *g: (layer, index(*g), 0, chunk))
    return [spec(first_chunk + k) for k in range(3)]


def _ffn_call(x, mod, first_chunk, weights, layer, seqs, rows, name):
    wg, wu, wd = weights
    nseq, t, d = x.shape
    parts = seqs * rows // FFN_ROWS
    grid = (nseq // seqs, t // rows)
    xspec = pl.BlockSpec((seqs, rows, d), lambda i, j: (i, j, 0))
    mspecs = _mod_specs(layer, seqs, first_chunk, lambda i, j: i)
    return pl.pallas_call(
        functools.partial(_ffn_kernel, parts=parts),
        out_shape=jax.ShapeDtypeStruct(x.shape, F32),
        grid=grid,
        in_specs=[xspec, *mspecs, *[_resident_layer(w.shape, layer) for w in weights]],
        out_specs=xspec,
        compiler_params=pltpu.CompilerParams(
            dimension_semantics=("arbitrary", "arbitrary"), vmem_limit_bytes=VMEM_LIMIT),
        name=name,
    )(x, mod, mod, mod, wg, wu, wd)


def _head_mean_square(qk, avg_ref):
    parts = []
    for c0 in range(0, qk.shape[1], MXU_DIM):
        w = min(MXU_DIM, qk.shape[1] - c0)
        sq = qk[:, c0:c0 + w] * qk[:, c0:c0 + w]
        parts.append(_dot(sq.astype(BF16), avg_ref[:w, :w]))
    return jnp.concatenate(parts, axis=-1)


def _modulated(x2, sh, sc):
    return (_rms(x2) * (1.0 + sc) + sh).astype(BF16)


def _split_projection(z, ones_ref, gain):
    qk = z[:, :QK_W]
    qkn = qk * lax.rsqrt(_head_mean_square(qk, ones_ref) + RMS_EPS) * gain
    q = qkn[:, :ATTN_W]
    k = qkn[:, ATTN_W:QK_W]
    v = z[:, QK_W:QK_W + KV_W]
    xr = z[:, QK_W + KV_W:QK_W + KV_W + LRU_W]
    gr = z[:, QK_W + KV_W + LRU_W:]
    return q, k, v, xr, gr


def _shift_rows(x, k, t, fill):
    return jnp.where(t >= k, pltpu.roll(x, k, 0), fill)


def _conv(delayed, taps, bias):
    y = bias + delayed[3] * taps[0]
    y = y + delayed[2] * taps[1]
    y = y + delayed[1] * taps[2]
    return y + delayed[0] * taps[3]


def _lru_gates(xc, wgate_ref, bias):
    xb = xc.astype(BF16)
    cols = [_dot(xb[:, t * MXU_DIM:(t + 1) * MXU_DIM], wgate_ref[gate, t])
            for gate in range(2) for t in range(LRU_W // MXU_DIM)]
    return jnp.concatenate(cols, axis=-1) + bias


def _softplus_neg(lam):
    return jnp.maximum(-lam, 0.0) + jnp.log1p(jnp.exp(-jnp.abs(lam)))


def _lru_inputs(xc, gates, softplus):
    r = jax.nn.sigmoid(gates[:, :LRU_W])
    gi = jax.nn.sigmoid(gates[:, LRU_W:])
    log_a = (-RG_C * r) * softplus
    a = jnp.exp(log_a)
    w = -jnp.tanh(log_a) * (a * a + 1.0)
    u = jnp.where(w > 0.0, w * lax.rsqrt(w), 0.0) * (gi * xc)
    return a, u


def _log_scan(a, u, t, period):
    s = 1
    while s < period:
        a_prev = _shift_rows(a, s, t, 1.0)
        u_prev = _shift_rows(u, s, t, 0.0)
        u = u + a * u_prev
        a = a * a_prev
        s *= 2
    return a, u


def _slab_load(ref, start, n, stride=1):
    rows = pl.ds(start, n) if stride == 1 else pl.ds(start, n, stride=stride)
    return jnp.concatenate([ref[s, rows, :] for s in range(ref.shape[0])], axis=-1)


def _slab_store(ref, start, n, val, stride=1):
    rows = pl.ds(start, n) if stride == 1 else pl.ds(start, n, stride=stride)
    for s in range(ref.shape[0]):
        ref[s, rows, :] = val[:, s * LANES:(s + 1) * LANES]


def _blocked_scan(a_scr, u_scr, base, n, h_init):
    if n <= 2 * SUBLANES:
        t = lax.broadcasted_iota(jnp.int32, (n, 1), 0)
        a, u = _log_scan(_slab_load(a_scr, base, n), _slab_load(u_scr, base, n), t, n)
        return a * h_init + u
    m = n // SCAN_GROUP
    a_loc, h_loc = [], []
    for r in range(SCAN_GROUP):
        a_r = _slab_load(a_scr, base + r, m, SCAN_GROUP)
        u_r = _slab_load(u_scr, base + r, m, SCAN_GROUP)
        h_loc.append(u_r if r == 0 else a_r * h_loc[-1] + u_r)
        a_loc.append(a_r if r == 0 else a_r * a_loc[-1])
    _slab_store(a_scr, base + n, m, a_loc[-1])
    _slab_store(u_scr, base + n, m, h_loc[-1])
    ends = _blocked_scan(a_scr, u_scr, base + n, m, h_init)
    g = lax.broadcasted_iota(jnp.int32, (m, 1), 0)
    carry = jnp.where(g >= 1, pltpu.roll(ends, 1, 0), h_init)
    for r in range(SCAN_GROUP):
        _slab_store(u_scr, base + r, m, a_loc[r] * carry + h_loc[r], SCAN_GROUP)
    return _slab_load(u_scr, base, n)


def _scan_rows(n):
    return n if n <= 2 * SUBLANES else n + _scan_rows(n // SCAN_GROUP)


def _gelu_tanh(x):
    return 0.5 * x * (1.0 + jnp.tanh(np.sqrt(2.0 / np.pi) * (x + 0.044715 * (x * x * x))))


def _merged_heads(attn, lru, beta):
    return (jnp.concatenate([_rms(attn), _rms(lru)], axis=-1) * beta).astype(BF16)


def _softmax_with_sink(s, sink):
    m = jnp.maximum(jnp.max(s, axis=-1, keepdims=True), sink)
    p = jnp.exp2(s - m)
    den = jnp.sum(p, axis=-1, keepdims=True) + jnp.exp2(sink - m)
    return p, den


def _half_variants(x, low):
    xs = pltpu.roll(x, HALF, x.ndim - 1)
    zero = jnp.zeros_like(x)
    return (
        (jnp.where(low, x, zero).astype(BF16), jnp.where(low, zero, xs).astype(BF16)),
        (jnp.where(low, xs, zero).astype(BF16), jnp.where(low, zero, x).astype(BF16)),
    )


def _mix_prompt_kernel(sinks_ref, x_ref, sh_ref, sc_ref, g_ref, w_in_ref, ones_ref, gain_ref, bias_ref,
                       convw_ref, convb_ref, wgate_ref, bgate_ref, lam_ref, beta_ref, wout_ref,
                       y_ref, k_ref, v_ref, conv_ref, h_ref,
                       kprev, vprev, hcar, xr_scr, a_scr, u_scr, *, layer):
    n = pl.program_id(1)

    @pl.when(n == 0)
    def _():
        kprev[...] = jnp.zeros_like(kprev)
        vprev[...] = jnp.zeros_like(vprev)
        hcar[...] = jnp.zeros_like(hcar)
        xr_scr[:, 0:SUBLANES, :] = jnp.zeros((xr_scr.shape[0], SUBLANES, LANES), F32)

    subs = x_ref.shape[1] // SUB_ROWS
    low = lax.broadcasted_iota(jnp.int32, (Q_BLOCK + SUB_ROWS, LANES), 1) < HALF
    second_head = lax.broadcasted_iota(jnp.int32, (2 * Q_BLOCK, 1), 0) >= Q_BLOCK
    carry = dict(k=kprev[...], v=vprev[...], h=hcar[...])
    st = [dict() for _ in range(subs)]

    def rep(r8):
        return jnp.broadcast_to(r8[None], (SUB_ROWS // SUBLANES,) + r8.shape).reshape(SUB_ROWS, r8.shape[-1])

    shift, scale1, gate = rep(sh_ref[0]), rep(1.0 + sc_ref[0]), rep(g_ref[0])
    gain, beta, gate_bias = rep(gain_ref[...]), rep(beta_ref[...]), rep(bgate_ref[...])
    taps, conv_bias = [rep(convw_ref[i]) for i in range(CONV_WIDTH)], rep(convb_ref[...])
    softplus = rep(_softplus_neg(lam_ref[...]))

    def rows(j):
        return pl.ds(j * SUB_ROWS, SUB_ROWS)

    def modulate(j):
        st[j]["h"] = (_rms(x_ref[0, rows(j), :]) * scale1 + shift).astype(BF16)
        st[j]["z"] = []

    def project(j, tiles):
        for c in tiles:
            st[j]["z"].append(_dot(st[j]["h"], w_in_ref[:, c * MXU_DIM:(c + 1) * MXU_DIM]))

    def split(j):
        z = jnp.concatenate(st[j].pop("z"), axis=-1)
        q, k, v, xr, gr = _split_projection(z, ones_ref, gain)
        st[j].update(q=q, k=k, v=v, xr=xr, gr=gr)

    pairs = [(kv, half) for kv in range(N_KV_HEADS) for half in range(2)]

    def scores(j):
        s = st[j]
        kvar = _half_variants(jnp.concatenate([carry["k"], s["k"]], axis=0), low)
        s["vvar"] = _half_variants(jnp.concatenate([carry["v"], s["v"]], axis=0), low)
        carry.update(k=s["k"][SUB_ROWS - Q_BLOCK:], v=s["v"][SUB_ROWS - Q_BLOCK:])
        s["s"] = []
        for qb in range(SUB_ROWS // Q_BLOCK):
            bias_row = jnp.minimum(n, 1) if (j == 0 and qb == 0) else 1
            q = s["q"][qb * Q_BLOCK:(qb + 1) * Q_BLOCK]
            for kv, half in pairs:
                stacked = jnp.concatenate([q[:, (2 * kv) * LANES:(2 * kv + 1) * LANES],
                                           q[:, (2 * kv + 1) * LANES:(2 * kv + 2) * LANES]], axis=0)
                keys = kvar[kv][half][qb * Q_BLOCK:(qb + 2) * Q_BLOCK]
                s["s"].append(_dot_nt(stacked.astype(BF16), keys) + bias_ref[bias_row, 2 * kv + half])

    def softmax(j):
        s = st[j]
        s["p"] = []
        for i, sc in enumerate(s.pop("s")):
            kv, half = pairs[i % len(pairs)]
            first, second = GROUP * kv + half, GROUP * kv + half + 2
            sink = jnp.where(second_head, sinks_ref[layer, second], sinks_ref[layer, first])
            s["p"].append(_softmax_with_sink(sc, sink))

    def attend(j):
        s = st[j]
        vvar = s.pop("vvar")
        blocks = []
        for qb in range(SUB_ROWS // Q_BLOCK):
            out = {}
            for kv, half in pairs:
                p, den = s["p"][qb * len(pairs) + 2 * kv + half]
                out[kv, half] = _dot(p.astype(BF16), vvar[kv][half][qb * Q_BLOCK:(qb + 2) * Q_BLOCK]) / den
            tiles = []
            for kv in range(N_KV_HEADS):
                both = out[kv, 0] + out[kv, 1]
                tiles += [both[:Q_BLOCK], both[Q_BLOCK:]]
            blocks.append(jnp.concatenate(tiles, axis=-1))
        s.pop("p")
        s["attn"] = jnp.concatenate(blocks, axis=0)

    def conv(j):
        first = SUBLANES + j * SUB_ROWS
        _slab_store(xr_scr, first, SUB_ROWS, st[j]["xr"])
        st[j]["xc"] = _conv([_slab_load(xr_scr, first - kk, SUB_ROWS) for kk in range(CONV_WIDTH)],
                            taps, conv_bias)

    def gates(j):
        st[j]["gates"] = _lru_gates(st[j]["xc"], wgate_ref, gate_bias)

    def recur(j):
        s = st[j]
        a, u = _lru_inputs(s.pop("xc"), s.pop("gates"), softplus)
        _slab_store(a_scr.at[j], 0, SUB_ROWS, a)
        _slab_store(u_scr.at[j], 0, SUB_ROWS, u)
        hs = _blocked_scan(a_scr.at[j], u_scr.at[j], 0, SUB_ROWS, carry["h"])
        carry.update(h=hs[SUB_ROWS - 1:, :])
        s["lru"] = hs * _gelu_tanh(s.pop("gr"))
        if j == subs - 1:
            h_ref[0] = hs[SUB_ROWS - SUBLANES:, :]

    def merge(j):
        st[j]["merged"] = _merged_heads(st[j].pop("attn"), st[j].pop("lru"), beta)

    def output(j):
        merged = st[j].pop("merged")
        for c in range(D_MODEL // MXU_DIM):
            cols = slice(c * MXU_DIM, (c + 1) * MXU_DIM)
            y_ref[0, rows(j), cols] = x_ref[0, rows(j), cols] + gate[:, cols] * _dot(merged, wout_ref[c])

    first_tiles = range(0, 6)
    last_tiles = range(6, IN_COLS // MXU_DIM)
    modulate(0)
    project(0, first_tiles)
    project(0, last_tiles)
    for j in range(subs):
        nxt = j + 1 < subs
        split(j)
        if j > 0:
            merge(j - 1)
            output(j - 1)
        scores(j)
        conv(j)
        if nxt:
            modulate(j + 1)
        gates(j)
        if nxt:
            project(j + 1, first_tiles)
        softmax(j)
        attend(j)
        if nxt:
            project(j + 1, last_tiles)
        recur(j)
    merge(subs - 1)
    output(subs - 1)

    last = st[subs - 1]
    k_ref[0] = carry["k"]
    v_ref[0] = carry["v"]
    conv_ref[0] = last["xr"][SUB_ROWS - SUBLANES:, :]
    _slab_store(xr_scr, 0, SUBLANES, last["xr"][SUB_ROWS - SUBLANES:, :])
    kprev[...] = carry["k"]
    vprev[...] = carry["v"]
    hcar[...] = carry["h"]


def _prompt_bias():
    i = np.arange(Q_BLOCK)[:, None]
    j = np.arange(2 * Q_BLOCK)[None, :]
    dist = Q_BLOCK + i - j
    band = (dist >= 0) & (dist <= WINDOW)
    slopes = np.asarray([2.0 ** (-8.0 * (h + 1) / N_HEADS) for h in range(N_HEADS)], np.float32)
    alibi = -(slopes[:, None, None] * dist[None].astype(np.float64)) * LOG2E
    general = np.where(band[None], alibi, np.float32(NEG_INF))
    first = np.where((band & (j >= Q_BLOCK))[None], alibi, np.float32(NEG_INF))
    per_head = np.stack([first, general]).astype(np.float32)
    pairs = [(GROUP * kv + half, GROUP * kv + half + 2) for kv in range(N_KV_HEADS) for half in range(2)]
    return np.stack([np.concatenate([per_head[:, a], per_head[:, b]], axis=1) for a, b in pairs], axis=1)


def _mix_prompt_call(x, mod, lw):
    b, t, d = x.shape
    nb = t // MIX_ROWS
    subs = MIX_ROWS // SUB_ROWS
    xspec = pl.BlockSpec((1, MIX_ROWS, d), lambda i, j: (i, j, 0))
    mspecs = _mod_specs(lw["layer"], 1, 3, lambda i, j: i, rows=SUBLANES)
    bias = jnp.asarray(_prompt_bias())
    last = lambda shape: pl.BlockSpec(shape, lambda i, j: (i, 0, 0))
    head = ["w_in", "ones", "gain"]
    tail = ["conv_w", "conv_b", "w_gate", "b_gate", "lam", "beta", "w_out"]
    outs = pl.pallas_call(
        functools.partial(_mix_prompt_kernel, layer=lw["layer"]),
        out_shape=(
            jax.ShapeDtypeStruct(x.shape, F32),
            jax.ShapeDtypeStruct((b, WINDOW, KV_W), F32),
            jax.ShapeDtypeStruct((b, WINDOW, KV_W), F32),
            jax.ShapeDtypeStruct((b, SUBLANES, LRU_W), F32),
            jax.ShapeDtypeStruct((b, SUBLANES, LRU_W), F32),
        ),
        grid=(b, nb),
        in_specs=[
            pl.BlockSpec(memory_space=pltpu.SMEM),
            xspec, *mspecs,
            *[_weight_spec(lw, key) for key in head],
            _resident(bias.shape),
            *[_weight_spec(lw, key) for key in tail],
        ],
        out_specs=(xspec, last((1, WINDOW, KV_W)), last((1, WINDOW, KV_W)),
                   last((1, SUBLANES, LRU_W)), last((1, SUBLANES, LRU_W))),
        scratch_shapes=[
            pltpu.VMEM((Q_BLOCK, KV_W), F32), pltpu.VMEM((Q_BLOCK, KV_W), F32),
            pltpu.VMEM((1, LRU_W), F32),
            pltpu.VMEM((LRU_W // LANES, SUBLANES + MIX_ROWS, LANES), F32),
            pltpu.VMEM((subs, LRU_W // LANES, _scan_rows(SUB_ROWS), LANES), F32),
            pltpu.VMEM((subs, LRU_W // LANES, _scan_rows(SUB_ROWS), LANES), F32),
        ],
        compiler_params=pltpu.CompilerParams(
            dimension_semantics=("arbitrary", "arbitrary"), vmem_limit_bytes=VMEM_LIMIT),
        name="mix_prompt",
    )(lw["sinks"], x, mod, mod, mod, *[lw[key] for key in head], bias, *[lw[key] for key in tail])
    return outs


def _mix_sample_kernel(x_ref, sh_ref, sc_ref, g_ref, ck_ref, cv_ref, cs_ref, h0_ref,
                       w_in_ref, ones_ref, gain_ref, bias_ref, sink_ref,
                       convw_ref, convb_ref, wgate_ref, bgate_ref, lam_ref, beta_ref, wout_ref,
                       y_ref, k_ref, v_ref, conv_ref, h_ref):
    sb, tq, d = x_ref.shape
    rows = sb * tq

    def flat(a):
        return jnp.broadcast_to(a, (sb, tq, a.shape[-1])).reshape(rows, a.shape[-1])

    x2 = x_ref[...].reshape(rows, d)
    z = _dot(_modulated(x2, flat(sh_ref[...]), flat(sc_ref[...])), w_in_ref[...])
    q, k, v, xr, gr = _split_projection(z, ones_ref, gain_ref[0:1])
    k3 = k.reshape(sb, tq, KV_W)
    v3 = v.reshape(sb, tq, KV_W)
    ck = ck_ref[...]
    cv = cv_ref[...]
    k_ref[...] = jnp.concatenate([ck[:, tq:, :], k3], axis=1)
    v_ref[...] = jnp.concatenate([cv[:, tq:, :], v3], axis=1)

    pad = jnp.zeros((sb, WINDOW - tq, KV_W), F32)
    kall = jnp.concatenate([ck, k3, pad], axis=1)
    vall = jnp.concatenate([cv, v3, pad], axis=1)
    low2 = lax.broadcasted_iota(jnp.int32, (rows, LANES), 1) < HALF
    zero2 = jnp.zeros((rows, LANES), F32)
    pieces = []
    for head in range(N_HEADS):
        tile, parity, kv = head // 2, head % 2, head // GROUP
        qt = q[:, tile * LANES:(tile + 1) * LANES]
        src = qt if parity == kv else pltpu.roll(qt, HALF, 1)
        piece = jnp.where(low2, src, zero2) if kv == 0 else jnp.where(low2, zero2, src)
        pieces.append(piece.reshape(sb, tq, LANES))
    qrows = jnp.concatenate(pieces, axis=1).astype(BF16)
    s = jnp.einsum("snc,sjc->snj", qrows, kall.astype(BF16), preferred_element_type=F32)
    s = s + bias_ref[...]
    p, den = _softmax_with_sink(s, sink_ref[...])
    o = jnp.einsum("snj,sjc->snc", p.astype(BF16), vall.astype(BF16), preferred_element_type=F32) / den
    tiles = []
    for tile in range(ATTN_W // LANES):
        kv = (2 * tile) // GROUP
        oe = o[:, (2 * tile) * tq:(2 * tile + 1) * tq, :].reshape(rows, LANES)
        oo = o[:, (2 * tile + 1) * tq:(2 * tile + 2) * tq, :].reshape(rows, LANES)
        if kv == 0:
            tiles.append(jnp.where(low2, oe, pltpu.roll(oo, HALF, 1)))
        else:
            tiles.append(jnp.where(low2, pltpu.roll(oe, HALF, 1), oo))
    attn = jnp.concatenate(tiles, axis=-1)

    t = lax.broadcasted_iota(jnp.int32, (sb, tq, 1), 1).reshape(rows, 1)
    state = cs_ref[...].reshape(rows, LRU_W)
    prev = {3: state, 2: pltpu.roll(state, rows - 1, 0), 1: pltpu.roll(state, rows - 2, 0)}
    delayed = [xr] + [_shift_rows(xr, kk, t, prev[kk]) for kk in (1, 2, 3)]
    xc = _conv(delayed, [convw_ref[i, 0:1] for i in range(CONV_WIDTH)], convb_ref[0:1])
    conv_ref[...] = xr.reshape(sb, tq, LRU_W)
    gates = _lru_gates(xc, wgate_ref, bgate_ref[0:1])
    a, u = _log_scan(*_lru_inputs(xc, gates, _softplus_neg(lam_ref[0:1])), t, tq)
    hs = a * flat(h0_ref[...]) + u
    h_ref[...] = hs.reshape(sb, tq, LRU_W)

    merged = _merged_heads(attn, hs * _gelu_tanh(gr), beta_ref[0:1])
    y = jnp.concatenate([_dot(merged, wout_ref[c]) for c in range(d // MXU_DIM)], axis=-1)
    y_ref[...] = (x2 + flat(g_ref[...]) * y).reshape(sb, tq, d)


def _sample_bias(tq):
    i = np.arange(tq)[:, None]
    j = np.arange(2 * WINDOW)[None, :]
    dist = WINDOW + i - j
    ok = (dist >= 0) & (dist <= WINDOW) & (j < WINDOW + tq)
    slopes = np.asarray([2.0 ** (-8.0 * (h + 1) / N_HEADS) for h in range(N_HEADS)], np.float32)
    alibi = -(slopes[:, None, None] * dist[None].astype(np.float64)) * LOG2E
    return np.where(ok[None], alibi, np.float32(NEG_INF)).reshape(N_HEADS * tq, 2 * WINDOW).astype(np.float32)


def _mix_sample_call(x, mod, k_windows, v_windows, conv_states, lru_states, lw):
    nseq, tq, d = x.shape
    sb = SAMPLE_SEQS
    layer = lw["layer"]
    seq_block = lambda shape: pl.BlockSpec(shape, lambda i: (i, 0, 0))
    layer_seq_block = lambda shape: pl.BlockSpec((None,) + shape, lambda i: (layer, i, 0, 0))
    first_block = layer * (nseq // sb)
    window_block = pl.BlockSpec((sb, WINDOW, KV_W), lambda i: (first_block + i, 0, 0))
    xspec = seq_block((sb, tq, d))
    mspecs = _mod_specs(layer, sb, 3, lambda i: i)
    bias = jnp.asarray(_sample_bias(tq))
    head = ["w_in", "ones", "gain"]
    tail = ["sink_col", "conv_w", "conv_b", "w_gate", "b_gate", "lam", "beta", "w_out"]
    return pl.pallas_call(
        _mix_sample_kernel,
        out_shape=(
            jax.ShapeDtypeStruct(x.shape, F32),
            jax.ShapeDtypeStruct(k_windows.shape, F32),
            jax.ShapeDtypeStruct(v_windows.shape, F32),
            jax.ShapeDtypeStruct((nseq, tq, LRU_W), F32),
            jax.ShapeDtypeStruct((nseq, tq, LRU_W), F32),
        ),
        grid=(nseq // sb,),
        in_specs=[
            xspec, *mspecs,
            window_block, window_block,
            layer_seq_block((sb, tq, LRU_W)), layer_seq_block((sb, 1, LRU_W)),
            *[_weight_spec(lw, key) for key in head],
            _resident(bias.shape),
            *[_weight_spec(lw, key) for key in tail],
        ],
        out_specs=(xspec, window_block, window_block,
                   seq_block((sb, tq, LRU_W)), seq_block((sb, tq, LRU_W))),
        input_output_aliases={4: 1, 5: 2},
        compiler_params=pltpu.CompilerParams(
            dimension_semantics=("arbitrary",), vmem_limit_bytes=VMEM_LIMIT),
        name="mix_sample",
    )(x, mod, mod, mod, k_windows, v_windows, conv_states, lru_states,
      *[lw[key] for key in head], bias, *[lw[key] for key in tail])


def _diagonal_tiles(w):
    depth, n, c, _ = w.shape
    per_tile = MXU_DIM // c
    tiles = w.reshape(depth, n // per_tile, per_tile, c, c)
    eye = jnp.eye(per_tile, dtype=w.dtype)
    dense = tiles[:, :, :, :, None, :] * eye[None, None, :, None, :, None]
    return dense.reshape(depth, n // per_tile, MXU_DIM, MXU_DIM)


def _rows8(v):
    return jnp.broadcast_to(v[..., None, :], v.shape[:-1] + (SUBLANES, v.shape[-1]))


def _mixer_weights(tq, w_in_bf16, q_gain, k_gain, sinks, conv_w, conv_b, w_rg, b_rg, w_ig, b_ig, lru_lambda,
                   beta_attn, beta_lru, w_out_tiles):
    head_of = np.arange(MXU_DIM) // HEAD_DIM
    ones = jnp.asarray((head_of[:, None] == head_of[None, :]) * (1.0 / HEAD_DIM), BF16)
    gain = jnp.concatenate([jnp.tile(q_gain * (HEAD_DIM ** -0.5 * LOG2E), (1, N_HEADS)),
                            jnp.tile(k_gain, (1, N_KV_HEADS))], axis=1)
    return dict(
        w_in=w_in_bf16,
        ones=ones,
        gain=_rows8(gain),
        sinks=sinks * LOG2E,
        sink_col=jnp.repeat(sinks * LOG2E, tq, axis=1)[..., None],
        conv_w=_rows8(conv_w),
        conv_b=_rows8(conv_b),
        w_gate=jnp.stack([_diagonal_tiles(w_rg), _diagonal_tiles(w_ig)], axis=1).astype(BF16),
        b_gate=_rows8(jnp.concatenate([b_rg, b_ig], axis=1)),
        lam=_rows8(lru_lambda),
        beta=_rows8(jnp.concatenate([beta_attn, beta_lru], axis=1)),
        w_out=w_out_tiles,
    )


def kernel(x_prompt, x_sample, cache_k, cache_v, state_conv, state_lru, c_prompt, c_sample, w_ada, b_ada, w1_gate, w1_up, w1_down, w_in, q_gain, k_gain, sinks, conv_w, conv_b, w_rg, b_rg, w_ig, b_ig, lru_lambda, beta_attn, beta_lru, w_out, w2_gate, w2_up, w2_down):
    nb = x_prompt.shape[0]
    ns, tq, _ = x_sample.shape
    mod = _mod_call(jnp.concatenate([c_prompt, c_sample], axis=0), w_ada, b_ada)
    mod_p = mod[:, :nb].reshape(DEPTH, nb, 1, N_MOD * D_MODEL)
    mod_p8 = _rows8(mod[:, :nb])
    mod_s = mod[:, nb:].reshape(DEPTH, ns, 1, N_MOD * D_MODEL)

    w1g, w1u, w2g, w2u = _cast_call([w1_gate, w1_up, w2_gate, w2_up], CAST_ROWS)
    w1d, w2d = _cast_call([w1_down, w2_down], D_FF // (D_MODEL // CAST_ROWS))
    (w_in_bf16,) = _cast_call([w_in], CAST_ROWS)
    w_out_tiles = _cast_column_tiles_call(w_out, CAST_ROWS)
    w1, w2 = (w1g, w1u, w1d), (w2g, w2u, w2d)

    k_windows = cache_k.reshape(DEPTH * ns, WINDOW, KV_W)
    v_windows = cache_v.reshape(DEPTH * ns, WINDOW, KV_W)
    conv_states = jnp.pad(state_conv, ((0, 0), (0, 0), (0, tq - (CONV_WIDTH - 1)), (0, 0)))
    lru_states = state_lru.reshape(DEPTH, ns, 1, LRU_W)
    weights = _mixer_weights(tq, w_in_bf16, q_gain, k_gain, sinks, conv_w, conv_b, w_rg, b_rg, w_ig, b_ig,
                             lru_lambda, beta_attn, beta_lru, w_out_tiles)
    yp, ys = x_prompt, x_sample
    outs_p, outs_s = [], []
    for l in range(DEPTH):
        lw = dict(weights, layer=l)

        yp = _ffn_call(yp, mod_p, 0, w1, l, 1, FFN_ROWS * FFN_PARTS, "ffn1_prompt")
        ys = _ffn_call(ys, mod_s, 0, w1, l, FFN_ROWS // tq, tq, "ffn1_sample")

        yp, kp, vp, cp, hp = _mix_prompt_call(yp, mod_p8, lw)
        ys, k_windows, v_windows, cs, hs = _mix_sample_call(
            ys, mod_s, k_windows, v_windows, conv_states, lru_states, lw)
        outs_p.append((kp.reshape(nb, WINDOW, N_KV_HEADS, HEAD_DIM), vp.reshape(nb, WINDOW, N_KV_HEADS, HEAD_DIM),
                       cp[:, SUBLANES - (CONV_WIDTH - 1):], hp[:, SUBLANES - 1]))
        outs_s.append((cs[:, tq - (CONV_WIDTH - 1):], hs[:, tq - 1]))

        yp = _ffn_call(yp, mod_p, 6, w2, l, 1, FFN_ROWS * FFN_PARTS, "ffn2_prompt")
        ys = _ffn_call(ys, mod_s, 6, w2, l, FFN_ROWS // tq, tq, "ffn2_sample")

    stack = lambda outs, k: jnp.stack([o[k] for o in outs])
    return (yp, ys,
            stack(outs_p, 0), stack(outs_p, 1), stack(outs_p, 2), stack(outs_p, 3),
            k_windows.reshape(cache_k.shape), v_windows.reshape(cache_v.shape),
            stack(outs_s, 0), stack(outs_s, 1))
```

```python
import functools

import numpy as np
import jax
import jax.numpy as jnp
from jax import lax
from jax.experimental import pallas as pl
from jax.experimental.pallas import tpu as pltpu

D_MODEL = 1024
DEPTH = 2
HEAD_DIM = 64
N_HEADS = 8
N_KV_HEADS = 2
GROUP = N_HEADS // N_KV_HEADS
ATTN_W = N_HEADS * HEAD_DIM
KV_W = N_KV_HEADS * HEAD_DIM
LRU_W = 512
N_LRU_BLOCKS = 8
LRU_BLOCK = LRU_W // N_LRU_BLOCKS
CONV_WIDTH = 4
RG_C = 8.0
WINDOW = 128
Q_BLOCK = 128
D_FF = 2816
N_MOD = 9
FFN_RES = 0.5
IN_COLS = ATTN_W + 2 * KV_W + 2 * LRU_W
QK_W = ATTN_W + KV_W
RMS_EPS = 1e-6
NEG_INF = -1e30
LOG2E = float(np.log2(np.e))

LANES = 128
SUBLANES = 8
HALF = LANES // 2
MXU_DIM = 256
SCAN_GROUP = 4
VMEM_LIMIT = 56 * 1024 * 1024

FFN_ROWS = 512
FFN_PARTS = 2
CAST_ROWS = 256
MIX_ROWS = 1024
SUB_ROWS = 256
SAMPLE_SEQS = 32

BF16 = jnp.bfloat16
F32 = jnp.float32


def _dot(a, b):
    return jnp.dot(a, b, preferred_element_type=F32)


def _dot_nt(a, b):
    return lax.dot_general(a, b, (((1,), (1,)), ((), ())), preferred_element_type=F32)


def _rms(x):
    return x * lax.rsqrt(jnp.mean(x * x, axis=-1, keepdims=True) + RMS_EPS)


def _resident(shape):
    nd = len(shape)
    return pl.BlockSpec(shape, lambda *_: (0,) * nd, pipeline_mode=pl.Buffered(1))


def _resident_layer(shape, layer):
    nd = len(shape)
    return pl.BlockSpec((None,) + tuple(shape[1:]), lambda *_: (layer,) + (0,) * (nd - 1),
                        pipeline_mode=pl.Buffered(1))


def _cast_kernel(*refs):
    n = len(refs) // 2
    for src, dst in zip(refs[:n], refs[n:]):
        dst[...] = src[...].astype(BF16)


def _cast_call(ws, rows):
    depth, r, c = ws[0].shape
    spec = pl.BlockSpec((1, rows, c), lambda l, i: (l, i, 0))
    return pl.pallas_call(
        _cast_kernel,
        out_shape=[jax.ShapeDtypeStruct(w.shape, BF16) for w in ws],
        grid=(depth, r // rows),
        in_specs=[spec] * len(ws),
        out_specs=[spec] * len(ws),
        compiler_params=pltpu.CompilerParams(
            dimension_semantics=("arbitrary", "arbitrary"), vmem_limit_bytes=VMEM_LIMIT),
        name="cast_weights",
    )(*ws)


def _cast_tiles_kernel(src, dst):
    for c in range(dst.shape[1]):
        dst[0, c] = src[0, :, c * MXU_DIM:(c + 1) * MXU_DIM].astype(BF16)


def _cast_column_tiles_call(w, rows):
    depth, k, n = w.shape
    return pl.pallas_call(
        _cast_tiles_kernel,
        out_shape=jax.ShapeDtypeStruct((depth, n // MXU_DIM, k, MXU_DIM), BF16),
        grid=(depth, k // rows),
        in_specs=[pl.BlockSpec((1, rows, n), lambda l, i: (l, i, 0))],
        out_specs=pl.BlockSpec((1, n // MXU_DIM, rows, MXU_DIM), lambda l, i: (l, 0, i, 0)),
        compiler_params=pltpu.CompilerParams(
            dimension_semantics=("arbitrary", "arbitrary"), vmem_limit_bytes=VMEM_LIMIT),
        name="cast_column_tiles",
    )(w)


def _mod_kernel(c_ref, w_ref, b_ref, o_ref):
    c = c_ref[...]
    h = (c * jax.nn.sigmoid(c)).astype(BF16)
    o_ref[0] = _dot(h, w_ref[0].astype(BF16)) + b_ref[0]


def _mod_call(c_all, w_ada, b_ada):
    n = c_all.shape[0]
    return pl.pallas_call(
        _mod_kernel,
        out_shape=jax.ShapeDtypeStruct((DEPTH, n, N_MOD * D_MODEL), F32),
        grid=(DEPTH, N_MOD),
        in_specs=[
            pl.BlockSpec((n, D_MODEL), lambda l, j: (0, 0)),
            pl.BlockSpec((1, D_MODEL, D_MODEL), lambda l, j: (l, 0, j)),
            pl.BlockSpec((1, 1, D_MODEL), lambda l, j: (l, 0, j)),
        ],
        out_specs=pl.BlockSpec((1, n, D_MODEL), lambda l, j: (l, 0, j)),
        compiler_params=pltpu.CompilerParams(
            dimension_semantics=("arbitrary", "arbitrary"), vmem_limit_bytes=VMEM_LIMIT),
        name="adaln_mod",
    )(c_all, w_ada, b_ada.reshape(DEPTH, 1, N_MOD * D_MODEL))


def _ffn_kernel(x_ref, sh_ref, sc_ref, g_ref, wg_ref, wu_ref, wd_ref, o_ref, *, parts):
    s, t, d = x_ref.shape
    along_t = s == 1
    ps, pt = (s, t // parts) if along_t else (s // parts, t)

    def piece(ref, p):
        if along_t:
            return ref[:, pl.ds(p * pt, pt), :] if ref.shape[1] == t else ref[...]
        return ref[pl.ds(p * ps, ps)]

    def modulated(p):
        h = _rms(piece(x_ref, p)) * (1.0 + piece(sc_ref, p)) + piece(sh_ref, p)
        return h.reshape(ps * pt, d).astype(BF16)

    h2 = modulated(0)
    g_next = _dot(h2, wg_ref[...])
    u_next = _dot(h2, wu_ref[...])
    for p in range(parts):
        g, u = g_next, u_next
        if p + 1 < parts:
            h2 = modulated(p + 1)
            g_next = _dot(h2, wg_ref[...])
        a = (g * jax.nn.sigmoid(g) * u).astype(BF16)
        y = _dot(a, wd_ref[...]).reshape(ps, pt, d)
        if along_t:
            o_ref[:, pl.ds(p * pt, pt), :] = piece(x_ref, p) + (FFN_RES * piece(g_ref, p)) * y
        else:
            o_ref[pl.ds(p * ps, ps)] = piece(x_ref, p) + (FFN_RES * piece(g_ref, p)) * y
        if p + 1 < parts:
            u_next = _dot(h2, wu_ref[...])


def _weight_spec(lw, key):
    if key == "ones":
        return _resident(lw[key].shape)
    return _resident_layer(lw[key].shape, lw["layer"])


def _mod_specs(layer, seqs, first_chunk, index, rows=1):
    def spec(chunk):
        return pl.BlockSpec((None, seqs, rows, D_MODEL), lambda *g: (layer, index(*g), 0, chunk))
    return [spec(first_chunk + k) for k in range(3)]


def _ffn_call(x, mod, first_chunk, weights, layer, seqs, rows, name):
    wg, wu, wd = weights
    nseq, t, d = x.shape
    parts = seqs * rows // FFN_ROWS
    grid = (nseq // seqs, t // rows)
    xspec = pl.BlockSpec((seqs, rows, d), lambda i, j: (i, j, 0))
    mspecs = _mod_specs(layer, seqs, first_chunk, lambda i, j: i)
    return pl.pallas_call(
        functools.partial(_ffn_kernel, parts=parts),
        out_shape=jax.ShapeDtypeStruct(x.shape, F32),
        grid=grid,
        in_specs=[xspec, *mspecs, *[_resident_layer(w.shape, layer) for w in weights]],
        out_specs=xspec,
        compiler_params=pltpu.CompilerParams(
            dimension_semantics=("arbitrary", "arbitrary"), vmem_limit_bytes=VMEM_LIMIT),
        name=name,
    )(x, mod, mod, mod, wg, wu, wd)


def _head_mean_square(qk, avg_ref):
    parts = []
    for c0 in range(0, qk.shape[1], MXU_DIM):
        w = min(MXU_DIM, qk.shape[1] - c0)
        sq = qk[:, c0:c0 + w] * qk[:, c0:c0 + w]
        parts.append(_dot(sq.astype(BF16), avg_ref[:w, :w]))
    return jnp.concatenate(parts, axis=-1)


def _modulated(x2, sh, sc):
    return (_rms(x2) * (1.0 + sc) + sh).astype(BF16)


def _split_projection(z, ones_ref, gain):
    qk = z[:, :QK_W]
    qkn = qk * lax.rsqrt(_head_mean_square(qk, ones_ref) + RMS_EPS) * gain
    q = qkn[:, :ATTN_W]
    k = qkn[:, ATTN_W:QK_W]
    v = z[:, QK_W:QK_W + KV_W]
    xr = z[:, QK_W + KV_W:QK_W + KV_W + LRU_W]
    gr = z[:, QK_W + KV_W + LRU_W:]
    return q, k, v, xr, gr


def _shift_rows(x, k, t, fill):
    return jnp.where(t >= k, pltpu.roll(x, k, 0), fill)


def _conv(delayed, taps, bias):
    y = bias + delayed[3] * taps[0]
    y = y + delayed[2] * taps[1]
    y = y + delayed[1] * taps[2]
    return y + delayed[0] * taps[3]


def _lru_gates(xc, wgate_ref, bias):
    xb = xc.astype(BF16)
    cols = [_dot(xb[:, t * MXU_DIM:(t + 1) * MXU_DIM], wgate_ref[gate, t])
            for gate in range(2) for t in range(LRU_W // MXU_DIM)]
    return jnp.concatenate(cols, axis=-1) + bias


def _softplus_neg(lam):
    return jnp.maximum(-lam, 0.0) + jnp.log1p(jnp.exp(-jnp.abs(lam)))


def _lru_inputs(xc, gates, softplus):
    r = jax.nn.sigmoid(gates[:, :LRU_W])
    gi = jax.nn.sigmoid(gates[:, LRU_W:])
    log_a = (-RG_C * r) * softplus
    a = jnp.exp(log_a)
    w = -jnp.tanh(log_a) * (a * a + 1.0)
    u = jnp.where(w > 0.0, w * lax.rsqrt(w), 0.0) * (gi * xc)
    return a, u


def _log_scan(a, u, t, period):
    s = 1
    while s < period:
        a_prev = _shift_rows(a, s, t, 1.0)
        u_prev = _shift_rows(u, s, t, 0.0)
        u = u + a * u_prev
        a = a * a_prev
        s *= 2
    return a, u


def _slab_load(ref, start, n, stride=1):
    rows = pl.ds(start, n) if stride == 1 else pl.ds(start, n, stride=stride)
    return jnp.concatenate([ref[s, rows, :] for s in range(ref.shape[0])], axis=-1)


def _slab_store(ref, start, n, val, stride=1):
    rows = pl.ds(start, n) if stride == 1 else pl.ds(start, n, stride=stride)
    for s in range(ref.shape[0]):
        ref[s, rows, :] = val[:, s * LANES:(s + 1) * LANES]


def _blocked_scan(a_scr, u_scr, base, n, h_init):
    if n <= 2 * SUBLANES:
        t = lax.broadcasted_iota(jnp.int32, (n, 1), 0)
        a, u = _log_scan(_slab_load(a_scr, base, n), _slab_load(u_scr, base, n), t, n)
        return a * h_init + u
    m = n // SCAN_GROUP
    a_loc, h_loc = [], []
    for r in range(SCAN_GROUP):
        a_r = _slab_load(a_scr, base + r, m, SCAN_GROUP)
        u_r = _slab_load(u_scr, base + r, m, SCAN_GROUP)
        h_loc.append(u_r if r == 0 else a_r * h_loc[-1] + u_r)
        a_loc.append(a_r if r == 0 else a_r * a_loc[-1])
    _slab_store(a_scr, base + n, m, a_loc[-1])
    _slab_store(u_scr, base + n, m, h_loc[-1])
    ends = _blocked_scan(a_scr, u_scr, base + n, m, h_init)
    g = lax.broadcasted_iota(jnp.int32, (m, 1), 0)
    carry = jnp.where(g >= 1, pltpu.roll(ends, 1, 0), h_init)
    for r in range(SCAN_GROUP):
        _slab_store(u_scr, base + r, m, a_loc[r] * carry + h_loc[r], SCAN_GROUP)
    return _slab_load(u_scr, base, n)


def _scan_rows(n):
    return n if n <= 2 * SUBLANES else n + _scan_rows(n // SCAN_GROUP)


def _gelu_tanh(x):
    return 0.5 * x * (1.0 + jnp.tanh(np.sqrt(2.0 / np.pi) * (x + 0.044715 * (x * x * x))))


def _merged_heads(attn, lru, beta):
    return (jnp.concatenate([_rms(attn), _rms(lru)], axis=-1) * beta).astype(BF16)


def _softmax_with_sink(s, sink):
    m = jnp.maximum(jnp.max(s, axis=-1, keepdims=True), sink)
    p = jnp.exp2(s - m)
    den = jnp.sum(p, axis=-1, keepdims=True) + jnp.exp2(sink - m)
    return p, den


def _half_variants(x, low):
    xs = pltpu.roll(x, HALF, x.ndim - 1)
    zero = jnp.zeros_like(x)
    return (
        (jnp.where(low, x, zero).astype(BF16), jnp.where(low, zero, xs).astype(BF16)),
        (jnp.where(low, xs, zero).astype(BF16), jnp.where(low, zero, x).astype(BF16)),
    )


def _mix_prompt_kernel(sinks_ref, x_ref, sh_ref, sc_ref, g_ref, w_in_ref, ones_ref, gain_ref, bias_ref,
                       convw_ref, convb_ref, wgate_ref, bgate_ref, lam_ref, beta_ref, wout_ref,
                       y_ref, k_ref, v_ref, conv_ref, h_ref,
                       kprev, vprev, hcar, xr_scr, a_scr, u_scr, *, layer):
    n = pl.program_id(1)

    @pl.when(n == 0)
    def _():
        kprev[...] = jnp.zeros_like(kprev)
        vprev[...] = jnp.zeros_like(vprev)
        hcar[...] = jnp.zeros_like(hcar)
        xr_scr[:, 0:SUBLANES, :] = jnp.zeros((xr_scr.shape[0], SUBLANES, LANES), F32)

    subs = x_ref.shape[1] // SUB_ROWS
    low = lax.broadcasted_iota(jnp.int32, (Q_BLOCK + SUB_ROWS, LANES), 1) < HALF
    second_head = lax.broadcasted_iota(jnp.int32, (2 * Q_BLOCK, 1), 0) >= Q_BLOCK
    carry = dict(k=kprev[...], v=vprev[...], h=hcar[...])
    st = [dict() for _ in range(subs)]

    def rep(r8):
        return jnp.broadcast_to(r8[None], (SUB_ROWS // SUBLANES,) + r8.shape).reshape(SUB_ROWS, r8.shape[-1])

    shift, scale1, gate = rep(sh_ref[0]), rep(1.0 + sc_ref[0]), rep(g_ref[0])
    gain, beta, gate_bias = rep(gain_ref[...]), rep(beta_ref[...]), rep(bgate_ref[...])
    taps, conv_bias = [rep(convw_ref[i]) for i in range(CONV_WIDTH)], rep(convb_ref[...])
    softplus = rep(_softplus_neg(lam_ref[...]))

    def rows(j):
        return pl.ds(j * SUB_ROWS, SUB_ROWS)

    def modulate(j):
        st[j]["h"] = (_rms(x_ref[0, rows(j), :]) * scale1 + shift).astype(BF16)
        st[j]["z"] = []

    def project(j, tiles):
        for c in tiles:
            st[j]["z"].append(_dot(st[j]["h"], w_in_ref[:, c * MXU_DIM:(c + 1) * MXU_DIM]))

    def split(j):
        z = jnp.concatenate(st[j].pop("z"), axis=-1)
        q, k, v, xr, gr = _split_projection(z, ones_ref, gain)
        st[j].update(q=q, k=k, v=v, xr=xr, gr=gr)

    pairs = [(kv, half) for kv in range(N_KV_HEADS) for half in range(2)]

    def scores(j):
        s = st[j]
        kvar = _half_variants(jnp.concatenate([carry["k"], s["k"]], axis=0), low)
        s["vvar"] = _half_variants(jnp.concatenate([carry["v"], s["v"]], axis=0), low)
        carry.update(k=s["k"][SUB_ROWS - Q_BLOCK:], v=s["v"][SUB_ROWS - Q_BLOCK:])
        s["s"] = []
        for qb in range(SUB_ROWS // Q_BLOCK):
            bias_row = jnp.minimum(n, 1) if (j == 0 and qb == 0) else 1
            q = s["q"][qb * Q_BLOCK:(qb + 1) * Q_BLOCK]
            for kv, half in pairs:
                stacked = jnp.concatenate([q[:, (2 * kv) * LANES:(2 * kv + 1) * LANES],
                                           q[:, (2 * kv + 1) * LANES:(2 * kv + 2) * LANES]], axis=0)
                keys = kvar[kv][half][qb * Q_BLOCK:(qb + 2) * Q_BLOCK]
                s["s"].append(_dot_nt(stacked.astype(BF16), keys) + bias_ref[bias_row, 2 * kv + half])

    def softmax(j):
        s = st[j]
        s["p"] = []
        for i, sc in enumerate(s.pop("s")):
            kv, half = pairs[i % len(pairs)]
            first, second = GROUP * kv + half, GROUP * kv + half + 2
            sink = jnp.where(second_head, sinks_ref[layer, second], sinks_ref[layer, first])
            s["p"].append(_softmax_with_sink(sc, sink))

    def attend(j):
        s = st[j]
        vvar = s.pop("vvar")
        blocks = []
        for qb in range(SUB_ROWS // Q_BLOCK):
            out = {}
            for kv, half in pairs:
                p, den = s["p"][qb * len(pairs) + 2 * kv + half]
                out[kv, half] = _dot(p.astype(BF16), vvar[kv][half][qb * Q_BLOCK:(qb + 2) * Q_BLOCK]) / den
            tiles = []
            for kv in range(N_KV_HEADS):
                both = out[kv, 0] + out[kv, 1]
                tiles += [both[:Q_BLOCK], both[Q_BLOCK:]]
            blocks.append(jnp.concatenate(tiles, axis=-1))
        s.pop("p")
        s["attn"] = jnp.concatenate(blocks, axis=0)

    def conv(j):
        first = SUBLANES + j * SUB_ROWS
        _slab_store(xr_scr, first, SUB_ROWS, st[j]["xr"])
        st[j]["xc"] = _conv([_slab_load(xr_scr, first - kk, SUB_ROWS) for kk in range(CONV_WIDTH)],
                            taps, conv_bias)

    def gates(j):
        st[j]["gates"] = _lru_gates(st[j]["xc"], wgate_ref, gate_bias)

    def recur(j):
        s = st[j]
        a, u = _lru_inputs(s.pop("xc"), s.pop("gates"), softplus)
        _slab_store(a_scr.at[j], 0, SUB_ROWS, a)
        _slab_store(u_scr.at[j], 0, SUB_ROWS, u)
        hs = _blocked_scan(a_scr.at[j], u_scr.at[j], 0, SUB_ROWS, carry["h"])
        carry.update(h=hs[SUB_ROWS - 1:, :])
        s["lru"] = hs * _gelu_tanh(s.pop("gr"))
        if j == subs - 1:
            h_ref[0] = hs[SUB_ROWS - SUBLANES:, :]

    def merge(j):
        st[j]["merged"] = _merged_heads(st[j].pop("attn"), st[j].pop("lru"), beta)

    def output(j):
        merged = st[j].pop("merged")
        for c in range(D_MODEL // MXU_DIM):
            cols = slice(c * MXU_DIM, (c + 1) * MXU_DIM)
            y_ref[0, rows(j), cols] = x_ref[0, rows(j), cols] + gate[:, cols] * _dot(merged, wout_ref[c])

    first_tiles = range(0, 6)
    last_tiles = range(6, IN_COLS // MXU_DIM)
    modulate(0)
    project(0, first_tiles)
    project(0, last_tiles)
    for j in range(subs):
        nxt = j + 1 < subs
        split(j)
        if j > 0:
            merge(j - 1)
            output(j - 1)
        scores(j)
        conv(j)
        if nxt:
            modulate(j + 1)
        if nxt:
            project(j + 1, first_tiles)
        softmax(j)
        attend(j)
        gates(j)
        if nxt:
            project(j + 1, last_tiles)
        recur(j)
    merge(subs - 1)
    output(subs - 1)

    last = st[subs - 1]
    k_ref[0] = carry["k"]
    v_ref[0] = carry["v"]
    conv_ref[0] = last["xr"][SUB_ROWS - SUBLANES:, :]
    _slab_store(xr_scr, 0, SUBLANES, last["xr"][SUB_ROWS - SUBLANES:, :])
    kprev[...] = carry["k"]
    vprev[...] = carry["v"]
    hcar[...] = carry["h"]


def _prompt_bias():
    i = np.arange(Q_BLOCK)[:, None]
    j = np.arange(2 * Q_BLOCK)[None, :]
    dist = Q_BLOCK + i - j
    band = (dist >= 0) & (dist <= WINDOW)
    slopes = np.asarray([2.0 ** (-8.0 * (h + 1) / N_HEADS) for h in range(N_HEADS)], np.float32)
    alibi = -(slopes[:, None, None] * dist[None].astype(np.float64)) * LOG2E
    general = np.where(band[None], alibi, np.float32(NEG_INF))
    first = np.where((band & (j >= Q_BLOCK))[None], alibi, np.float32(NEG_INF))
    per_head = np.stack([first, general]).astype(np.float32)
    pairs = [(GROUP * kv + half, GROUP * kv + half + 2) for kv in range(N_KV_HEADS) for half in range(2)]
    return np.stack([np.concatenate([per_head[:, a], per_head[:, b]], axis=1) for a, b in pairs], axis=1)


def _mix_prompt_call(x, mod, lw):
    b, t, d = x.shape
    nb = t // MIX_ROWS
    subs = MIX_ROWS // SUB_ROWS
    xspec = pl.BlockSpec((1, MIX_ROWS, d), lambda i, j: (i, j, 0))
    mspecs = _mod_specs(lw["layer"], 1, 3, lambda i, j: i, rows=SUBLANES)
    bias = jnp.asarray(_prompt_bias())
    last = lambda shape: pl.BlockSpec(shape, lambda i, j: (i, 0, 0))
    head = ["w_in", "ones", "gain"]
    tail = ["conv_w", "conv_b", "w_gate", "b_gate", "lam", "beta", "w_out"]
    outs = pl.pallas_call(
        functools.partial(_mix_prompt_kernel, layer=lw["layer"]),
        out_shape=(
            jax.ShapeDtypeStruct(x.shape, F32),
            jax.ShapeDtypeStruct((b, WINDOW, KV_W), F32),
            jax.ShapeDtypeStruct((b, WINDOW, KV_W), F32),
            jax.ShapeDtypeStruct((b, SUBLANES, LRU_W), F32),
            jax.ShapeDtypeStruct((b, SUBLANES, LRU_W), F32),
        ),
        grid=(b, nb),
        in_specs=[
            pl.BlockSpec(memory_space=pltpu.SMEM),
            xspec, *mspecs,
            *[_weight_spec(lw, key) for key in head],
            _resident(bias.shape),
            *[_weight_spec(lw, key) for key in tail],
        ],
        out_specs=(xspec, last((1, WINDOW, KV_W)), last((1, WINDOW, KV_W)),
                   last((1, SUBLANES, LRU_W)), last((1, SUBLANES, LRU_W))),
        scratch_shapes=[
            pltpu.VMEM((Q_BLOCK, KV_W), F32), pltpu.VMEM((Q_BLOCK, KV_W), F32),
            pltpu.VMEM((1, LRU_W), F32),
            pltpu.VMEM((LRU_W // LANES, SUBLANES + MIX_ROWS, LANES), F32),
            pltpu.VMEM((subs, LRU_W // LANES, _scan_rows(SUB_ROWS), LANES), F32),
            pltpu.VMEM((subs, LRU_W // LANES, _scan_rows(SUB_ROWS), LANES), F32),
        ],
        compiler_params=pltpu.CompilerParams(
            dimension_semantics=("arbitrary", "arbitrary"), vmem_limit_bytes=VMEM_LIMIT),
        name="mix_prompt",
    )(lw["sinks"], x, mod, mod, mod, *[lw[key] for key in head], bias, *[lw[key] for key in tail])
    return outs


def _mix_sample_kernel(x_ref, sh_ref, sc_ref, g_ref, ck_ref, cv_ref, cs_ref, h0_ref,
                       w_in_ref, ones_ref, gain_ref, bias_ref, sink_ref,
                       convw_ref, convb_ref, wgate_ref, bgate_ref, lam_ref, beta_ref, wout_ref,
                       y_ref, k_ref, v_ref, conv_ref, h_ref):
    sb, tq, d = x_ref.shape
    rows = sb * tq

    def flat(a):
        return jnp.broadcast_to(a, (sb, tq, a.shape[-1])).reshape(rows, a.shape[-1])

    x2 = x_ref[...].reshape(rows, d)
    z = _dot(_modulated(x2, flat(sh_ref[...]), flat(sc_ref[...])), w_in_ref[...])
    q, k, v, xr, gr = _split_projection(z, ones_ref, gain_ref[0:1])
    k3 = k.reshape(sb, tq, KV_W)
    v3 = v.reshape(sb, tq, KV_W)
    ck = ck_ref[...]
    cv = cv_ref[...]
    k_ref[...] = jnp.concatenate([ck[:, tq:, :], k3], axis=1)
    v_ref[...] = jnp.concatenate([cv[:, tq:, :], v3], axis=1)

    pad = jnp.zeros((sb, WINDOW - tq, KV_W), F32)
    kall = jnp.concatenate([ck, k3, pad], axis=1)
    vall = jnp.concatenate([cv, v3, pad], axis=1)
    low2 = lax.broadcasted_iota(jnp.int32, (rows, LANES), 1) < HALF
    zero2 = jnp.zeros((rows, LANES), F32)
    pieces = []
    for head in range(N_HEADS):
        tile, parity, kv = head // 2, head % 2, head // GROUP
        qt = q[:, tile * LANES:(tile + 1) * LANES]
        src = qt if parity == kv else pltpu.roll(qt, HALF, 1)
        piece = jnp.where(low2, src, zero2) if kv == 0 else jnp.where(low2, zero2, src)
        pieces.append(piece.reshape(sb, tq, LANES))
    qrows = jnp.concatenate(pieces, axis=1).astype(BF16)
    s = jnp.einsum("snc,sjc->snj", qrows, kall.astype(BF16), preferred_element_type=F32)
    s = s + bias_ref[...]
    p, den = _softmax_with_sink(s, sink_ref[...])
    o = jnp.einsum("snj,sjc->snc", p.astype(BF16), vall.astype(BF16), preferred_element_type=F32) / den
    tiles = []
    for tile in range(ATTN_W // LANES):
        kv = (2 * tile) // GROUP
        oe = o[:, (2 * tile) * tq:(2 * tile + 1) * tq, :].reshape(rows, LANES)
        oo = o[:, (2 * tile + 1) * tq:(2 * tile + 2) * tq, :].reshape(rows, LANES)
        if kv == 0:
            tiles.append(jnp.where(low2, oe, pltpu.roll(oo, HALF, 1)))
        else:
            tiles.append(jnp.where(low2, pltpu.roll(oe, HALF, 1), oo))
    attn = jnp.concatenate(tiles, axis=-1)

    t = lax.broadcasted_iota(jnp.int32, (sb, tq, 1), 1).reshape(rows, 1)
    state = cs_ref[...].reshape(rows, LRU_W)
    prev = {3: state, 2: pltpu.roll(state, rows - 1, 0), 1: pltpu.roll(state, rows - 2, 0)}
    delayed = [xr] + [_shift_rows(xr, kk, t, prev[kk]) for kk in (1, 2, 3)]
    xc = _conv(delayed, [convw_ref[i, 0:1] for i in range(CONV_WIDTH)], convb_ref[0:1])
    conv_ref[...] = xr.reshape(sb, tq, LRU_W)
    gates = _lru_gates(xc, wgate_ref, bgate_ref[0:1])
    a, u = _log_scan(*_lru_inputs(xc, gates, _softplus_neg(lam_ref[0:1])), t, tq)
    hs = a * flat(h0_ref[...]) + u
    h_ref[...] = hs.reshape(sb, tq, LRU_W)

    merged = _merged_heads(attn, hs * _gelu_tanh(gr), beta_ref[0:1])
    y = jnp.concatenate([_dot(merged, wout_ref[c]) for c in range(d // MXU_DIM)], axis=-1)
    y_ref[...] = (x2 + flat(g_ref[...]) * y).reshape(sb, tq, d)


def _sample_bias(tq):
    i = np.arange(tq)[:, None]
    j = np.arange(2 * WINDOW)[None, :]
    dist = WINDOW + i - j
    ok = (dist >= 0) & (dist <= WINDOW) & (j < WINDOW + tq)
    slopes = np.asarray([2.0 ** (-8.0 * (h + 1) / N_HEADS) for h in range(N_HEADS)], np.float32)
    alibi = -(slopes[:, None, None] * dist[None].astype(np.float64)) * LOG2E
    return np.where(ok[None], alibi, np.float32(NEG_INF)).reshape(N_HEADS * tq, 2 * WINDOW).astype(np.float32)


def _mix_sample_call(x, mod, k_windows, v_windows, conv_states, lru_states, lw):
    nseq, tq, d = x.shape
    sb = SAMPLE_SEQS
    layer = lw["layer"]
    seq_block = lambda shape: pl.BlockSpec(shape, lambda i: (i, 0, 0))
    layer_seq_block = lambda shape: pl.BlockSpec((None,) + shape, lambda i: (layer, i, 0, 0))
    first_block = layer * (nseq // sb)
    window_block = pl.BlockSpec((sb, WINDOW, KV_W), lambda i: (first_block + i, 0, 0))
    xspec = seq_block((sb, tq, d))
    mspecs = _mod_specs(layer, sb, 3, lambda i: i)
    bias = jnp.asarray(_sample_bias(tq))
    head = ["w_in", "ones", "gain"]
    tail = ["sink_col", "conv_w", "conv_b", "w_gate", "b_gate", "lam", "beta", "w_out"]
    return pl.pallas_call(
        _mix_sample_kernel,
        out_shape=(
            jax.ShapeDtypeStruct(x.shape, F32),
            jax.ShapeDtypeStruct(k_windows.shape, F32),
            jax.ShapeDtypeStruct(v_windows.shape, F32),
            jax.ShapeDtypeStruct((nseq, tq, LRU_W), F32),
            jax.ShapeDtypeStruct((nseq, tq, LRU_W), F32),
        ),
        grid=(nseq // sb,),
        in_specs=[
            xspec, *mspecs,
            window_block, window_block,
            layer_seq_block((sb, tq, LRU_W)), layer_seq_block((sb, 1, LRU_W)),
            *[_weight_spec(lw, key) for key in head],
            _resident(bias.shape),
            *[_weight_spec(lw, key) for key in tail],
        ],
        out_specs=(xspec, window_block, window_block,
                   seq_block((sb, tq, LRU_W)), seq_block((sb, tq, LRU_W))),
        input_output_aliases={4: 1, 5: 2},
        compiler_params=pltpu.CompilerParams(
            dimension_semantics=("arbitrary",), vmem_limit_bytes=VMEM_LIMIT),
        name="mix_sample",
    )(x, mod, mod, mod, k_windows, v_windows, conv_states, lru_states,
      *[lw[key] for key in head], bias, *[lw[key] for key in tail])


def _diagonal_tiles(w):
    depth, n, c, _ = w.shape
    per_tile = MXU_DIM // c
    tiles = w.reshape(depth, n // per_tile, per_tile, c, c)
    eye = jnp.eye(per_tile, dtype=w.dtype)
    dense = tiles[:, :, :, :, None, :] * eye[None, None, :, None, :, None]
    return dense.reshape(depth, n // per_tile, MXU_DIM, MXU_DIM)


def _rows8(v):
    return jnp.broadcast_to(v[..., None, :], v.shape[:-1] + (SUBLANES, v.shape[-1]))


def _mixer_weights(tq, w_in_bf16, q_gain, k_gain, sinks, conv_w, conv_b, w_rg, b_rg, w_ig, b_ig, lru_lambda,
                   beta_attn, beta_lru, w_out_tiles):
    head_of = np.arange(MXU_DIM) // HEAD_DIM
    ones = jnp.asarray((head_of[:, None] == head_of[None, :]) * (1.0 / HEAD_DIM), BF16)
    gain = jnp.concatenate([jnp.tile(q_gain * (HEAD_DIM ** -0.5 * LOG2E), (1, N_HEADS)),
                            jnp.tile(k_gain, (1, N_KV_HEADS))], axis=1)
    return dict(
        w_in=w_in_bf16,
        ones=ones,
        gain=_rows8(gain),
        sinks=sinks * LOG2E,
        sink_col=jnp.repeat(sinks * LOG2E, tq, axis=1)[..., None],
        conv_w=_rows8(conv_w),
        conv_b=_rows8(conv_b),
        w_gate=jnp.stack([_diagonal_tiles(w_rg), _diagonal_tiles(w_ig)], axis=1).astype(BF16),
        b_gate=_rows8(jnp.concatenate([b_rg, b_ig], axis=1)),
        lam=_rows8(lru_lambda),
        beta=_rows8(jnp.concatenate([beta_attn, beta_lru], axis=1)),
        w_out=w_out_tiles,
    )


def kernel(x_prompt, x_sample, cache_k, cache_v, state_conv, state_lru, c_prompt, c_sample, w_ada, b_ada, w1_gate, w1_up, w1_down, w_in, q_gain, k_gain, sinks, conv_w, conv_b, w_rg, b_rg, w_ig, b_ig, lru_lambda, beta_attn, beta_lru, w_out, w2_gate, w2_up, w2_down):
    nb = x_prompt.shape[0]
    ns, tq, _ = x_sample.shape
    mod = _mod_call(jnp.concatenate([c_prompt, c_sample], axis=0), w_ada, b_ada)
    mod_p = mod[:, :nb].reshape(DEPTH, nb, 1, N_MOD * D_MODEL)
    mod_p8 = _rows8(mod[:, :nb])
    mod_s = mod[:, nb:].reshape(DEPTH, ns, 1, N_MOD * D_MODEL)

    w1g, w1u, w2g, w2u = _cast_call([w1_gate, w1_up, w2_gate, w2_up], CAST_ROWS)
    w1d, w2d = _cast_call([w1_down, w2_down], D_FF // (D_MODEL // CAST_ROWS))
    (w_in_bf16,) = _cast_call([w_in], CAST_ROWS)
    w_out_tiles = _cast_column_tiles_call(w_out, CAST_ROWS)
    w1, w2 = (w1g, w1u, w1d), (w2g, w2u, w2d)

    k_windows = cache_k.reshape(DEPTH * ns, WINDOW, KV_W)
    v_windows = cache_v.reshape(DEPTH * ns, WINDOW, KV_W)
    conv_states = jnp.pad(state_conv, ((0, 0), (0, 0), (0, tq - (CONV_WIDTH - 1)), (0, 0)))
    lru_states = state_lru.reshape(DEPTH, ns, 1, LRU_W)
    weights = _mixer_weights(tq, w_in_bf16, q_gain, k_gain, sinks, conv_w, conv_b, w_rg, b_rg, w_ig, b_ig,
                             lru_lambda, beta_attn, beta_lru, w_out_tiles)
    yp, ys = x_prompt, x_sample
    outs_p, outs_s = [], []
    for l in range(DEPTH):
        lw = dict(weights, layer=l)

        yp = _ffn_call(yp, mod_p, 0, w1, l, 1, FFN_ROWS * FFN_PARTS, "ffn1_prompt")
        ys = _ffn_call(ys, mod_s, 0, w1, l, FFN_ROWS // tq, tq, "ffn1_sample")

        yp, kp, vp, cp, hp = _mix_prompt_call(yp, mod_p8, lw)
        ys, k_windows, v_windows, cs, hs = _mix_sample_call(
            ys, mod_s, k_windows, v_windows, conv_states, lru_states, lw)
        outs_p.append((kp.reshape(nb, WINDOW, N_KV_HEADS, HEAD_DIM), vp.reshape(nb, WINDOW, N_KV_HEADS, HEAD_DIM),
                       cp[:, SUBLANES - (CONV_WIDTH - 1):], hp[:, SUBLANES - 1]))
        outs_s.append((cs[:, tq - (CONV_WIDTH - 1):], hs[:, tq - 1]))

        yp = _ffn_call(yp, mod_p, 6, w2, l, 1, FFN_ROWS * FFN_PARTS, "ffn2_prompt")
        ys = _ffn_call(ys, mod_s, 6, w2, l, FFN_ROWS // tq, tq, "ffn2_sample")

    stack = lambda outs, k: jnp.stack([o[k] for o in outs])
    return (yp, ys,
            stack(outs_p, 0), stack(outs_p, 1), stack(outs_p, 2), stack(outs_p, 3),
            k_windows.reshape(cache_k.shape), v_windows.reshape(cache_v.shape),
            stack(outs_s, 0), stack(outs_s, 1))
```

```python
import functools

import numpy as np
import jax
import jax.numpy as jnp
from jax import lax
from jax.experimental import pallas as pl
from jax.experimental.pallas import tpu as pltpu

D_MODEL = 1024
DEPTH = 2
HEAD_DIM = 64
N_HEADS = 8
N_KV_HEADS = 2
GROUP = N_HEADS // N_KV_HEADS
ATTN_W = N_HEADS * HEAD_DIM
KV_W = N_KV_HEADS * HEAD_DIM
LRU_W = 512
N_LRU_BLOCKS = 8
LRU_BLOCK = LRU_W // N_LRU_BLOCKS
CONV_WIDTH = 4
RG_C = 8.0
WINDOW = 128
Q_BLOCK = 128
D_FF = 2816
N_MOD = 9
FFN_RES = 0.5
IN_COLS = ATTN_W + 2 * KV_W + 2 * LRU_W
QK_W = ATTN_W + KV_W
RMS_EPS = 1e-6
NEG_INF = -1e30
LOG2E = float(np.log2(np.e))

LANES = 128
SUBLANES = 8
HALF = LANES // 2
MXU_DIM = 256
SCAN_GROUP = 4
VMEM_LIMIT = 56 * 1024 * 1024

FFN_ROWS = 512
FFN_PARTS = 2
CAST_ROWS = 256
MIX_ROWS = 1024
SUB_ROWS = 256
SAMPLE_SEQS = 32

BF16 = jnp.bfloat16
F32 = jnp.float32


def _dot(a, b):
    return jnp.dot(a, b, preferred_element_type=F32)


def _dot_nt(a, b):
    return lax.dot_general(a, b, (((1,), (1,)), ((), ())), preferred_element_type=F32)


def _rms(x):
    return x * lax.rsqrt(jnp.mean(x * x, axis=-1, keepdims=True) + RMS_EPS)


def _resident(shape):
    nd = len(shape)
    return pl.BlockSpec(shape, lambda *_: (0,) * nd, pipeline_mode=pl.Buffered(1))


def _resident_layer(shape, layer):
    nd = len(shape)
    return pl.BlockSpec((None,) + tuple(shape[1:]), lambda *_: (layer,) + (0,) * (nd - 1),
                        pipeline_mode=pl.Buffered(1))


def _cast_kernel(*refs):
    n = len(refs) // 2
    for src, dst in zip(refs[:n], refs[n:]):
        dst[...] = src[...].astype(BF16)


def _cast_call(ws, rows):
    depth, r, c = ws[0].shape
    spec = pl.BlockSpec((1, rows, c), lambda l, i: (l, i, 0))
    return pl.pallas_call(
        _cast_kernel,
        out_shape=[jax.ShapeDtypeStruct(w.shape, BF16) for w in ws],
        grid=(depth, r // rows),
        in_specs=[spec] * len(ws),
        out_specs=[spec] * len(ws),
        compiler_params=pltpu.CompilerParams(
            dimension_semantics=("arbitrary", "arbitrary"), vmem_limit_bytes=VMEM_LIMIT),
        name="cast_weights",
    )(*ws)


def _cast_tiles_kernel(src, dst):
    for c in range(dst.shape[1]):
        dst[0, c] = src[0, :, c * MXU_DIM:(c + 1) * MXU_DIM].astype(BF16)


def _cast_column_tiles_call(w, rows):
    depth, k, n = w.shape
    return pl.pallas_call(
        _cast_tiles_kernel,
        out_shape=jax.ShapeDtypeStruct((depth, n // MXU_DIM, k, MXU_DIM), BF16),
        grid=(depth, k // rows),
        in_specs=[pl.BlockSpec((1, rows, n), lambda l, i: (l, i, 0))],
        out_specs=pl.BlockSpec((1, n // MXU_DIM, rows, MXU_DIM), lambda l, i: (l, 0, i, 0)),
        compiler_params=pltpu.CompilerParams(
            dimension_semantics=("arbitrary", "arbitrary"), vmem_limit_bytes=VMEM_LIMIT),
        name="cast_column_tiles",
    )(w)


def _mod_kernel(c_ref, w_ref, b_ref, o_ref):
    c = c_ref[...]
    h = (c * jax.nn.sigmoid(c)).astype(BF16)
    o_ref[0] = _dot(h, w_ref[0].astype(BF16)) + b_ref[0]


def _mod_call(c_all, w_ada, b_ada):
    n = c_all.shape[0]
    return pl.pallas_call(
        _mod_kernel,
        out_shape=jax.ShapeDtypeStruct((DEPTH, n, N_MOD * D_MODEL), F32),
        grid=(DEPTH, N_MOD),
        in_specs=[
            pl.BlockSpec((n, D_MODEL), lambda l, j: (0, 0)),
            pl.BlockSpec((1, D_MODEL, D_MODEL), lambda l, j: (l, 0, j)),
            pl.BlockSpec((1, 1, D_MODEL), lambda l, j: (l, 0, j)),
        ],
        out_specs=pl.BlockSpec((1, n, D_MODEL), lambda l, j: (l, 0, j)),
        compiler_params=pltpu.CompilerParams(
            dimension_semantics=("arbitrary", "arbitrary"), vmem_limit_bytes=VMEM_LIMIT),
        name="adaln_mod",
    )(c_all, w_ada, b_ada.reshape(DEPTH, 1, N_MOD * D_MODEL))


def _ffn_kernel(x_ref, sh_ref, sc_ref, g_ref, wg_ref, wu_ref, wd_ref, o_ref, *, parts):
    s, t, d = x_ref.shape
    along_t = s == 1
    ps, pt = (s, t // parts) if along_t else (s // parts, t)

    def piece(ref, p):
        if along_t:
            return ref[:, pl.ds(p * pt, pt), :] if ref.shape[1] == t else ref[...]
        return ref[pl.ds(p * ps, ps)]

    def modulated(p):
        h = _rms(piece(x_ref, p)) * (1.0 + piece(sc_ref, p)) + piece(sh_ref, p)
        return h.reshape(ps * pt, d).astype(BF16)

    h2 = modulated(0)
    g_next = _dot(h2, wg_ref[...])
    u_next = _dot(h2, wu_ref[...])
    for p in range(parts):
        g, u = g_next, u_next
        if p + 1 < parts:
            h2 = modulated(p + 1)
            g_next = _dot(h2, wg_ref[...])
        a = (g * jax.nn.sigmoid(g) * u).astype(BF16)
        y = _dot(a, wd_ref[...]).reshape(ps, pt, d)
        if along_t:
            o_ref[:, pl.ds(p * pt, pt), :] = piece(x_ref, p) + (FFN_RES * piece(g_ref, p)) * y
        else:
            o_ref[pl.ds(p * ps, ps)] = piece(x_ref, p) + (FFN_RES * piece(g_ref, p)) * y
        if p + 1 < parts:
            u_next = _dot(h2, wu_ref[...])


def _weight_spec(lw, key):
    if key == "ones":
        return _resident(lw[key].shape)
    return _resident_layer(lw[key].shape, lw["layer"])


def _mod_specs(layer, seqs, first_chunk, index, rows=1):
    def spec(chunk):
        return pl.BlockSpec((None, seqs, rows, D_MODEL), lambda *g: (layer, index(*g), 0, chunk))
    return [spec(first_chunk + k) for k in range(3)]


def _ffn_call(x, mod, first_chunk, weights, layer, seqs, rows, name):
    wg, wu, wd = weights
    nseq, t, d = x.shape
    parts = seqs * rows // FFN_ROWS
    grid = (nseq // seqs, t // rows)
    xspec = pl.BlockSpec((seqs, rows, d), lambda i, j: (i, j, 0))
    mspecs = _mod_specs(layer, seqs, first_chunk, lambda i, j: i)
    return pl.pallas_call(
        functools.partial(_ffn_kernel, parts=parts),
        out_shape=jax.ShapeDtypeStruct(x.shape, F32),
        grid=grid,
        in_specs=[xspec, *mspecs, *[_resident_layer(w.shape, layer) for w in weights]],
        out_specs=xspec,
        compiler_params=pltpu.CompilerParams(
            dimension_semantics=("arbitrary", "arbitrary"), vmem_limit_bytes=VMEM_LIMIT),
        name=name,
    )(x, mod, mod, mod, wg, wu, wd)


def _head_mean_square(qk, avg_ref):
    parts = []
    for c0 in range(0, qk.shape[1], MXU_DIM):
        w = min(MXU_DIM, qk.shape[1] - c0)
        sq = qk[:, c0:c0 + w] * qk[:, c0:c0 + w]
        parts.append(_dot(sq.astype(BF16), avg_ref[:w, :w]))
    return jnp.concatenate(parts, axis=-1)


def _modulated(x2, sh, sc):
    return (_rms(x2) * (1.0 + sc) + sh).astype(BF16)


def _split_projection(z, ones_ref, gain):
    qk = z[:, :QK_W]
    qkn = qk * lax.rsqrt(_head_mean_square(qk, ones_ref) + RMS_EPS) * gain
    q = qkn[:, :ATTN_W]
    k = qkn[:, ATTN_W:QK_W]
    v = z[:, QK_W:QK_W + KV_W]
    xr = z[:, QK_W + KV_W:QK_W + KV_W + LRU_W]
    gr = z[:, QK_W + KV_W + LRU_W:]
    return q, k, v, xr, gr


def _shift_rows(x, k, t, fill):
    return jnp.where(t >= k, pltpu.roll(x, k, 0), fill)


def _conv(delayed, taps, bias):
    y = bias + delayed[3] * taps[0]
    y = y + delayed[2] * taps[1]
    y = y + delayed[1] * taps[2]
    return y + delayed[0] * taps[3]


def _lru_gates(xc, wgate_ref, bias):
    xb = xc.astype(BF16)
    cols = [_dot(xb[:, t * MXU_DIM:(t + 1) * MXU_DIM], wgate_ref[gate, t])
            for gate in range(2) for t in range(LRU_W // MXU_DIM)]
    return jnp.concatenate(cols, axis=-1) + bias


def _softplus_neg(lam):
    return jnp.maximum(-lam, 0.0) + jnp.log1p(jnp.exp(-jnp.abs(lam)))


def _lru_inputs(xc, gates, softplus):
    r = jax.nn.sigmoid(gates[:, :LRU_W])
    gi = jax.nn.sigmoid(gates[:, LRU_W:])
    log_a = (-RG_C * r) * softplus
    a = jnp.exp(log_a)
    w = -jnp.tanh(log_a) * (a * a + 1.0)
    u = jnp.where(w > 0.0, w * lax.rsqrt(w), 0.0) * (gi * xc)
    return a, u


def _log_scan(a, u, t, period):
    s = 1
    while s < period:
        a_prev = _shift_rows(a, s, t, 1.0)
        u_prev = _shift_rows(u, s, t, 0.0)
        u = u + a * u_prev
        a = a * a_prev
        s *= 2
    return a, u


def _slab_load(ref, start, n, stride=1):
    rows = pl.ds(start, n) if stride == 1 else pl.ds(start, n, stride=stride)
    return jnp.concatenate([ref[s, rows, :] for s in range(ref.shape[0])], axis=-1)


def _slab_store(ref, start, n, val, stride=1):
    rows = pl.ds(start, n) if stride == 1 else pl.ds(start, n, stride=stride)
    for s in range(ref.shape[0]):
        ref[s, rows, :] = val[:, s * LANES:(s + 1) * LANES]


def _blocked_scan(a_scr, u_scr, base, n, h_init):
    if n <= 2 * SUBLANES:
        t = lax.broadcasted_iota(jnp.int32, (n, 1), 0)
        a, u = _log_scan(_slab_load(a_scr, base, n), _slab_load(u_scr, base, n), t, n)
        return a * h_init + u
    m = n // SCAN_GROUP
    a_loc, h_loc = [], []
    for r in range(SCAN_GROUP):
        a_r = _slab_load(a_scr, base + r, m, SCAN_GROUP)
        u_r = _slab_load(u_scr, base + r, m, SCAN_GROUP)
        h_loc.append(u_r if r == 0 else a_r * h_loc[-1] + u_r)
        a_loc.append(a_r if r == 0 else a_r * a_loc[-1])
    _slab_store(a_scr, base + n, m, a_loc[-1])
    _slab_store(u_scr, base + n, m, h_loc[-1])
    ends = _blocked_scan(a_scr, u_scr, base + n, m, h_init)
    g = lax.broadcasted_iota(jnp.int32, (m, 1), 0)
    carry = jnp.where(g >= 1, pltpu.roll(ends, 1, 0), h_init)
    for r in range(SCAN_GROUP):
        _slab_store(u_scr, base + r, m, a_loc[r] * carry + h_loc[r], SCAN_GROUP)
    return _slab_load(u_scr, base, n)


def _scan_rows(n):
    return n if n <= 2 * SUBLANES else n + _scan_rows(n // SCAN_GROUP)


def _gelu_tanh(x):
    return 0.5 * x * (1.0 + jnp.tanh(np.sqrt(2.0 / np.pi) * (x + 0.044715 * (x * x * x))))


def _merged_heads(attn, lru, beta):
    return (jnp.concatenate([_rms(attn), _rms(lru)], axis=-1) * beta).astype(BF16)


def _softmax_with_sink(s, sink):
    m = jnp.maximum(jnp.max(s, axis=-1, keepdims=True), sink)
    p = jnp.exp2(s - m)
    den = jnp.sum(p, axis=-1, keepdims=True) + jnp.exp2(sink - m)
    return p, den


def _half_variants(x, low):
    xs = pltpu.roll(x, HALF, x.ndim - 1)
    zero = jnp.zeros_like(x)
    return (
        (jnp.where(low, x, zero).astype(BF16), jnp.where(low, zero, xs).astype(BF16)),
        (jnp.where(low, xs, zero).astype(BF16), jnp.where(low, zero, x).astype(BF16)),
    )


def _mix_prompt_kernel(sinks_ref, x_ref, sh_ref, sc_ref, g_ref, w_in_ref, ones_ref, gain_ref, bias_ref,
                       convw_ref, convb_ref, wgate_ref, bgate_ref, lam_ref, beta_ref, wout_ref,
                       y_ref, k_ref, v_ref, conv_ref, h_ref,
                       kprev, vprev, hcar, xr_scr, a_scr, u_scr, *, layer):
    n = pl.program_id(1)

    @pl.when(n == 0)
    def _():
        kprev[...] = jnp.zeros_like(kprev)
        vprev[...] = jnp.zeros_like(vprev)
        hcar[...] = jnp.zeros_like(hcar)
        xr_scr[:, 0:SUBLANES, :] = jnp.zeros((xr_scr.shape[0], SUBLANES, LANES), F32)

    subs = x_ref.shape[1] // SUB_ROWS
    low = lax.broadcasted_iota(jnp.int32, (Q_BLOCK + SUB_ROWS, LANES), 1) < HALF
    second_head = lax.broadcasted_iota(jnp.int32, (2 * Q_BLOCK, 1), 0) >= Q_BLOCK
    carry = dict(k=kprev[...], v=vprev[...], h=hcar[...])
    st = [dict() for _ in range(subs)]

    def rep(r8):
        return jnp.broadcast_to(r8[None], (SUB_ROWS // SUBLANES,) + r8.shape).reshape(SUB_ROWS, r8.shape[-1])

    shift, scale1, gate = rep(sh_ref[0]), rep(1.0 + sc_ref[0]), rep(g_ref[0])
    gain, beta, gate_bias = rep(gain_ref[...]), rep(beta_ref[...]), rep(bgate_ref[...])
    taps, conv_bias = [rep(convw_ref[i]) for i in range(CONV_WIDTH)], rep(convb_ref[...])
    softplus = rep(_softplus_neg(lam_ref[...]))

    def rows(j):
        return pl.ds(j * SUB_ROWS, SUB_ROWS)

    def modulate(j):
        st[j]["h"] = (_rms(x_ref[0, rows(j), :]) * scale1 + shift).astype(BF16)
        st[j]["z"] = []

    def project(j, tiles):
        for c in tiles:
            st[j]["z"].append(_dot(st[j]["h"], w_in_ref[:, c * MXU_DIM:(c + 1) * MXU_DIM]))

    def split(j):
        z = jnp.concatenate(st[j].pop("z"), axis=-1)
        q, k, v, xr, gr = _split_projection(z, ones_ref, gain)
        st[j].update(q=q, k=k, v=v, xr=xr, gr=gr)

    pairs = [(kv, half) for kv in range(N_KV_HEADS) for half in range(2)]

    def scores(j):
        s = st[j]
        kvar = _half_variants(jnp.concatenate([carry["k"], s["k"]], axis=0), low)
        s["vvar"] = _half_variants(jnp.concatenate([carry["v"], s["v"]], axis=0), low)
        carry.update(k=s["k"][SUB_ROWS - Q_BLOCK:], v=s["v"][SUB_ROWS - Q_BLOCK:])
        s["s"] = []
        for qb in range(SUB_ROWS // Q_BLOCK):
            bias_row = jnp.minimum(n, 1) if (j == 0 and qb == 0) else 1
            q = s["q"][qb * Q_BLOCK:(qb + 1) * Q_BLOCK]
            for kv, half in pairs:
                stacked = jnp.concatenate([q[:, (2 * kv) * LANES:(2 * kv + 1) * LANES],
                                           q[:, (2 * kv + 1) * LANES:(2 * kv + 2) * LANES]], axis=0)
                keys = kvar[kv][half][qb * Q_BLOCK:(qb + 2) * Q_BLOCK]
                s["s"].append(_dot_nt(stacked.astype(BF16), keys) + bias_ref[bias_row, 2 * kv + half])

    def softmax(j):
        s = st[j]
        s["p"] = []
        for i, sc in enumerate(s.pop("s")):
            kv, half = pairs[i % len(pairs)]
            first, second = GROUP * kv + half, GROUP * kv + half + 2
            sink = jnp.where(second_head, sinks_ref[layer, second], sinks_ref[layer, first])
            s["p"].append(_softmax_with_sink(sc, sink))

    def attend(j):
        s = st[j]
        vvar = s.pop("vvar")
        blocks = []
        for qb in range(SUB_ROWS // Q_BLOCK):
            out = {}
            for kv, half in pairs:
                p, den = s["p"][qb * len(pairs) + 2 * kv + half]
                out[kv, half] = _dot(p.astype(BF16), vvar[kv][half][qb * Q_BLOCK:(qb + 2) * Q_BLOCK]) / den
            tiles = []
            for kv in range(N_KV_HEADS):
                both = out[kv, 0] + out[kv, 1]
                tiles += [both[:Q_BLOCK], both[Q_BLOCK:]]
            blocks.append(jnp.concatenate(tiles, axis=-1))
        s.pop("p")
        s["attn"] = jnp.concatenate(blocks, axis=0)

    def conv(j):
        first = SUBLANES + j * SUB_ROWS
        _slab_store(xr_scr, first, SUB_ROWS, st[j]["xr"])
        st[j]["xc"] = _conv([_slab_load(xr_scr, first - kk, SUB_ROWS) for kk in range(CONV_WIDTH)],
                            taps, conv_bias)

    def gates(j):
        st[j]["gates"] = _lru_gates(st[j]["xc"], wgate_ref, gate_bias)

    def recur(j):
        s = st[j]
        a, u = _lru_inputs(s.pop("xc"), s.pop("gates"), softplus)
        _slab_store(a_scr.at[j], 0, SUB_ROWS, a)
        _slab_store(u_scr.at[j], 0, SUB_ROWS, u)
        hs = _blocked_scan(a_scr.at[j], u_scr.at[j], 0, SUB_ROWS, carry["h"])
        carry.update(h=hs[SUB_ROWS - 1:, :])
        s["lru"] = hs * _gelu_tanh(s.pop("gr"))
        if j == subs - 1:
            h_ref[0] = hs[SUB_ROWS - SUBLANES:, :]

    def merge(j):
        st[j]["merged"] = _merged_heads(st[j].pop("attn"), st[j].pop("lru"), beta)

    def output(j):
        merged = st[j].pop("merged")
        for c in range(D_MODEL // MXU_DIM):
            cols = slice(c * MXU_DIM, (c + 1) * MXU_DIM)
            y_ref[0, rows(j), cols] = x_ref[0, rows(j), cols] + gate[:, cols] * _dot(merged, wout_ref[c])

    first_tiles = range(0, 6)
    last_tiles = range(6, IN_COLS // MXU_DIM)
    modulate(0)
    project(0, first_tiles)
    project(0, last_tiles)
    for j in range(subs):
        nxt = j + 1 < subs
        split(j)
        if j > 0:
            merge(j - 1)
            output(j - 1)
        scores(j)
        conv(j)
        if nxt:
            modulate(j + 1)
        if nxt:
            project(j + 1, first_tiles)
        softmax(j)
        attend(j)
        if nxt:
            project(j + 1, last_tiles)
        gates(j)
        recur(j)
    merge(subs - 1)
    output(subs - 1)

    last = st[subs - 1]
    k_ref[0] = carry["k"]
    v_ref[0] = carry["v"]
    conv_ref[0] = last["xr"][SUB_ROWS - SUBLANES:, :]
    _slab_store(xr_scr, 0, SUBLANES, last["xr"][SUB_ROWS - SUBLANES:, :])
    kprev[...] = carry["k"]
    vprev[...] = carry["v"]
    hcar[...] = carry["h"]


def _prompt_bias():
    i = np.arange(Q_BLOCK)[:, None]
    j = np.arange(2 * Q_BLOCK)[None, :]
    dist = Q_BLOCK + i - j
    band = (dist >= 0) & (dist <= WINDOW)
    slopes = np.asarray([2.0 ** (-8.0 * (h + 1) / N_HEADS) for h in range(N_HEADS)], np.float32)
    alibi = -(slopes[:, None, None] * dist[None].astype(np.float64)) * LOG2E
    general = np.where(band[None], alibi, np.float32(NEG_INF))
    first = np.where((band & (j >= Q_BLOCK))[None], alibi, np.float32(NEG_INF))
    per_head = np.stack([first, general]).astype(np.float32)
    pairs = [(GROUP * kv + half, GROUP * kv + half + 2) for kv in range(N_KV_HEADS) for half in range(2)]
    return np.stack([np.concatenate([per_head[:, a], per_head[:, b]], axis=1) for a, b in pairs], axis=1)


def _mix_prompt_call(x, mod, lw):
    b, t, d = x.shape
    nb = t // MIX_ROWS
    subs = MIX_ROWS // SUB_ROWS
    xspec = pl.BlockSpec((1, MIX_ROWS, d), lambda i, j: (i, j, 0))
    mspecs = _mod_specs(lw["layer"], 1, 3, lambda i, j: i, rows=SUBLANES)
    bias = jnp.asarray(_prompt_bias())
    last = lambda shape: pl.BlockSpec(shape, lambda i, j: (i, 0, 0))
    head = ["w_in", "ones", "gain"]
    tail = ["conv_w", "conv_b", "w_gate", "b_gate", "lam", "beta", "w_out"]
    outs = pl.pallas_call(
        functools.partial(_mix_prompt_kernel, layer=lw["layer"]),
        out_shape=(
            jax.ShapeDtypeStruct(x.shape, F32),
            jax.ShapeDtypeStruct((b, WINDOW, KV_W), F32),
            jax.ShapeDtypeStruct((b, WINDOW, KV_W), F32),
            jax.ShapeDtypeStruct((b, SUBLANES, LRU_W), F32),
            jax.ShapeDtypeStruct((b, SUBLANES, LRU_W), F32),
        ),
        grid=(b, nb),
        in_specs=[
            pl.BlockSpec(memory_space=pltpu.SMEM),
            xspec, *mspecs,
            *[_weight_spec(lw, key) for key in head],
            _resident(bias.shape),
            *[_weight_spec(lw, key) for key in tail],
        ],
        out_specs=(xspec, last((1, WINDOW, KV_W)), last((1, WINDOW, KV_W)),
                   last((1, SUBLANES, LRU_W)), last((1, SUBLANES, LRU_W))),
        scratch_shapes=[
            pltpu.VMEM((Q_BLOCK, KV_W), F32), pltpu.VMEM((Q_BLOCK, KV_W), F32),
            pltpu.VMEM((1, LRU_W), F32),
            pltpu.VMEM((LRU_W // LANES, SUBLANES + MIX_ROWS, LANES), F32),
            pltpu.VMEM((subs, LRU_W // LANES, _scan_rows(SUB_ROWS), LANES), F32),
            pltpu.VMEM((subs, LRU_W // LANES, _scan_rows(SUB_ROWS), LANES), F32),
        ],
        compiler_params=pltpu.CompilerParams(
            dimension_semantics=("arbitrary", "arbitrary"), vmem_limit_bytes=VMEM_LIMIT),
        name="mix_prompt",
    )(lw["sinks"], x, mod, mod, mod, *[lw[key] for key in head], bias, *[lw[key] for key in tail])
    return outs


def _mix_sample_kernel(x_ref, sh_ref, sc_ref, g_ref, ck_ref, cv_ref, cs_ref, h0_ref,
                       w_in_ref, ones_ref, gain_ref, bias_ref, sink_ref,
                       convw_ref, convb_ref, wgate_ref, bgate_ref, lam_ref, beta_ref, wout_ref,
                       y_ref, k_ref, v_ref, conv_ref, h_ref):
    sb, tq, d = x_ref.shape
    rows = sb * tq

    def flat(a):
        return jnp.broadcast_to(a, (sb, tq, a.shape[-1])).reshape(rows, a.shape[-1])

    x2 = x_ref[...].reshape(rows, d)
    z = _dot(_modulated(x2, flat(sh_ref[...]), flat(sc_ref[...])), w_in_ref[...])
    q, k, v, xr, gr = _split_projection(z, ones_ref, gain_ref[0:1])
    k3 = k.reshape(sb, tq, KV_W)
    v3 = v.reshape(sb, tq, KV_W)
    ck = ck_ref[...]
    cv = cv_ref[...]
    k_ref[...] = jnp.concatenate([ck[:, tq:, :], k3], axis=1)
    v_ref[...] = jnp.concatenate([cv[:, tq:, :], v3], axis=1)

    pad = jnp.zeros((sb, WINDOW - tq, KV_W), F32)
    kall = jnp.concatenate([ck, k3, pad], axis=1)
    vall = jnp.concatenate([cv, v3, pad], axis=1)
    low2 = lax.broadcasted_iota(jnp.int32, (rows, LANES), 1) < HALF
    zero2 = jnp.zeros((rows, LANES), F32)
    pieces = []
    for head in range(N_HEADS):
        tile, parity, kv = head // 2, head % 2, head // GROUP
        qt = q[:, tile * LANES:(tile + 1) * LANES]
        src = qt if parity == kv else pltpu.roll(qt, HALF, 1)
        piece = jnp.where(low2, src, zero2) if kv == 0 else jnp.where(low2, zero2, src)
        pieces.append(piece.reshape(sb, tq, LANES))
    qrows = jnp.concatenate(pieces, axis=1).astype(BF16)
    s = jnp.einsum("snc,sjc->snj", qrows, kall.astype(BF16), preferred_element_type=F32)
    s = s + bias_ref[...]
    p, den = _softmax_with_sink(s, sink_ref[...])
    o = jnp.einsum("snj,sjc->snc", p.astype(BF16), vall.astype(BF16), preferred_element_type=F32) / den
    tiles = []
    for tile in range(ATTN_W // LANES):
        kv = (2 * tile) // GROUP
        oe = o[:, (2 * tile) * tq:(2 * tile + 1) * tq, :].reshape(rows, LANES)
        oo = o[:, (2 * tile + 1) * tq:(2 * tile + 2) * tq, :].reshape(rows, LANES)
        if kv == 0:
            tiles.append(jnp.where(low2, oe, pltpu.roll(oo, HALF, 1)))
        else:
            tiles.append(jnp.where(low2, pltpu.roll(oe, HALF, 1), oo))
    attn = jnp.concatenate(tiles, axis=-1)

    t = lax.broadcasted_iota(jnp.int32, (sb, tq, 1), 1).reshape(rows, 1)
    state = cs_ref[...].reshape(rows, LRU_W)
    prev = {3: state, 2: pltpu.roll(state, rows - 1, 0), 1: pltpu.roll(state, rows - 2, 0)}
    delayed = [xr] + [_shift_rows(xr, kk, t, prev[kk]) for kk in (1, 2, 3)]
    xc = _conv(delayed, [convw_ref[i, 0:1] for i in range(CONV_WIDTH)], convb_ref[0:1])
    conv_ref[...] = xr.reshape(sb, tq, LRU_W)
    gates = _lru_gates(xc, wgate_ref, bgate_ref[0:1])
    a, u = _log_scan(*_lru_inputs(xc, gates, _softplus_neg(lam_ref[0:1])), t, tq)
    hs = a * flat(h0_ref[...]) + u
    h_ref[...] = hs.reshape(sb, tq, LRU_W)

    merged = _merged_heads(attn, hs * _gelu_tanh(gr), beta_ref[0:1])
    y = jnp.concatenate([_dot(merged, wout_ref[c]) for c in range(d // MXU_DIM)], axis=-1)
    y_ref[...] = (x2 + flat(g_ref[...]) * y).reshape(sb, tq, d)


def _sample_bias(tq):
    i = np.arange(tq)[:, None]
    j = np.arange(2 * WINDOW)[None, :]
    dist = WINDOW + i - j
    ok = (dist >= 0) & (dist <= WINDOW) & (j < WINDOW + tq)
    slopes = np.asarray([2.0 ** (-8.0 * (h + 1) / N_HEADS) for h in range(N_HEADS)], np.float32)
    alibi = -(slopes[:, None, None] * dist[None].astype(np.float64)) * LOG2E
    return np.where(ok[None], alibi, np.float32(NEG_INF)).reshape(N_HEADS * tq, 2 * WINDOW).astype(np.float32)


def _mix_sample_call(x, mod, k_windows, v_windows, conv_states, lru_states, lw):
    nseq, tq, d = x.shape
    sb = SAMPLE_SEQS
    layer = lw["layer"]
    seq_block = lambda shape: pl.BlockSpec(shape, lambda i: (i, 0, 0))
    layer_seq_block = lambda shape: pl.BlockSpec((None,) + shape, lambda i: (layer, i, 0, 0))
    first_block = layer * (nseq // sb)
    window_block = pl.BlockSpec((sb, WINDOW, KV_W), lambda i: (first_block + i, 0, 0))
    xspec = seq_block((sb, tq, d))
    mspecs = _mod_specs(layer, sb, 3, lambda i: i)
    bias = jnp.asarray(_sample_bias(tq))
    head = ["w_in", "ones", "gain"]
    tail = ["sink_col", "conv_w", "conv_b", "w_gate", "b_gate", "lam", "beta", "w_out"]
    return pl.pallas_call(
        _mix_sample_kernel,
        out_shape=(
            jax.ShapeDtypeStruct(x.shape, F32),
            jax.ShapeDtypeStruct(k_windows.shape, F32),
            jax.ShapeDtypeStruct(v_windows.shape, F32),
            jax.ShapeDtypeStruct((nseq, tq, LRU_W), F32),
            jax.ShapeDtypeStruct((nseq, tq, LRU_W), F32),
        ),
        grid=(nseq // sb,),
        in_specs=[
            xspec, *mspecs,
            window_block, window_block,
            layer_seq_block((sb, tq, LRU_W)), layer_seq_block((sb, 1, LRU_W)),
            *[_weight_spec(lw, key) for key in head],
            _resident(bias.shape),
            *[_weight_spec(lw, key) for key in tail],
        ],
        out_specs=(xspec, window_block, window_block,
                   seq_block((sb, tq, LRU_W)), seq_block((sb, tq, LRU_W))),
        input_output_aliases={4: 1, 5: 2},
        compiler_params=pltpu.CompilerParams(
            dimension_semantics=("arbitrary",), vmem_limit_bytes=VMEM_LIMIT),
        name="mix_sample",
    )(x, mod, mod, mod, k_windows, v_windows, conv_states, lru_states,
      *[lw[key] for key in head], bias, *[lw[key] for key in tail])


def _diagonal_tiles(w):
    depth, n, c, _ = w.shape
    per_tile = MXU_DIM // c
    tiles = w.reshape(depth, n // per_tile, per_tile, c, c)
    eye = jnp.eye(per_tile, dtype=w.dtype)
    dense = tiles[:, :, :, :, None, :] * eye[None, None, :, None, :, None]
    return dense.reshape(depth, n // per_tile, MXU_DIM, MXU_DIM)


def _rows8(v):
    return jnp.broadcast_to(v[..., None, :], v.shape[:-1] + (SUBLANES, v.shape[-1]))


def _mixer_weights(tq, w_in_bf16, q_gain, k_gain, sinks, conv_w, conv_b, w_rg, b_rg, w_ig, b_ig, lru_lambda,
                   beta_attn, beta_lru, w_out_tiles):
    head_of = np.arange(MXU_DIM) // HEAD_DIM
    ones = jnp.asarray((head_of[:, None] == head_of[None, :]) * (1.0 / HEAD_DIM), BF16)
    gain = jnp.concatenate([jnp.tile(q_gain * (HEAD_DIM ** -0.5 * LOG2E), (1, N_HEADS)),
                            jnp.tile(k_gain, (1, N_KV_HEADS))], axis=1)
    return dict(
        w_in=w_in_bf16,
        ones=ones,
        gain=_rows8(gain),
        sinks=sinks * LOG2E,
        sink_col=jnp.repeat(sinks * LOG2E, tq, axis=1)[..., None],
        conv_w=_rows8(conv_w),
        conv_b=_rows8(conv_b),
        w_gate=jnp.stack([_diagonal_tiles(w_rg), _diagonal_tiles(w_ig)], axis=1).astype(BF16),
        b_gate=_rows8(jnp.concatenate([b_rg, b_ig], axis=1)),
        lam=_rows8(lru_lambda),
        beta=_rows8(jnp.concatenate([beta_attn, beta_lru], axis=1)),
        w_out=w_out_tiles,
    )


def kernel(x_prompt, x_sample, cache_k, cache_v, state_conv, state_lru, c_prompt, c_sample, w_ada, b_ada, w1_gate, w1_up, w1_down, w_in, q_gain, k_gain, sinks, conv_w, conv_b, w_rg, b_rg, w_ig, b_ig, lru_lambda, beta_attn, beta_lru, w_out, w2_gate, w2_up, w2_down):
    nb = x_prompt.shape[0]
    ns, tq, _ = x_sample.shape
    mod = _mod_call(jnp.concatenate([c_prompt, c_sample], axis=0), w_ada, b_ada)
    mod_p = mod[:, :nb].reshape(DEPTH, nb, 1, N_MOD * D_MODEL)
    mod_p8 = _rows8(mod[:, :nb])
    mod_s = mod[:, nb:].reshape(DEPTH, ns, 1, N_MOD * D_MODEL)

    w1g, w1u, w2g, w2u = _cast_call([w1_gate, w1_up, w2_gate, w2_up], CAST_ROWS)
    w1d, w2d = _cast_call([w1_down, w2_down], D_FF // (D_MODEL // CAST_ROWS))
    (w_in_bf16,) = _cast_call([w_in], CAST_ROWS)
    w_out_tiles = _cast_column_tiles_call(w_out, CAST_ROWS)
    w1, w2 = (w1g, w1u, w1d), (w2g, w2u, w2d)

    k_windows = cache_k.reshape(DEPTH * ns, WINDOW, KV_W)
    v_windows = cache_v.reshape(DEPTH * ns, WINDOW, KV_W)
    conv_states = jnp.pad(state_conv, ((0, 0), (0, 0), (0, tq - (CONV_WIDTH - 1)), (0, 0)))
    lru_states = state_lru.reshape(DEPTH, ns, 1, LRU_W)
    weights = _mixer_weights(tq, w_in_bf16, q_gain, k_gain, sinks, conv_w, conv_b, w_rg, b_rg, w_ig, b_ig,
                             lru_lambda, beta_attn, beta_lru, w_out_tiles)
    yp, ys = x_prompt, x_sample
    outs_p, outs_s = [], []
    for l in range(DEPTH):
        lw = dict(weights, layer=l)

        yp = _ffn_call(yp, mod_p, 0, w1, l, 1, FFN_ROWS * FFN_PARTS, "ffn1_prompt")
        ys = _ffn_call(ys, mod_s, 0, w1, l, FFN_ROWS // tq, tq, "ffn1_sample")

        yp, kp, vp, cp, hp = _mix_prompt_call(yp, mod_p8, lw)
        ys, k_windows, v_windows, cs, hs = _mix_sample_call(
            ys, mod_s, k_windows, v_windows, conv_states, lru_states, lw)
        outs_p.append((kp.reshape(nb, WINDOW, N_KV_HEADS, HEAD_DIM), vp.reshape(nb, WINDOW, N_KV_HEADS, HEAD_DIM),
                       cp[:, SUBLANES - (CONV_WIDTH - 1):], hp[:, SUBLANES - 1]))
        outs_s.append((cs[:, tq - (CONV_WIDTH - 1):], hs[:, tq - 1]))

        yp = _ffn_call(yp, mod_p, 6, w2, l, 1, FFN_ROWS * FFN_PARTS, "ffn2_prompt")
        ys = _ffn_call(ys, mod_s, 6, w2, l, FFN_ROWS // tq, tq, "ffn2_sample")

    stack = lambda outs, k: jnp.stack([o[k] for o in outs])
    return (yp, ys,
            stack(outs_p, 0), stack(outs_p, 1), stack(outs_p, 2), stack(outs_p, 3),
            k_windows.reshape(cache_k.shape), v_windows.reshape(cache_v.shape),
            stack(outs_s, 0), stack(outs_s, 1))
```

```python
import functools

import numpy as np
import jax
import jax.numpy as jnp
from jax import lax
from jax.experimental import pallas as pl
from jax.experimental.pallas import tpu as pltpu

D_MODEL = 1024
DEPTH = 2
HEAD_DIM = 64
N_HEADS = 8
N_KV_HEADS = 2
GROUP = N_HEADS // N_KV_HEADS
ATTN_W = N_HEADS * HEAD_DIM
KV_W = N_KV_HEADS * HEAD_DIM
LRU_W = 512
N_LRU_BLOCKS = 8
LRU_BLOCK = LRU_W // N_LRU_BLOCKS
CONV_WIDTH = 4
RG_C = 8.0
WINDOW = 128
Q_BLOCK = 128
D_FF = 2816
N_MOD = 9
FFN_RES = 0.5
IN_COLS = ATTN_W + 2 * KV_W + 2 * LRU_W
QK_W = ATTN_W + KV_W
RMS_EPS = 1e-6
NEG_INF = -1e30
LOG2E = float(np.log2(np.e))

LANES = 128
SUBLANES = 8
HALF = LANES // 2
MXU_DIM = 256
SCAN_GROUP = 4
VMEM_LIMIT = 56 * 1024 * 1024

MOD_CHUNKS = 3
FFN_ROWS = 512
FFN_PARTS = 2
CAST_ROWS = 256
MIX_ROWS = 1024
SUB_ROWS = 256
SAMPLE_SEQS = 32

BF16 = jnp.bfloat16
F32 = jnp.float32


def _dot(a, b):
    return jnp.dot(a, b, preferred_element_type=F32)


def _dot_nt(a, b):
    return lax.dot_general(a, b, (((1,), (1,)), ((), ())), preferred_element_type=F32)


def _rms(x):
    return x * lax.rsqrt(jnp.mean(x * x, axis=-1, keepdims=True) + RMS_EPS)


def _resident(shape):
    nd = len(shape)
    return pl.BlockSpec(shape, lambda *_: (0,) * nd, pipeline_mode=pl.Buffered(1))


def _resident_layer(shape, layer):
    nd = len(shape)
    return pl.BlockSpec((None,) + tuple(shape[1:]), lambda *_: (layer,) + (0,) * (nd - 1),
                        pipeline_mode=pl.Buffered(1))


def _cast_kernel(*refs):
    n = len(refs) // 2
    for src, dst in zip(refs[:n], refs[n:]):
        dst[...] = src[...].astype(BF16)


def _cast_call(ws, rows):
    depth, r, c = ws[0].shape
    spec = pl.BlockSpec((1, rows, c), lambda l, i: (l, i, 0))
    return pl.pallas_call(
        _cast_kernel,
        out_shape=[jax.ShapeDtypeStruct(w.shape, BF16) for w in ws],
        grid=(depth, r // rows),
        in_specs=[spec] * len(ws),
        out_specs=[spec] * len(ws),
        compiler_params=pltpu.CompilerParams(
            dimension_semantics=("arbitrary", "arbitrary"), vmem_limit_bytes=VMEM_LIMIT),
        name="cast_weights",
    )(*ws)


def _cast_tiles_kernel(src, dst):
    for c in range(dst.shape[1]):
        dst[0, c] = src[0, :, c * MXU_DIM:(c + 1) * MXU_DIM].astype(BF16)


def _cast_column_tiles_call(w, rows):
    depth, k, n = w.shape
    return pl.pallas_call(
        _cast_tiles_kernel,
        out_shape=jax.ShapeDtypeStruct((depth, n // MXU_DIM, k, MXU_DIM), BF16),
        grid=(depth, k // rows),
        in_specs=[pl.BlockSpec((1, rows, n), lambda l, i: (l, i, 0))],
        out_specs=pl.BlockSpec((1, n // MXU_DIM, rows, MXU_DIM), lambda l, i: (l, 0, i, 0)),
        compiler_params=pltpu.CompilerParams(
            dimension_semantics=("arbitrary", "arbitrary"), vmem_limit_bytes=VMEM_LIMIT),
        name="cast_column_tiles",
    )(w)


def _mod_kernel(c_ref, w_ref, b_ref, o_ref):
    c = c_ref[...]
    h = (c * jax.nn.sigmoid(c)).astype(BF16)
    o_ref[0] = _dot(h, w_ref[0].astype(BF16)) + b_ref[0]


def _mod_call(c_all, w_ada, b_ada):
    n = c_all.shape[0]
    return pl.pallas_call(
        _mod_kernel,
        out_shape=jax.ShapeDtypeStruct((DEPTH, n, N_MOD * D_MODEL), F32),
        grid=(DEPTH, N_MOD // MOD_CHUNKS),
        in_specs=[
            pl.BlockSpec((n, D_MODEL), lambda l, j: (0, 0)),
            pl.BlockSpec((1, D_MODEL, MOD_CHUNKS * D_MODEL), lambda l, j: (l, 0, j)),
            pl.BlockSpec((1, 1, MOD_CHUNKS * D_MODEL), lambda l, j: (l, 0, j)),
        ],
        out_specs=pl.BlockSpec((1, n, MOD_CHUNKS * D_MODEL), lambda l, j: (l, 0, j)),
        compiler_params=pltpu.CompilerParams(
            dimension_semantics=("arbitrary", "arbitrary"), vmem_limit_bytes=VMEM_LIMIT),
        name="adaln_mod",
    )(c_all, w_ada, b_ada.reshape(DEPTH, 1, N_MOD * D_MODEL))


def _ffn_kernel(x_ref, sh_ref, sc_ref, g_ref, wg_ref, wu_ref, wd_ref, o_ref, *, parts):
    s, t, d = x_ref.shape
    along_t = s == 1
    ps, pt = (s, t // parts) if along_t else (s // parts, t)

    def piece(ref, p):
        if along_t:
            return ref[:, pl.ds(p * pt, pt), :] if ref.shape[1] == t else ref[...]
        return ref[pl.ds(p * ps, ps)]

    def modulated(p):
        h = _rms(piece(x_ref, p)) * (1.0 + piece(sc_ref, p)) + piece(sh_ref, p)
        return h.reshape(ps * pt, d).astype(BF16)

    h2 = modulated(0)
    g_next = _dot(h2, wg_ref[...])
    u_next = _dot(h2, wu_ref[...])
    for p in range(parts):
        g, u = g_next, u_next
        if p + 1 < parts:
            h2 = modulated(p + 1)
            g_next = _dot(h2, wg_ref[...])
        a = (g * jax.nn.sigmoid(g) * u).astype(BF16)
        y = _dot(a, wd_ref[...]).reshape(ps, pt, d)
        if along_t:
            o_ref[:, pl.ds(p * pt, pt), :] = piece(x_ref, p) + (FFN_RES * piece(g_ref, p)) * y
        else:
            o_ref[pl.ds(p * ps, ps)] = piece(x_ref, p) + (FFN_RES * piece(g_ref, p)) * y
        if p + 1 < parts:
            u_next = _dot(h2, wu_ref[...])


def _weight_spec(lw, key):
    if key == "ones":
        return _resident(lw[key].shape)
    return _resident_layer(lw[key].shape, lw["layer"])


def _mod_specs(layer, seqs, first_chunk, index, rows=1):
    def spec(chunk):
        return pl.BlockSpec((None, seqs, rows, D_MODEL), lambda *g: (layer, index(*g), 0, chunk))
    return [spec(first_chunk + k) for k in range(3)]


def _ffn_call(x, mod, first_chunk, weights, layer, seqs, rows, name):
    wg, wu, wd = weights
    nseq, t, d = x.shape
    parts = seqs * rows // FFN_ROWS
    grid = (nseq // seqs, t // rows)
    xspec = pl.BlockSpec((seqs, rows, d), lambda i, j: (i, j, 0))
    mspecs = _mod_specs(layer, seqs, first_chunk, lambda i, j: i)
    return pl.pallas_call(
        functools.partial(_ffn_kernel, parts=parts),
        out_shape=jax.ShapeDtypeStruct(x.shape, F32),
        grid=grid,
        in_specs=[xspec, *mspecs, *[_resident_layer(w.shape, layer) for w in weights]],
        out_specs=xspec,
        compiler_params=pltpu.CompilerParams(
            dimension_semantics=("arbitrary", "arbitrary"), vmem_limit_bytes=VMEM_LIMIT),
        name=name,
    )(x, mod, mod, mod, wg, wu, wd)


def _head_mean_square(qk, avg_ref):
    parts = []
    for c0 in range(0, qk.shape[1], MXU_DIM):
        w = min(MXU_DIM, qk.shape[1] - c0)
        sq = qk[:, c0:c0 + w] * qk[:, c0:c0 + w]
        parts.append(_dot(sq.astype(BF16), avg_ref[:w, :w]))
    return jnp.concatenate(parts, axis=-1)


def _modulated(x2, sh, sc):
    return (_rms(x2) * (1.0 + sc) + sh).astype(BF16)


def _split_projection(z, ones_ref, gain):
    qk = z[:, :QK_W]
    qkn = qk * lax.rsqrt(_head_mean_square(qk, ones_ref) + RMS_EPS) * gain
    q = qkn[:, :ATTN_W]
    k = qkn[:, ATTN_W:QK_W]
    v = z[:, QK_W:QK_W + KV_W]
    xr = z[:, QK_W + KV_W:QK_W + KV_W + LRU_W]
    gr = z[:, QK_W + KV_W + LRU_W:]
    return q, k, v, xr, gr


def _shift_rows(x, k, t, fill):
    return jnp.where(t >= k, pltpu.roll(x, k, 0), fill)


def _conv(delayed, taps, bias):
    y = bias + delayed[3] * taps[0]
    y = y + delayed[2] * taps[1]
    y = y + delayed[1] * taps[2]
    return y + delayed[0] * taps[3]


def _lru_gates(xc, wgate_ref, bias):
    xb = xc.astype(BF16)
    cols = [_dot(xb[:, t * MXU_DIM:(t + 1) * MXU_DIM], wgate_ref[gate, t])
            for gate in range(2) for t in range(LRU_W // MXU_DIM)]
    return jnp.concatenate(cols, axis=-1) + bias


def _softplus_neg(lam):
    return jnp.maximum(-lam, 0.0) + jnp.log1p(jnp.exp(-jnp.abs(lam)))


def _lru_inputs(xc, gates, softplus):
    r = jax.nn.sigmoid(gates[:, :LRU_W])
    gi = jax.nn.sigmoid(gates[:, LRU_W:])
    log_a = (-RG_C * r) * softplus
    a = jnp.exp(log_a)
    w = -jnp.tanh(log_a) * (a * a + 1.0)
    u = jnp.where(w > 0.0, w * lax.rsqrt(w), 0.0) * (gi * xc)
    return a, u


def _log_scan(a, u, t, period):
    s = 1
    while s < period:
        a_prev = _shift_rows(a, s, t, 1.0)
        u_prev = _shift_rows(u, s, t, 0.0)
        u = u + a * u_prev
        a = a * a_prev
        s *= 2
    return a, u


def _slab_load(ref, start, n, stride=1):
    rows = pl.ds(start, n) if stride == 1 else pl.ds(start, n, stride=stride)
    return jnp.concatenate([ref[s, rows, :] for s in range(ref.shape[0])], axis=-1)


def _slab_store(ref, start, n, val, stride=1):
    rows = pl.ds(start, n) if stride == 1 else pl.ds(start, n, stride=stride)
    for s in range(ref.shape[0]):
        ref[s, rows, :] = val[:, s * LANES:(s + 1) * LANES]


def _blocked_scan(a_scr, u_scr, base, n, h_init):
    if n <= 2 * SUBLANES:
        t = lax.broadcasted_iota(jnp.int32, (n, 1), 0)
        a, u = _log_scan(_slab_load(a_scr, base, n), _slab_load(u_scr, base, n), t, n)
        return a * h_init + u
    m = n // SCAN_GROUP
    a_loc, h_loc = [], []
    for r in range(SCAN_GROUP):
        a_r = _slab_load(a_scr, base + r, m, SCAN_GROUP)
        u_r = _slab_load(u_scr, base + r, m, SCAN_GROUP)
        h_loc.append(u_r if r == 0 else a_r * h_loc[-1] + u_r)
        a_loc.append(a_r if r == 0 else a_r * a_loc[-1])
    _slab_store(a_scr, base + n, m, a_loc[-1])
    _slab_store(u_scr, base + n, m, h_loc[-1])
    ends = _blocked_scan(a_scr, u_scr, base + n, m, h_init)
    g = lax.broadcasted_iota(jnp.int32, (m, 1), 0)
    carry = jnp.where(g >= 1, pltpu.roll(ends, 1, 0), h_init)
    for r in range(SCAN_GROUP):
        _slab_store(u_scr, base + r, m, a_loc[r] * carry + h_loc[r], SCAN_GROUP)
    return _slab_load(u_scr, base, n)


def _scan_rows(n):
    return n if n <= 2 * SUBLANES else n + _scan_rows(n // SCAN_GROUP)


def _gelu_tanh(x):
    return 0.5 * x * (1.0 + jnp.tanh(np.sqrt(2.0 / np.pi) * (x + 0.044715 * (x * x * x))))


def _merged_heads(attn, lru, beta):
    return (jnp.concatenate([_rms(attn), _rms(lru)], axis=-1) * beta).astype(BF16)


def _softmax_with_sink(s, sink):
    m = jnp.maximum(jnp.max(s, axis=-1, keepdims=True), sink)
    p = jnp.exp2(s - m)
    den = jnp.sum(p, axis=-1, keepdims=True) + jnp.exp2(sink - m)
    return p, den


def _half_variants(x, low):
    xs = pltpu.roll(x, HALF, x.ndim - 1)
    zero = jnp.zeros_like(x)
    return (
        (jnp.where(low, x, zero).astype(BF16), jnp.where(low, zero, xs).astype(BF16)),
        (jnp.where(low, xs, zero).astype(BF16), jnp.where(low, zero, x).astype(BF16)),
    )


def _mix_prompt_kernel(sinks_ref, x_ref, sh_ref, sc_ref, g_ref, w_in_ref, ones_ref, gain_ref, bias_ref,
                       convw_ref, convb_ref, wgate_ref, bgate_ref, lam_ref, beta_ref, wout_ref,
                       y_ref, k_ref, v_ref, conv_ref, h_ref,
                       kprev, vprev, hcar, xr_scr, a_scr, u_scr, *, layer):
    n = pl.program_id(1)

    @pl.when(n == 0)
    def _():
        kprev[...] = jnp.zeros_like(kprev)
        vprev[...] = jnp.zeros_like(vprev)
        hcar[...] = jnp.zeros_like(hcar)
        xr_scr[:, 0:SUBLANES, :] = jnp.zeros((xr_scr.shape[0], SUBLANES, LANES), F32)

    subs = x_ref.shape[1] // SUB_ROWS
    low = lax.broadcasted_iota(jnp.int32, (Q_BLOCK + SUB_ROWS, LANES), 1) < HALF
    second_head = lax.broadcasted_iota(jnp.int32, (2 * Q_BLOCK, 1), 0) >= Q_BLOCK
    carry = dict(k=kprev[...], v=vprev[...], h=hcar[...])
    st = [dict() for _ in range(subs)]

    def rep(r8):
        return jnp.broadcast_to(r8[None], (SUB_ROWS // SUBLANES,) + r8.shape).reshape(SUB_ROWS, r8.shape[-1])

    shift, scale1, gate = rep(sh_ref[0]), rep(1.0 + sc_ref[0]), rep(g_ref[0])
    gain, beta, gate_bias = rep(gain_ref[...]), rep(beta_ref[...]), rep(bgate_ref[...])
    taps, conv_bias = [rep(convw_ref[i]) for i in range(CONV_WIDTH)], rep(convb_ref[...])
    softplus = rep(_softplus_neg(lam_ref[...]))

    def rows(j):
        return pl.ds(j * SUB_ROWS, SUB_ROWS)

    def modulate(j):
        st[j]["h"] = (_rms(x_ref[0, rows(j), :]) * scale1 + shift).astype(BF16)
        st[j]["z"] = []

    def project(j, tiles):
        for c in tiles:
            st[j]["z"].append(_dot(st[j]["h"], w_in_ref[:, c * MXU_DIM:(c + 1) * MXU_DIM]))

    def split(j):
        z = jnp.concatenate(st[j].pop("z"), axis=-1)
        q, k, v, xr, gr = _split_projection(z, ones_ref, gain)
        st[j].update(q=q, k=k, v=v, xr=xr, gr=gr)

    pairs = [(kv, half) for kv in range(N_KV_HEADS) for half in range(2)]

    def scores(j):
        s = st[j]
        kvar = _half_variants(jnp.concatenate([carry["k"], s["k"]], axis=0), low)
        s["vvar"] = _half_variants(jnp.concatenate([carry["v"], s["v"]], axis=0), low)
        carry.update(k=s["k"][SUB_ROWS - Q_BLOCK:], v=s["v"][SUB_ROWS - Q_BLOCK:])
        s["s"] = []
        for qb in range(SUB_ROWS // Q_BLOCK):
            bias_row = jnp.minimum(n, 1) if (j == 0 and qb == 0) else 1
            q = s["q"][qb * Q_BLOCK:(qb + 1) * Q_BLOCK]
            for kv, half in pairs:
                stacked = jnp.concatenate([q[:, (2 * kv) * LANES:(2 * kv + 1) * LANES],
                                           q[:, (2 * kv + 1) * LANES:(2 * kv + 2) * LANES]], axis=0)
                keys = kvar[kv][half][qb * Q_BLOCK:(qb + 2) * Q_BLOCK]
                s["s"].append(_dot_nt(stacked.astype(BF16), keys) + bias_ref[bias_row, 2 * kv + half])

    def softmax(j):
        s = st[j]
        s["p"] = []
        for i, sc in enumerate(s.pop("s")):
            kv, half = pairs[i % len(pairs)]
            first, second = GROUP * kv + half, GROUP * kv + half + 2
            sink = jnp.where(second_head, sinks_ref[layer, second], sinks_ref[layer, first])
            s["p"].append(_softmax_with_sink(sc, sink))

    def attend(j):
        s = st[j]
        vvar = s.pop("vvar")
        blocks = []
        for qb in range(SUB_ROWS // Q_BLOCK):
            out = {}
            for kv, half in pairs:
                p, den = s["p"][qb * len(pairs) + 2 * kv + half]
                out[kv, half] = _dot(p.astype(BF16), vvar[kv][half][qb * Q_BLOCK:(qb + 2) * Q_BLOCK]) / den
            tiles = []
            for kv in range(N_KV_HEADS):
                both = out[kv, 0] + out[kv, 1]
                tiles += [both[:Q_BLOCK], both[Q_BLOCK:]]
            blocks.append(jnp.concatenate(tiles, axis=-1))
        s.pop("p")
        s["attn"] = jnp.concatenate(blocks, axis=0)

    def conv(j):
        first = SUBLANES + j * SUB_ROWS
        _slab_store(xr_scr, first, SUB_ROWS, st[j]["xr"])
        st[j]["xc"] = _conv([_slab_load(xr_scr, first - kk, SUB_ROWS) for kk in range(CONV_WIDTH)],
                            taps, conv_bias)

    def gates(j):
        st[j]["gates"] = _lru_gates(st[j]["xc"], wgate_ref, gate_bias)

    def recur(j):
        s = st[j]
        a, u = _lru_inputs(s.pop("xc"), s.pop("gates"), softplus)
        _slab_store(a_scr.at[j], 0, SUB_ROWS, a)
        _slab_store(u_scr.at[j], 0, SUB_ROWS, u)
        hs = _blocked_scan(a_scr.at[j], u_scr.at[j], 0, SUB_ROWS, carry["h"])
        carry.update(h=hs[SUB_ROWS - 1:, :])
        s["lru"] = hs * _gelu_tanh(s.pop("gr"))
        if j == subs - 1:
            h_ref[0] = hs[SUB_ROWS - SUBLANES:, :]

    def merge(j):
        st[j]["merged"] = _merged_heads(st[j].pop("attn"), st[j].pop("lru"), beta)

    def output(j):
        merged = st[j].pop("merged")
        for c in range(D_MODEL // MXU_DIM):
            cols = slice(c * MXU_DIM, (c + 1) * MXU_DIM)
            y_ref[0, rows(j), cols] = x_ref[0, rows(j), cols] + gate[:, cols] * _dot(merged, wout_ref[c])

    first_tiles = range(0, 6)
    last_tiles = range(6, IN_COLS // MXU_DIM)
    modulate(0)
    project(0, first_tiles)
    project(0, last_tiles)
    for j in range(subs):
        nxt = j + 1 < subs
        split(j)
        if j > 0:
            merge(j - 1)
            output(j - 1)
        scores(j)
        conv(j)
        if nxt:
            modulate(j + 1)
        if nxt:
            project(j + 1, first_tiles)
        softmax(j)
        attend(j)
        gates(j)
        if nxt:
            project(j + 1, last_tiles)
        recur(j)
    merge(subs - 1)
    output(subs - 1)

    last = st[subs - 1]
    k_ref[0] = carry["k"]
    v_ref[0] = carry["v"]
    conv_ref[0] = last["xr"][SUB_ROWS - SUBLANES:, :]
    _slab_store(xr_scr, 0, SUBLANES, last["xr"][SUB_ROWS - SUBLANES:, :])
    kprev[...] = carry["k"]
    vprev[...] = carry["v"]
    hcar[...] = carry["h"]


def _prompt_bias():
    i = np.arange(Q_BLOCK)[:, None]
    j = np.arange(2 * Q_BLOCK)[None, :]
    dist = Q_BLOCK + i - j
    band = (dist >= 0) & (dist <= WINDOW)
    slopes = np.asarray([2.0 ** (-8.0 * (h + 1) / N_HEADS) for h in range(N_HEADS)], np.float32)
    alibi = -(slopes[:, None, None] * dist[None].astype(np.float64)) * LOG2E
    general = np.where(band[None], alibi, np.float32(NEG_INF))
    first = np.where((band & (j >= Q_BLOCK))[None], alibi, np.float32(NEG_INF))
    per_head = np.stack([first, general]).astype(np.float32)
    pairs = [(GROUP * kv + half, GROUP * kv + half + 2) for kv in range(N_KV_HEADS) for half in range(2)]
    return np.stack([np.concatenate([per_head[:, a], per_head[:, b]], axis=1) for a, b in pairs], axis=1)


def _mix_prompt_call(x, mod, lw):
    b, t, d = x.shape
    nb = t // MIX_ROWS
    subs = MIX_ROWS // SUB_ROWS
    xspec = pl.BlockSpec((1, MIX_ROWS, d), lambda i, j: (i, j, 0))
    mspecs = _mod_specs(lw["layer"], 1, 3, lambda i, j: i, rows=SUBLANES)
    bias = jnp.asarray(_prompt_bias())
    last = lambda shape: pl.BlockSpec(shape, lambda i, j: (i, 0, 0))
    head = ["w_in", "ones", "gain"]
    tail = ["conv_w", "conv_b", "w_gate", "b_gate", "lam", "beta", "w_out"]
    outs = pl.pallas_call(
        functools.partial(_mix_prompt_kernel, layer=lw["layer"]),
        out_shape=(
            jax.ShapeDtypeStruct(x.shape, F32),
            jax.ShapeDtypeStruct((b, WINDOW, KV_W), F32),
            jax.ShapeDtypeStruct((b, WINDOW, KV_W), F32),
            jax.ShapeDtypeStruct((b, SUBLANES, LRU_W), F32),
            jax.ShapeDtypeStruct((b, SUBLANES, LRU_W), F32),
        ),
        grid=(b, nb),
        in_specs=[
            pl.BlockSpec(memory_space=pltpu.SMEM),
            xspec, *mspecs,
            *[_weight_spec(lw, key) for key in head],
            _resident(bias.shape),
            *[_weight_spec(lw, key) for key in tail],
        ],
        out_specs=(xspec, last((1, WINDOW, KV_W)), last((1, WINDOW, KV_W)),
                   last((1, SUBLANES, LRU_W)), last((1, SUBLANES, LRU_W))),
        scratch_shapes=[
            pltpu.VMEM((Q_BLOCK, KV_W), F32), pltpu.VMEM((Q_BLOCK, KV_W), F32),
            pltpu.VMEM((1, LRU_W), F32),
            pltpu.VMEM((LRU_W // LANES, SUBLANES + MIX_ROWS, LANES), F32),
            pltpu.VMEM((subs, LRU_W // LANES, _scan_rows(SUB_ROWS), LANES), F32),
            pltpu.VMEM((subs, LRU_W // LANES, _scan_rows(SUB_ROWS), LANES), F32),
        ],
        compiler_params=pltpu.CompilerParams(
            dimension_semantics=("arbitrary", "arbitrary"), vmem_limit_bytes=VMEM_LIMIT),
        name="mix_prompt",
    )(lw["sinks"], x, mod, mod, mod, *[lw[key] for key in head], bias, *[lw[key] for key in tail])
    return outs


def _mix_sample_kernel(x_ref, sh_ref, sc_ref, g_ref, ck_ref, cv_ref, cs_ref, h0_ref,
                       w_in_ref, ones_ref, gain_ref, bias_ref, sink_ref,
                       convw_ref, convb_ref, wgate_ref, bgate_ref, lam_ref, beta_ref, wout_ref,
                       y_ref, k_ref, v_ref, conv_ref, h_ref):
    sb, tq, d = x_ref.shape
    rows = sb * tq

    def flat(a):
        return jnp.broadcast_to(a, (sb, tq, a.shape[-1])).reshape(rows, a.shape[-1])

    x2 = x_ref[...].reshape(rows, d)
    z = _dot(_modulated(x2, flat(sh_ref[...]), flat(sc_ref[...])), w_in_ref[...])
    q, k, v, xr, gr = _split_projection(z, ones_ref, gain_ref[0:1])
    k3 = k.reshape(sb, tq, KV_W)
    v3 = v.reshape(sb, tq, KV_W)
    ck = ck_ref[...]
    cv = cv_ref[...]
    k_ref[...] = jnp.concatenate([ck[:, tq:, :], k3], axis=1)
    v_ref[...] = jnp.concatenate([cv[:, tq:, :], v3], axis=1)

    pad = jnp.zeros((sb, WINDOW - tq, KV_W), F32)
    kall = jnp.concatenate([ck, k3, pad], axis=1)
    vall = jnp.concatenate([cv, v3, pad], axis=1)
    low2 = lax.broadcasted_iota(jnp.int32, (rows, LANES), 1) < HALF
    zero2 = jnp.zeros((rows, LANES), F32)
    pieces = []
    for head in range(N_HEADS):
        tile, parity, kv = head // 2, head % 2, head // GROUP
        qt = q[:, tile * LANES:(tile + 1) * LANES]
        src = qt if parity == kv else pltpu.roll(qt, HALF, 1)
        piece = jnp.where(low2, src, zero2) if kv == 0 else jnp.where(low2, zero2, src)
        pieces.append(piece.reshape(sb, tq, LANES))
    qrows = jnp.concatenate(pieces, axis=1).astype(BF16)
    s = jnp.einsum("snc,sjc->snj", qrows, kall.astype(BF16), preferred_element_type=F32)
    s = s + bias_ref[...]
    p, den = _softmax_with_sink(s, sink_ref[...])
    o = jnp.einsum("snj,sjc->snc", p.astype(BF16), vall.astype(BF16), preferred_element_type=F32) / den
    tiles = []
    for tile in range(ATTN_W // LANES):
        kv = (2 * tile) // GROUP
        oe = o[:, (2 * tile) * tq:(2 * tile + 1) * tq, :].reshape(rows, LANES)
        oo = o[:, (2 * tile + 1) * tq:(2 * tile + 2) * tq, :].reshape(rows, LANES)
        if kv == 0:
            tiles.append(jnp.where(low2, oe, pltpu.roll(oo, HALF, 1)))
        else:
            tiles.append(jnp.where(low2, pltpu.roll(oe, HALF, 1), oo))
    attn = jnp.concatenate(tiles, axis=-1)

    t = lax.broadcasted_iota(jnp.int32, (sb, tq, 1), 1).reshape(rows, 1)
    state = cs_ref[...].reshape(rows, LRU_W)
    prev = {3: state, 2: pltpu.roll(state, rows - 1, 0), 1: pltpu.roll(state, rows - 2, 0)}
    delayed = [xr] + [_shift_rows(xr, kk, t, prev[kk]) for kk in (1, 2, 3)]
    xc = _conv(delayed, [convw_ref[i, 0:1] for i in range(CONV_WIDTH)], convb_ref[0:1])
    conv_ref[...] = xr.reshape(sb, tq, LRU_W)
    gates = _lru_gates(xc, wgate_ref, bgate_ref[0:1])
    a, u = _log_scan(*_lru_inputs(xc, gates, _softplus_neg(lam_ref[0:1])), t, tq)
    hs = a * flat(h0_ref[...]) + u
    h_ref[...] = hs.reshape(sb, tq, LRU_W)

    merged = _merged_heads(attn, hs * _gelu_tanh(gr), beta_ref[0:1])
    y = jnp.concatenate([_dot(merged, wout_ref[c]) for c in range(d // MXU_DIM)], axis=-1)
    y_ref[...] = (x2 + flat(g_ref[...]) * y).reshape(sb, tq, d)


def _sample_bias(tq):
    i = np.arange(tq)[:, None]
    j = np.arange(2 * WINDOW)[None, :]
    dist = WINDOW + i - j
    ok = (dist >= 0) & (dist <= WINDOW) & (j < WINDOW + tq)
    slopes = np.asarray([2.0 ** (-8.0 * (h + 1) / N_HEADS) for h in range(N_HEADS)], np.float32)
    alibi = -(slopes[:, None, None] * dist[None].astype(np.float64)) * LOG2E
    return np.where(ok[None], alibi, np.float32(NEG_INF)).reshape(N_HEADS * tq, 2 * WINDOW).astype(np.float32)


def _mix_sample_call(x, mod, k_windows, v_windows, conv_states, lru_states, lw):
    nseq, tq, d = x.shape
    sb = SAMPLE_SEQS
    layer = lw["layer"]
    seq_block = lambda shape: pl.BlockSpec(shape, lambda i: (i, 0, 0))
    layer_seq_block = lambda shape: pl.BlockSpec((None,) + shape, lambda i: (layer, i, 0, 0))
    first_block = layer * (nseq // sb)
    window_block = pl.BlockSpec((sb, WINDOW, KV_W), lambda i: (first_block + i, 0, 0))
    xspec = seq_block((sb, tq, d))
    mspecs = _mod_specs(layer, sb, 3, lambda i: i)
    bias = jnp.asarray(_sample_bias(tq))
    head = ["w_in", "ones", "gain"]
    tail = ["sink_col", "conv_w", "conv_b", "w_gate", "b_gate", "lam", "beta", "w_out"]
    return pl.pallas_call(
        _mix_sample_kernel,
        out_shape=(
            jax.ShapeDtypeStruct(x.shape, F32),
            jax.ShapeDtypeStruct(k_windows.shape, F32),
            jax.ShapeDtypeStruct(v_windows.shape, F32),
            jax.ShapeDtypeStruct((nseq, tq, LRU_W), F32),
            jax.ShapeDtypeStruct((nseq, tq, LRU_W), F32),
        ),
        grid=(nseq // sb,),
        in_specs=[
            xspec, *mspecs,
            window_block, window_block,
            layer_seq_block((sb, tq, LRU_W)), layer_seq_block((sb, 1, LRU_W)),
            *[_weight_spec(lw, key) for key in head],
            _resident(bias.shape),
            *[_weight_spec(lw, key) for key in tail],
        ],
        out_specs=(xspec, window_block, window_block,
                   seq_block((sb, tq, LRU_W)), seq_block((sb, tq, LRU_W))),
        input_output_aliases={4: 1, 5: 2},
        compiler_params=pltpu.CompilerParams(
            dimension_semantics=("arbitrary",), vmem_limit_bytes=VMEM_LIMIT),
        name="mix_sample",
    )(x, mod, mod, mod, k_windows, v_windows, conv_states, lru_states,
      *[lw[key] for key in head], bias, *[lw[key] for key in tail])


def _diagonal_tiles(w):
    depth, n, c, _ = w.shape
    per_tile = MXU_DIM // c
    tiles = w.reshape(depth, n // per_tile, per_tile, c, c)
    eye = jnp.eye(per_tile, dtype=w.dtype)
    dense = tiles[:, :, :, :, None, :] * eye[None, None, :, None, :, None]
    return dense.reshape(depth, n // per_tile, MXU_DIM, MXU_DIM)


def _rows8(v):
    return jnp.broadcast_to(v[..., None, :], v.shape[:-1] + (SUBLANES, v.shape[-1]))


def _mixer_weights(tq, w_in_bf16, q_gain, k_gain, sinks, conv_w, conv_b, w_rg, b_rg, w_ig, b_ig, lru_lambda,
                   beta_attn, beta_lru, w_out_tiles):
    head_of = np.arange(MXU_DIM) // HEAD_DIM
    ones = jnp.asarray((head_of[:, None] == head_of[None, :]) * (1.0 / HEAD_DIM), BF16)
    gain = jnp.concatenate([jnp.tile(q_gain * (HEAD_DIM ** -0.5 * LOG2E), (1, N_HEADS)),
                            jnp.tile(k_gain, (1, N_KV_HEADS))], axis=1)
    return dict(
        w_in=w_in_bf16,
        ones=ones,
        gain=_rows8(gain),
        sinks=sinks * LOG2E,
        sink_col=jnp.repeat(sinks * LOG2E, tq, axis=1)[..., None],
        conv_w=_rows8(conv_w),
        conv_b=_rows8(conv_b),
        w_gate=jnp.stack([_diagonal_tiles(w_rg), _diagonal_tiles(w_ig)], axis=1).astype(BF16),
        b_gate=_rows8(jnp.concatenate([b_rg, b_ig], axis=1)),
        lam=_rows8(lru_lambda),
        beta=_rows8(jnp.concatenate([beta_attn, beta_lru], axis=1)),
        w_out=w_out_tiles,
    )


def kernel(x_prompt, x_sample, cache_k, cache_v, state_conv, state_lru, c_prompt, c_sample, w_ada, b_ada, w1_gate, w1_up, w1_down, w_in, q_gain, k_gain, sinks, conv_w, conv_b, w_rg, b_rg, w_ig, b_ig, lru_lambda, beta_attn, beta_lru, w_out, w2_gate, w2_up, w2_down):
    nb = x_prompt.shape[0]
    ns, tq, _ = x_sample.shape
    mod = _mod_call(jnp.concatenate([c_prompt, c_sample], axis=0), w_ada, b_ada)
    mod_p = mod[:, :nb].reshape(DEPTH, nb, 1, N_MOD * D_MODEL)
    mod_p8 = _rows8(mod[:, :nb])
    mod_s = mod[:, nb:].reshape(DEPTH, ns, 1, N_MOD * D_MODEL)

    w1g, w1u, w2g, w2u = _cast_call([w1_gate, w1_up, w2_gate, w2_up], CAST_ROWS)
    w1d, w2d = _cast_call([w1_down, w2_down], D_FF // (D_MODEL // CAST_ROWS))
    (w_in_bf16,) = _cast_call([w_in], CAST_ROWS)
    w_out_tiles = _cast_column_tiles_call(w_out, CAST_ROWS)
    w1, w2 = (w1g, w1u, w1d), (w2g, w2u, w2d)

    k_windows = cache_k.reshape(DEPTH * ns, WINDOW, KV_W)
    v_windows = cache_v.reshape(DEPTH * ns, WINDOW, KV_W)
    conv_states = jnp.pad(state_conv, ((0, 0), (0, 0), (0, tq - (CONV_WIDTH - 1)), (0, 0)))
    lru_states = state_lru.reshape(DEPTH, ns, 1, LRU_W)
    weights = _mixer_weights(tq, w_in_bf16, q_gain, k_gain, sinks, conv_w, conv_b, w_rg, b_rg, w_ig, b_ig,
                             lru_lambda, beta_attn, beta_lru, w_out_tiles)
    yp, ys = x_prompt, x_sample
    outs_p, outs_s = [], []
    for l in range(DEPTH):
        lw = dict(weights, layer=l)

        yp = _ffn_call(yp, mod_p, 0, w1, l, 1, FFN_ROWS * FFN_PARTS, "ffn1_prompt")
        ys = _ffn_call(ys, mod_s, 0, w1, l, FFN_ROWS // tq, tq, "ffn1_sample")

        yp, kp, vp, cp, hp = _mix_prompt_call(yp, mod_p8, lw)
        ys, k_windows, v_windows, cs, hs = _mix_sample_call(
            ys, mod_s, k_windows, v_windows, conv_states, lru_states, lw)
        outs_p.append((kp.reshape(nb, WINDOW, N_KV_HEADS, HEAD_DIM), vp.reshape(nb, WINDOW, N_KV_HEADS, HEAD_DIM),
                       cp[:, SUBLANES - (CONV_WIDTH - 1):], hp[:, SUBLANES - 1]))
        outs_s.append((cs[:, tq - (CONV_WIDTH - 1):], hs[:, tq - 1]))

        yp = _ffn_call(yp, mod_p, 6, w2, l, 1, FFN_ROWS * FFN_PARTS, "ffn2_prompt")
        ys = _ffn_call(ys, mod_s, 6, w2, l, FFN_ROWS // tq, tq, "ffn2_sample")

    stack = lambda outs, k: jnp.stack([o[k] for o in outs])
    return (yp, ys,
            stack(outs_p, 0), stack(outs_p, 1), stack(outs_p, 2), stack(outs_p, 3),
            k_windows.reshape(cache_k.shape), v_windows.reshape(cache_v.shape),
            stack(outs_s, 0), stack(outs_s, 1))
```
